```python
import math
import jax, jax.numpy as jnp
from jax import lax
import numpy as np

D_MODEL = 4096
BATCH = 4
SEQ = 2048
DEPTH = 2
DEC_BATCH = 128
DEC_SEQ = 8
PAST_LEN = 16384
PAGE_SIZE = 128

N_META = 16
CONV_W = 4
CHUNK = 128
EPS = 1e-6
SSD_D_INNER = D_MODEL
SSD_HEAD_DIM = 64
SSD_HEADS = SSD_D_INNER // SSD_HEAD_DIM
SSD_GROUPS = 8
SSD_HPG = SSD_HEADS // SSD_GROUPS
SSD_STATE = 128
SSD_CONV_DIM = SSD_D_INNER + 2 * SSD_GROUPS * SSD_STATE
RET_HEADS = 16
RET_QK_DIM = 256
RET_V_DIM = 256
RET_QK = RET_HEADS * RET_QK_DIM
RET_VW = RET_HEADS * RET_V_DIM
ROPE_BASE = 10000.0
L0_IN = SSD_D_INNER + SSD_CONV_DIM + SSD_HEADS + 2 * RET_QK + 2 * RET_VW
L0_MIX = SSD_D_INNER + RET_VW
LRU_WIDTH = 5120
LRU_BLOCKS = 16
LRU_BW = LRU_WIDTH // LRU_BLOCKS
LRU_C = 8.0
MOE_GROUPS = 4
MOE_PER_GROUP = 8
N_EXPERTS = MOE_GROUPS * MOE_PER_GROUP
EXPERT_FF = 512
MOE_TOPK = 2

kernel_name = 'hybrid_ssd_retention_rglru_hmoe_step'


def rmsnorm(x, g):
    xf = x.astype(jnp.float32)
    y = xf * lax.rsqrt(jnp.mean(xf * xf, axis=-1, keepdims=True) + EPS)
    return y.astype(x.dtype) * g


def group_rms(x):
    xf = x.astype(jnp.float32)
    return xf * lax.rsqrt(jnp.mean(xf * xf, axis=-1, keepdims=True) + EPS)


def causal_conv(u, buf, w, b):
    full = jnp.concatenate([buf.astype(u.dtype), u], axis=1)
    l = u.shape[1]
    y = b + w[0] * full[:, 0:l]
    for t in range(1, CONV_W):
        y = y + w[t] * full[:, t:t + l]
    return y, full[:, -(CONV_W - 1):]


def rotary(x, positions):
    half = x.shape[-1] // 2
    inv = 1.0 / (ROPE_BASE ** (jnp.arange(half, dtype=jnp.float32) / half))
    ang = positions.astype(jnp.float32)[:, None] * inv[None, :]
    cos = jnp.cos(ang)[None, :, None, :]
    sin = jnp.sin(ang)[None, :, None, :]
    xf = x.astype(jnp.float32)
    x1, x2 = xf[..., :half], xf[..., half:]
    return jnp.concatenate([x1 * cos - x2 * sin, x1 * sin + x2 * cos], axis=-1)


def chunked_decay_scan(q, k, v, log_a, s0, chunk):
    f32 = jnp.float32
    b, l, g, n = q.shape
    j, p = v.shape[-2:]
    c = l // chunk
    q = q.astype(f32).reshape(b, c, chunk, g, n)
    k = k.astype(f32).reshape(b, c, chunk, g, n)
    v = v.astype(f32).reshape(b, c, chunk, g, j, p)
    cum = jnp.cumsum(log_a.astype(f32).reshape(b, c, chunk, g, j), axis=2)
    cum_t = jnp.moveaxis(cum, 2, -1)
    causal = jnp.tril(jnp.ones((chunk, chunk), dtype=bool))
    decay = jnp.exp(jnp.where(causal, cum_t[..., :, None] - cum_t[..., None, :], -jnp.inf))
    scores = jnp.einsum('bctgn,bcsgn->bcgts', q, k)
    y_intra = jnp.einsum('bcgjts,bcsgjp->bctgjp', scores[:, :, :, None] * decay, v)
    kv_w = v * jnp.exp(cum[:, :, -1:] - cum)[..., None]
    chunk_states = jnp.einsum('bcsgn,bcsgjp->bcgjnp', k, kv_w)
    chunk_decay = jnp.exp(cum[:, :, -1])

    def step(s, inp):
        dcy, cs = inp
        return dcy[..., None, None] * s + cs, s

    s_final, s_starts = lax.scan(step, s0.astype(f32),
                                 (jnp.moveaxis(chunk_decay, 1, 0), jnp.moveaxis(chunk_states, 1, 0)))
    s_starts = jnp.moveaxis(s_starts, 0, 1)
    y_inter = jnp.einsum('bctgn,bcgjnp->bctgjp', q, s_starts) * jnp.exp(cum)[..., None]
    return (y_intra + y_inter).reshape(b, l, g, j, p), s_final


def run_segments(q, k, v, log_a, s0, segments):
    ys = []
    s = s0
    for start, length, chunk in segments:
        sl = slice(start, start + length)
        y, s = chunked_decay_scan(q[:, sl], k[:, sl], v[:, sl], log_a[:, sl], s, chunk)
        ys.append(y)
    return jnp.concatenate(ys, axis=1), s


def linear_recurrence(a, b, h0):
    b = b.at[:, 0].add(a[:, 0] * h0)

    def combine(x, y):
        a1, b1 = x
        a2, b2 = y
        return a1 * a2, a2 * b1 + b2

    _, h = lax.associative_scan(combine, (a, b), axis=1)
    return h


def mixer_ssd_ret(u, conv_buf, ssd_s0, ret_s0, positions, segments, p):
    f32 = jnp.float32
    b, l, _ = u.shape
    sizes = [SSD_D_INNER, SSD_CONV_DIM, SSD_HEADS, RET_QK, RET_QK, RET_VW, RET_VW]
    cuts = [int(c) for c in np.cumsum(sizes)[:-1]]
    z, xbc, dt, q, k, v, g = jnp.split(u @ p['w_in0'], cuts, axis=-1)
    xbc, new_conv = causal_conv(xbc, conv_buf, p['ssd_conv_w'], p['ssd_conv_b'])
    xbc = jax.nn.silu(xbc)
    gn = SSD_GROUPS * SSD_STATE
    xs, bm, cm = jnp.split(xbc, [SSD_D_INNER, SSD_D_INNER + gn], axis=-1)
    xs = xs.reshape(b, l, SSD_GROUPS, SSD_HPG, SSD_HEAD_DIM).astype(f32)
    bm = bm.reshape(b, l, SSD_GROUPS, SSD_STATE)
    cm = cm.reshape(b, l, SSD_GROUPS, SSD_STATE)
    dt = jax.nn.softplus(dt.astype(f32) + p['ssd_dt_bias'].astype(f32)).reshape(b, l, SSD_GROUPS, SSD_HPG)
    a = -jnp.exp(p['ssd_a_log'].astype(f32)).reshape(SSD_GROUPS, SSD_HPG)
    s0 = ssd_s0.reshape(b, SSD_GROUPS, SSD_HPG, SSD_STATE, SSD_HEAD_DIM)
    y_ssd, ssd_s = run_segments(cm, bm, xs * dt[..., None], dt * a, s0, segments)
    y_ssd = y_ssd + p['ssd_d'].astype(f32).reshape(SSD_GROUPS, SSD_HPG)[..., None] * xs
    y_ssd = y_ssd.reshape(b, l, SSD_D_INNER) * jax.nn.silu(z.astype(f32))
    y_ssd = group_rms(y_ssd.reshape(b, l, SSD_GROUPS, -1)).reshape(b, l, SSD_D_INNER) * p['ssd_norm_g']
    qr = rotary(q.reshape(b, l, RET_HEADS, RET_QK_DIM), positions)
    kr = rotary(k.reshape(b, l, RET_HEADS, RET_QK_DIM), positions) * (RET_QK_DIM ** -0.5)
    vr = v.reshape(b, l, RET_HEADS, 1, RET_V_DIM)
    log_gamma = jnp.log1p(-jnp.exp2(-5.0 - jnp.arange(RET_HEADS, dtype=f32)))
    la = jnp.broadcast_to(log_gamma[:, None], (b, l, RET_HEADS, 1))
    r0 = ret_s0.reshape(b, RET_HEADS, 1, RET_QK_DIM, RET_V_DIM)
    y_ret, ret_s = run_segments(qr, kr, vr, la, r0, segments)
    y_ret = group_rms(y_ret[..., 0, :]).reshape(b, l, RET_VW) * jax.nn.silu(g.astype(f32))
    out = jnp.concatenate([y_ssd, y_ret], axis=-1).astype(u.dtype) @ p['w_out0']
    return (out, new_conv,
            ssd_s.reshape(b, SSD_HEADS, SSD_STATE, SSD_HEAD_DIM).astype(u.dtype),
            ret_s.reshape(b, RET_HEADS, RET_QK_DIM, RET_V_DIM).astype(u.dtype))


def mixer_rglru(u, conv_buf, h0, positions, p):
    f32 = jnp.float32
    b, l, _ = u.shape
    gate, xr = jnp.split(u @ p['w_in1'], [LRU_WIDTH], axis=-1)
    xc, new_conv = causal_conv(xr, conv_buf, p['lru_conv_w'], p['lru_conv_b'])
    xb = xc.reshape(b, l, LRU_BLOCKS, LRU_BW)
    r = jax.nn.sigmoid((jnp.einsum('blhi,hij->blhj', xb, p['lru_wa']) + p['lru_ba']).astype(f32)).reshape(b, l, LRU_WIDTH)
    i = jax.nn.sigmoid((jnp.einsum('blhi,hij->blhj', xb, p['lru_wx']) + p['lru_bx']).astype(f32)).reshape(b, l, LRU_WIDTH)
    log_a = -LRU_C * r * jax.nn.softplus(-p['lru_lambda'].astype(f32))
    mult = jnp.sqrt(-jnp.expm1(2.0 * log_a))
    mult = jnp.where((positions == 0)[None, :, None], 1.0, mult)
    h = linear_recurrence(jnp.exp(log_a), mult * i * xc.astype(f32), h0.astype(f32))
    out = (h * jax.nn.gelu(gate.astype(f32))).astype(u.dtype) @ p['w_out1']
    return out, new_conv, h[:, -1].astype(u.dtype)


def hier_moe(u, w_rg, b_rg, w_re, b_re, w1, w3, w2):
    shp = u.shape
    t = u.reshape(-1, shp[-1])
    n_tok = t.shape[0]
    glog = (t @ w_rg + b_rg).astype(jnp.float32)
    gp = jax.nn.softmax(glog, axis=-1)
    gsel = jnp.argmax(glog, axis=-1)
    pg = jnp.take_along_axis(gp, gsel[:, None], axis=-1)
    elog = (t @ w_re + b_re).astype(jnp.float32).reshape(n_tok, MOE_GROUPS, MOE_PER_GROUP)
    elog = jnp.take_along_axis(elog, gsel[:, None, None], axis=1)[:, 0]
    topv, topi = lax.top_k(jax.nn.softmax(elog, axis=-1), MOE_TOPK)
    topv = topv / jnp.sum(topv, axis=-1, keepdims=True) * pg
    eidx = gsel[:, None] * MOE_PER_GROUP + topi
    combine = jnp.sum(jax.nn.one_hot(eidx, N_EXPERTS, dtype=jnp.float32) * topv[..., None], axis=1)
    hdn = jax.nn.silu(jnp.einsum('td,edf->tef', t, w1)) * jnp.einsum('td,edf->tef', t, w3)
    hdn = hdn * combine[..., None].astype(hdn.dtype)
    return jnp.einsum('tef,efd->td', hdn, w2).reshape(shp)


def trunk(x, positions, segments, ssd_conv0, ssd0, ret0, lru_conv0, lru0, p):
    h = x
    st = {}
    for layer in range(DEPTH):
        hn = rmsnorm(h, p['norm_mix'][layer])
        if layer % 2 == 0:
            mix, st['ssd_conv'], st['ssd'], st['ret'] = mixer_ssd_ret(hn, ssd_conv0, ssd0, ret0, positions, segments, p)
        else:
            mix, st['lru_conv'], st['lru'] = mixer_rglru(hn, lru_conv0, lru0, positions, p)
        h = h + mix
        h = h + hier_moe(rmsnorm(h, p['norm_ffn'][layer]), p['moe_w_rg'][layer], p['moe_b_rg'][layer],
                         p['moe_w_re'][layer], p['moe_b_re'][layer], p['moe_w1'][layer],
                         p['moe_w3'][layer], p['moe_w2'][layer])
    return rmsnorm(h, p['norm_final']), st


def setup_inputs(seed: int = 0) -> dict:
    key = jax.random.key(seed)
    ks = jax.random.split(key, 48)
    f32 = jnp.float32

    def nrm(i, shape, scale):
        return jax.random.normal(ks[i], shape, f32) * scale

    def uni(i, shape, lo, hi):
        return jax.random.uniform(ks[i], shape, f32, lo, hi)

    dt0 = jnp.exp(uni(14, (SSD_HEADS,), math.log(1e-3), math.log(1e-1)))
    a_lru = uni(26, (LRU_WIDTH,), 0.9, 0.999) ** (1.0 / LRU_C)
    return {
        'x_prompt': nrm(0, (BATCH, SEQ, D_MODEL), 1.0),
        'x_sample': nrm(1, (DEC_BATCH, DEC_SEQ, D_MODEL), 1.0),
        'state_ssd_conv': nrm(2, (DEC_BATCH, CONV_W - 1, SSD_CONV_DIM), 1.0),
        'state_ssd': nrm(3, (DEC_BATCH, SSD_HEADS, SSD_STATE, SSD_HEAD_DIM), 0.1),
        'state_ret': nrm(4, (DEC_BATCH, RET_HEADS, RET_QK_DIM, RET_V_DIM), 0.1),
        'state_lru_conv': nrm(5, (DEC_BATCH, CONV_W - 1, LRU_WIDTH), 1.0),
        'state_lru': nrm(6, (DEC_BATCH, LRU_WIDTH), 0.5),
        'meta_tokens': nrm(7, (N_META, D_MODEL), 1.0),
        'norm_mix': 1.0 + nrm(8, (DEPTH, D_MODEL), 0.02),
        'norm_ffn': 1.0 + nrm(9, (DEPTH, D_MODEL), 0.02),
        'norm_final': 1.0 + nrm(10, (D_MODEL,), 0.02),
        'w_in0': nrm(11, (D_MODEL, L0_IN), D_MODEL ** -0.5),
        'ssd_conv_w': nrm(12, (CONV_W, SSD_CONV_DIM), CONV_W ** -0.5),
        'ssd_conv_b': nrm(13, (SSD_CONV_DIM,), 0.01),
        'ssd_dt_bias': dt0 + jnp.log(-jnp.expm1(-dt0)),
        'ssd_a_log': jnp.log(uni(15, (SSD_HEADS,), 1.0, 16.0)),
        'ssd_d': 1.0 + nrm(16, (SSD_HEADS,), 0.02),
        'ssd_norm_g': 1.0 + nrm(17, (SSD_D_INNER,), 0.02),
        'w_out0': nrm(18, (L0_MIX, D_MODEL), L0_MIX ** -0.5),
        'w_in1': nrm(19, (D_MODEL, 2 * LRU_WIDTH), D_MODEL ** -0.5),
        'lru_conv_w': nrm(20, (CONV_W, LRU_WIDTH), CONV_W ** -0.5),
        'lru_conv_b': nrm(21, (LRU_WIDTH,), 0.01),
        'lru_wa': nrm(22, (LRU_BLOCKS, LRU_BW, LRU_BW), LRU_BW ** -0.5),
        'lru_ba': nrm(23, (LRU_BLOCKS, LRU_BW), 0.01),
        'lru_wx': nrm(24, (LRU_BLOCKS, LRU_BW, LRU_BW), LRU_BW ** -0.5),
        'lru_bx': nrm(25, (LRU_BLOCKS, LRU_BW), 0.01),
        'lru_lambda': jnp.log(a_lru) - jnp.log1p(-a_lru),
        'w_out1': nrm(27, (LRU_WIDTH, D_MODEL), LRU_WIDTH ** -0.5),
        'moe_w_rg': nrm(28, (DEPTH, D_MODEL, MOE_GROUPS), D_MODEL ** -0.5),
        'moe_b_rg': nrm(29, (DEPTH, MOE_GROUPS), 0.01),
        'moe_w_re': nrm(30, (DEPTH, D_MODEL, N_EXPERTS), D_MODEL ** -0.5),
        'moe_b_re': nrm(31, (DEPTH, N_EXPERTS), 0.01),
        'moe_w1': nrm(32, (DEPTH, N_EXPERTS, D_MODEL, EXPERT_FF), D_MODEL ** -0.5),
        'moe_w3': nrm(33, (DEPTH, N_EXPERTS, D_MODEL, EXPERT_FF), D_MODEL ** -0.5),
        'moe_w2': nrm(34, (DEPTH, N_EXPERTS, EXPERT_FF, D_MODEL), EXPERT_FF ** -0.5),
    }


def reference(x_prompt, x_sample, state_ssd_conv, state_ssd, state_ret, state_lru_conv, state_lru,
              meta_tokens, norm_mix, norm_ffn, norm_final, w_in0, ssd_conv_w, ssd_conv_b, ssd_dt_bias,
              ssd_a_log, ssd_d, ssd_norm_g, w_out0, w_in1, lru_conv_w, lru_conv_b, lru_wa, lru_ba,
              lru_wx, lru_bx, lru_lambda, w_out1, moe_w_rg, moe_b_rg, moe_w_re, moe_b_re,
              moe_w1, moe_w3, moe_w2):
    p = dict(meta_tokens=meta_tokens, norm_mix=norm_mix, norm_ffn=norm_ffn, norm_final=norm_final,
             w_in0=w_in0, ssd_conv_w=ssd_conv_w, ssd_conv_b=ssd_conv_b, ssd_dt_bias=ssd_dt_bias,
             ssd_a_log=ssd_a_log, ssd_d=ssd_d, ssd_norm_g=ssd_norm_g, w_out0=w_out0, w_in1=w_in1,
             lru_conv_w=lru_conv_w, lru_conv_b=lru_conv_b, lru_wa=lru_wa, lru_ba=lru_ba,
             lru_wx=lru_wx, lru_bx=lru_bx, lru_lambda=lru_lambda, w_out1=w_out1,
             moe_w_rg=moe_w_rg, moe_b_rg=moe_b_rg, moe_w_re=moe_w_re, moe_b_re=moe_b_re,
             moe_w1=moe_w1, moe_w3=moe_w3, moe_w2=moe_w2)
    dt = x_prompt.dtype
    xp = jnp.concatenate([jnp.broadcast_to(meta_tokens.astype(dt)[None], (BATCH, N_META, D_MODEL)), x_prompt], axis=1)
    pos_p = jnp.arange(N_META + SEQ, dtype=jnp.int32)
    seg_p = ((0, N_META, N_META), (N_META, SEQ, CHUNK))
    yp, stp = trunk(xp, pos_p, seg_p,
                    jnp.zeros((BATCH, CONV_W - 1, SSD_CONV_DIM), dt),
                    jnp.zeros((BATCH, SSD_HEADS, SSD_STATE, SSD_HEAD_DIM), dt),
                    jnp.zeros((BATCH, RET_HEADS, RET_QK_DIM, RET_V_DIM), dt),
                    jnp.zeros((BATCH, CONV_W - 1, LRU_WIDTH), dt),
                    jnp.zeros((BATCH, LRU_WIDTH), dt), p)
    y_prompt = yp[:, N_META:]
    pos_s = PAST_LEN + jnp.arange(DEC_SEQ, dtype=jnp.int32)
    seg_s = ((0, DEC_SEQ, DEC_SEQ),)
    y_sample, sts = trunk(x_sample, pos_s, seg_s, state_ssd_conv, state_ssd, state_ret,
                          state_lru_conv, state_lru, p)
    return (y_prompt, y_sample,
            stp['ssd_conv'], stp['ssd'], stp['ret'], stp['lru_conv'], stp['lru'],
            sts['ssd_conv'], sts['ssd'], sts['ret'], sts['lru_conv'], sts['lru'])
```

```python
import functools
import math

import numpy as np
import jax
import jax.numpy as jnp
from jax import lax
from jax.experimental import pallas as pl
from jax.experimental.pallas import tpu as pltpu

F32, BF16, I32 = jnp.float32, jnp.bfloat16, jnp.int32

N_META = 16
CONV_W = 4
EPS = 1e-6
SSD_HEAD_DIM = 64
SSD_GROUPS = 8
SSD_HPG = 8
SSD_STATE = 128
RET_HEADS = 16
RET_DIM = 256
ROPE_BASE = 10000.0
LRU_BLOCKS = 16
LRU_C = 8.0
MOE_GROUPS = 4
MOE_PER_GROUP = 8
N_EXPERTS = MOE_GROUPS * MOE_PER_GROUP

V7X_LANES = 128
V7X_VMEM_LIMIT = 56 * 1024 * 1024
ROWS = 128
GROUP_W = SSD_HPG * SSD_HEAD_DIM
PAIR_W = 2 * 320
MOE_TM = 512
MOE_TF = 128


def _cparams(sem):
    return pltpu.CompilerParams(dimension_semantics=sem, vmem_limit_bytes=V7X_VMEM_LIMIT)


def _pick(n, cands):
    for c in cands:
        if n % c == 0:
            return c
    raise ValueError(f"no tile for {n} in {cands}")


def _sigmoid(x):
    return 1.0 / (1.0 + jnp.exp(-x))


def _silu(x):
    return x * _sigmoid(x)


def _rmsnorm_body(x_ref, g_ref, o_ref):
    x = x_ref[...]
    y = x * lax.rsqrt(jnp.mean(x * x, axis=-1, keepdims=True) + EPS)
    o_ref[...] = (y * g_ref[...]).astype(o_ref.dtype)


def _rmsnorm(x, g, out_dtype, tm, n_out_blocks=None, in_block=None):
    m, d = x.shape
    n_blocks = m // tm if n_out_blocks is None else n_out_blocks
    in_map = (lambda i: (i, 0)) if in_block is None else (lambda i: (in_block(i), 0))
    return pl.pallas_call(
        _rmsnorm_body,
        grid=(n_blocks,),
        in_specs=[pl.BlockSpec((tm, d), in_map), pl.BlockSpec((1, d), lambda i: (0, 0))],
        out_specs=pl.BlockSpec((tm, d), lambda i: (i, 0)),
        out_shape=jax.ShapeDtypeStruct((n_blocks * tm, d), out_dtype),
        compiler_params=_cparams(("arbitrary",)),
        name="rmsnorm",
    )(x, g.reshape(1, d))


def _mm_body(*refs, has_res, cast_b):
    a_ref, b_ref = refs[0], refs[1]
    r_ref = refs[2] if has_res else None
    o_ref = refs[3] if has_res else refs[2]
    if cast_b:
        bs_ref = refs[-1]

        @pl.when(pl.program_id(1) == 0)
        def _():
            bs_ref[...] = b_ref[...].astype(BF16)

        b = bs_ref[...]
    else:
        b = b_ref[...]
    acc = jnp.dot(a_ref[...], b, preferred_element_type=F32)
    if has_res:
        acc = acc + r_ref[...]
    o_ref[...] = acc


def _matmul(a, b, n_cols, tm, tn, col_block0=0, res=None):
    m, k = a.shape
    cast_b = b.dtype != BF16
    in_specs = [pl.BlockSpec((tm, k), lambda j, i: (i, 0)),
                pl.BlockSpec((k, tn), lambda j, i: (0, j + col_block0))]
    args = [a, b]
    if res is not None:
        in_specs.append(pl.BlockSpec((tm, tn), lambda j, i: (i, j)))
        args.append(res)
    return pl.pallas_call(
        functools.partial(_mm_body, has_res=res is not None, cast_b=cast_b),
        grid=(n_cols // tn, m // tm),
        in_specs=in_specs,
        out_specs=pl.BlockSpec((tm, tn), lambda j, i: (i, j)),
        out_shape=jax.ShapeDtypeStruct((m, n_cols), F32),
        scratch_shapes=[pltpu.VMEM((k, tn), BF16)] if cast_b else [],
        compiler_params=_cparams(("arbitrary", "arbitrary")),
        name="proj",
    )(*args)


def _split3(a):
    hi = a.astype(BF16)
    r1 = a - hi.astype(F32)
    mid = r1.astype(BF16)
    lo = (r1 - mid.astype(F32)).astype(BF16)
    return hi, mid, lo


def _dot_a01(a, m01):
    hi, mid, lo = _split3(a)
    return (jnp.dot(hi, m01, preferred_element_type=F32)
            + jnp.dot(mid, m01, preferred_element_type=F32)
            + jnp.dot(lo, m01, preferred_element_type=F32))


def _dot_01a(m01, a):
    hi, mid, lo = _split3(a)
    return (jnp.dot(m01, hi, preferred_element_type=F32)
            + jnp.dot(m01, mid, preferred_element_type=F32)
            + jnp.dot(m01, lo, preferred_element_type=F32))


def _seg_masks(seg_len):
    ti = lax.broadcasted_iota(I32, (ROWS, ROWS), 0)
    si = lax.broadcasted_iota(I32, (ROWS, ROWS), 1)
    if seg_len == ROWS:
        same = None
        causal = si <= ti
        causal_t = ti <= si
        sel = si == ROWS - 1
    else:
        shift = int(math.log2(seg_len))
        tseg = lax.shift_right_logical(ti, shift)
        sseg = lax.shift_right_logical(si, shift)
        same = tseg == sseg
        causal = jnp.logical_and(si <= ti, same)
        causal_t = jnp.logical_and(ti <= si, same)
        sel = si == lax.shift_left(tseg, shift) + (seg_len - 1)
    return causal, causal_t, sel


def _ssd_body(xs_ref, bm_ref, cm_ref, z_ref, dtc_ref, lac_ref, lar_ref, dx_ref, ng_ref, *rest,
              seg_len, n_chunks, first_valid, has_s0, has_prev):
    s0_ref = rest[0] if has_s0 else None
    y_ref, sf_ref, s_scr = rest[int(has_s0) + int(has_prev):]
    nseg = ROWS // seg_len
    c = pl.program_id(2)

    @pl.when(c == 0)
    def _():
        for i in range(nseg):
            if has_s0:
                s_scr[i] = jnp.concatenate([s0_ref[i, j] for j in range(SSD_HPG)], axis=1)
            else:
                s_scr[i] = jnp.zeros((SSD_STATE, GROUP_W), F32)

    nvalid = jnp.where(c == 0, first_valid, ROWS) if first_valid != ROWS else ROWS
    row_c = lax.broadcasted_iota(I32, (ROWS, 1), 0)
    row_r = lax.broadcasted_iota(I32, (1, ROWS), 1)
    valid = row_c < nvalid
    valid_r = row_r < nvalid
    causal, causal_t, sel = _seg_masks(seg_len)
    tril = causal.astype(BF16)
    triu = causal_t.astype(BF16)
    e_i = lax.broadcasted_iota(I32, (SSD_HPG, GROUP_W), 0)
    e_l = lax.shift_right_logical(lax.broadcasted_iota(I32, (SSD_HPG, GROUP_W), 1), 6)
    expand = (e_i == e_l).astype(BF16)
    lane_head = lax.shift_right_logical(lax.broadcasted_iota(I32, (1, GROUP_W), 1), 6)

    xs = jnp.where(valid, xs_ref[...], 0.0)
    bm = jnp.where(valid, bm_ref[...], 0.0).astype(BF16)
    cm = jnp.where(valid, cm_ref[...], 0.0).astype(BF16)
    z = jnp.where(valid, z_ref[...], 0.0)
    dt = jnp.where(valid, dtc_ref[0], 0.0)
    la = jnp.where(valid, lac_ref[0], 0.0)
    la_r = jnp.where(valid_r, lar_ref[0], 0.0)

    cum = _dot_01a(tril, la)
    cum_r = _dot_a01(la_r, triu)
    if seg_len == ROWS:
        cum_last = jnp.broadcast_to(cum[ROWS - 1:ROWS, :], (ROWS, SSD_HPG))
    else:
        cum_last = _dot_01a(sel.astype(BF16), cum)
    ecum = jnp.exp(cum)
    ecum_x = _dot_a01(ecum, expand)
    v = xs * _dot_a01(dt, expand)
    kvw = (v * _dot_a01(jnp.exp(cum_last - cum), expand)).astype(BF16)

    scores = lax.dot_general(cm, bm, (((1,), (1,)), ((), ())), preferred_element_type=F32)
    y = jnp.zeros((ROWS, GROUP_W), F32)
    for j in range(SSD_HPG):
        dj = jnp.exp(jnp.where(causal, cum[:, j:j + 1] - cum_r[j:j + 1, :], -jnp.inf))
        pj = (scores * dj).astype(BF16)
        vj = jnp.where(lane_head == j, v, 0.0).astype(BF16)
        y = y + jnp.dot(pj, vj, preferred_element_type=F32)

    y_inter = []
    for i in range(nseg):
        r0 = i * seg_len
        s_old = s_scr[i]
        y_inter.append(jnp.dot(cm[r0:r0 + seg_len], s_old.astype(BF16), preferred_element_type=F32))
        cs = lax.dot_general(bm[r0:r0 + seg_len], kvw[r0:r0 + seg_len], (((0,), (0,)), ((), ())),
                             preferred_element_type=F32)
        r_last = r0 + seg_len - 1
        s_scr[i] = ecum_x[r_last:r_last + 1, :] * s_old + cs
    y_inter = y_inter[0] if nseg == 1 else jnp.concatenate(y_inter, axis=0)
    y = y + y_inter * ecum_x
    y = y + dx_ref[...] * xs
    y = y * _silu(z)
    y = y * lax.rsqrt(jnp.mean(y * y, axis=-1, keepdims=True) + EPS) * ng_ref[...]
    y_ref[...] = y.astype(y_ref.dtype)

    @pl.when(c == n_chunks - 1)
    def _():
        for i in range(nseg):
            s_fin = s_scr[i]
            for j in range(SSD_HPG):
                sf_ref[i, j] = s_fin[:, j * SSD_HEAD_DIM:(j + 1) * SSD_HEAD_DIM]


def _alias_prev(in_specs, args, y_prev):
    if y_prev is None:
        return {}
    in_specs.append(pl.BlockSpec(memory_space=pl.ANY))
    args.append(y_prev)
    return {len(args) - 1: 0}


def _ssd_scan(xbc, p1, dtc, lac, lar, dx, ng, s0, y_prev, *, out_width, nb, n_chunks, seg_len, first_valid,
              row_block):
    aux_block = row_block
    t_rows = xbc.shape[0]
    nseg = ROWS // seg_len
    d_inner = SSD_GROUPS * GROUP_W
    nb_blk = (d_inner // SSD_STATE)
    in_specs = [
        pl.BlockSpec((ROWS, GROUP_W), lambda b, g, c: (row_block(b, c), g)),
        pl.BlockSpec((ROWS, SSD_STATE), lambda b, g, c: (row_block(b, c), nb_blk + g)),
        pl.BlockSpec((ROWS, SSD_STATE), lambda b, g, c: (row_block(b, c), nb_blk + SSD_GROUPS + g)),
        pl.BlockSpec((ROWS, GROUP_W), lambda b, g, c: (row_block(b, c), g)),
        pl.BlockSpec((None, 1, ROWS, SSD_HPG), lambda b, g, c: (g, 0, aux_block(b, c), 0)),
        pl.BlockSpec((None, 1, ROWS, SSD_HPG), lambda b, g, c: (g, 0, aux_block(b, c), 0)),
        pl.BlockSpec((None, 1, SSD_HPG, ROWS), lambda b, g, c: (g, 0, 0, aux_block(b, c))),
        pl.BlockSpec((None, 1, GROUP_W), lambda b, g, c: (g, 0, 0)),
        pl.BlockSpec((None, 1, GROUP_W), lambda b, g, c: (g, 0, 0)),
    ]
    args = [xbc, xbc, xbc, p1, dtc, lac, lar, dx, ng]
    state_spec = pl.BlockSpec((nseg, SSD_HPG, SSD_STATE, SSD_HEAD_DIM), lambda b, g, c: (b, g, 0, 0))
    if s0 is not None:
        in_specs.append(state_spec)
        args.append(s0)
    aliases = _alias_prev(in_specs, args, y_prev)
    n_heads = SSD_GROUPS * SSD_HPG
    return pl.pallas_call(
        functools.partial(_ssd_body, seg_len=seg_len, n_chunks=n_chunks, first_valid=first_valid,
                          has_s0=s0 is not None, has_prev=y_prev is not None),
        grid=(nb, SSD_GROUPS, n_chunks),
        in_specs=in_specs,
        out_specs=[pl.BlockSpec((ROWS, GROUP_W), lambda b, g, c: (row_block(b, c), g)), state_spec],
        out_shape=[jax.ShapeDtypeStruct((t_rows, out_width), BF16),
                   jax.ShapeDtypeStruct((nb * nseg, n_heads, SSD_STATE, SSD_HEAD_DIM), F32)],
        scratch_shapes=[pltpu.VMEM((nseg, SSD_STATE, GROUP_W), F32)],
        input_output_aliases=aliases,
        compiler_params=_cparams(("arbitrary", "arbitrary", "arbitrary")),
        name="ssd_scan",
    )(*args)


def _ret_body(q_ref, k_ref, v_ref, g_ref, cos_ref, sin_ref, lg_ref, *rest,
              seg_len, n_chunks, first_valid, has_s0, has_prev):
    s0_ref = rest[0] if has_s0 else None
    y_ref, sf_ref, s_scr = rest[int(has_s0) + int(has_prev):]
    nseg = ROWS // seg_len
    half = RET_DIM // 2
    c = pl.program_id(2)

    @pl.when(c == 0)
    def _():
        for i in range(nseg):
            if has_s0:
                s_scr[i] = s0_ref[i, 0]
            else:
                s_scr[i] = jnp.zeros((RET_DIM, RET_DIM), F32)

    nvalid = jnp.where(c == 0, first_valid, ROWS) if first_valid != ROWS else ROWS
    row_c = lax.broadcasted_iota(I32, (ROWS, 1), 0)
    row_r = lax.broadcasted_iota(I32, (1, ROWS), 1)
    valid = row_c < nvalid
    causal, _, _ = _seg_masks(seg_len)
    lg = lg_ref[...]
    lg_c = lg[:, 0:1]
    pos_c = jnp.minimum(jnp.bitwise_and(row_c, seg_len - 1) + 1, nvalid).astype(F32)
    pos_r = jnp.minimum(jnp.bitwise_and(row_r, seg_len - 1) + 1, nvalid).astype(F32)
    last = jnp.minimum(seg_len, nvalid).astype(F32) if first_valid != ROWS else float(seg_len)
    cum = pos_c * lg_c
    cum_r = pos_r * lg_c
    cum_last = last * lg_c

    cos = cos_ref[...]
    sin = sin_ref[...]

    def rot(x):
        x1, x2 = x[:, :half], x[:, half:]
        return jnp.concatenate([x1 * cos - x2 * sin, x1 * sin + x2 * cos], axis=1)

    q = rot(jnp.where(valid, q_ref[...], 0.0)).astype(BF16)
    k = (rot(jnp.where(valid, k_ref[...], 0.0)) * (RET_DIM ** -0.5)).astype(BF16)
    v = jnp.where(valid, v_ref[...], 0.0)
    gate = jnp.where(valid, g_ref[...], 0.0)

    decay = jnp.exp(jnp.where(causal, cum - cum_r, -jnp.inf))
    scores = lax.dot_general(q, k, (((1,), (1,)), ((), ())), preferred_element_type=F32)
    y = jnp.dot((scores * decay).astype(BF16), v.astype(BF16), preferred_element_type=F32)
    kvw = (v * jnp.exp(cum_last - cum)).astype(BF16)
    ecum = jnp.exp(cum)
    e_last = jnp.exp(cum_last)

    y_inter = []
    for i in range(nseg):
        r0 = i * seg_len
        s_old = s_scr[i]
        y_inter.append(jnp.dot(q[r0:r0 + seg_len], s_old.astype(BF16), preferred_element_type=F32))
        cs = lax.dot_general(k[r0:r0 + seg_len], kvw[r0:r0 + seg_len], (((0,), (0,)), ((), ())),
                             preferred_element_type=F32)
        s_scr[i] = e_last * s_old + cs
    y_inter = y_inter[0] if nseg == 1 else jnp.concatenate(y_inter, axis=0)
    y = y + y_inter * ecum
    y = y * lax.rsqrt(jnp.mean(y * y, axis=-1, keepdims=True) + EPS)
    y = y * _silu(gate)
    y_ref[...] = y.astype(y_ref.dtype)

    @pl.when(c == n_chunks - 1)
    def _():
        for i in range(nseg):
            sf_ref[i, 0] = s_scr[i]


def _ret_scan(p3, cos, sin, lg, s0, y_prev, *, out_width, out_blk0, nb, n_chunks, seg_len, first_valid,
              row_block, cs_block):
    t_rows = p3.shape[0]
    nseg = ROWS // seg_len
    half = RET_DIM // 2
    in_specs = [
        pl.BlockSpec((ROWS, RET_DIM), lambda b, h, c: (row_block(b, c), h)),
        pl.BlockSpec((ROWS, RET_DIM), lambda b, h, c: (row_block(b, c), RET_HEADS + h)),
        pl.BlockSpec((ROWS, RET_DIM), lambda b, h, c: (row_block(b, c), 2 * RET_HEADS + h)),
        pl.BlockSpec((ROWS, RET_DIM), lambda b, h, c: (row_block(b, c), 3 * RET_HEADS + h)),
        pl.BlockSpec((ROWS, half), lambda b, h, c: (cs_block(b, c), 0)),
        pl.BlockSpec((ROWS, half), lambda b, h, c: (cs_block(b, c), 0)),
        pl.BlockSpec((None, 1, V7X_LANES), lambda b, h, c: (h, 0, 0)),
    ]
    args = [p3, p3, p3, p3, cos, sin, lg]
    state_spec = pl.BlockSpec((nseg, 1, RET_DIM, RET_DIM), lambda b, h, c: (b, h, 0, 0))
    if s0 is not None:
        in_specs.append(state_spec)
        args.append(s0)
    aliases = _alias_prev(in_specs, args, y_prev)
    return pl.pallas_call(
        functools.partial(_ret_body, seg_len=seg_len, n_chunks=n_chunks, first_valid=first_valid,
                          has_s0=s0 is not None, has_prev=y_prev is not None),
        grid=(nb, RET_HEADS, n_chunks),
        in_specs=in_specs,
        out_specs=[pl.BlockSpec((ROWS, RET_DIM), lambda b, h, c: (row_block(b, c), out_blk0 + h)),
                   state_spec],
        out_shape=[jax.ShapeDtypeStruct((t_rows, out_width), BF16),
                   jax.ShapeDtypeStruct((nb * nseg, RET_HEADS, RET_DIM, RET_DIM), F32)],
        scratch_shapes=[pltpu.VMEM((nseg, RET_DIM, RET_DIM), F32)],
        input_output_aliases=aliases,
        compiler_params=_cparams(("arbitrary", "arbitrary", "arbitrary")),
        name="ret_scan",
    )(*args)


def _lru_body(x_ref, gate_ref, wa_ref, wx_ref, ba_ref, bx_ref, sp_ref, h0_ref, *rest,
              seg_len, n_chunks, first_valid, mark_pos0, has_prev):
    y_ref, hl_ref, carry = rest[int(has_prev):]
    c = pl.program_id(2)
    nvalid = jnp.where(c == 0, first_valid, ROWS) if first_valid != ROWS else ROWS
    row_c = lax.broadcasted_iota(I32, (ROWS, 1), 0)
    valid = row_c < nvalid

    x = jnp.where(valid, x_ref[...], 0.0)
    xb = x.astype(BF16)
    r = _sigmoid(jnp.dot(xb, wa_ref[...], preferred_element_type=F32) + ba_ref[...])
    ig = _sigmoid(jnp.dot(xb, wx_ref[...], preferred_element_type=F32) + bx_ref[...])
    log_a = -LRU_C * r * sp_ref[...]
    a = jnp.exp(log_a)
    mult = jnp.sqrt(-jnp.tanh(log_a) * (a * a + 1.0))
    if mark_pos0:
        mult = jnp.where(jnp.logical_and(c == 0, row_c == 0), 1.0, mult)
    bterm = mult * ig * x
    a = jnp.where(valid, a, 1.0)
    bterm = jnp.where(valid, bterm, 0.0)

    sub = lax.broadcasted_iota(I32, (8, PAIR_W), 0)
    if seg_len == ROWS:
        @pl.when(c == 0)
        def _():
            carry[...] = h0_ref[...]
        h_prev = carry[...]
    tiles = []
    for t in range(ROWS // 8):
        at = a[8 * t:8 * t + 8]
        bt = bterm[8 * t:8 * t + 8]
        for d in (1, 2, 4):
            a_sh = jnp.where(sub >= d, pltpu.roll(at, d, 0), 1.0)
            b_sh = jnp.where(sub >= d, pltpu.roll(bt, d, 0), 0.0)
            bt = at * b_sh + bt
            at = at * a_sh
        if seg_len != ROWS:
            h_prev = h0_ref[t:t + 1, :]
        ht = bt + at * h_prev
        h_prev = ht[7:8, :]
        if seg_len != ROWS:
            hl_ref[t:t + 1, :] = h_prev
        tiles.append(ht)
    h = jnp.concatenate(tiles, axis=0)
    if seg_len == ROWS:
        carry[...] = h_prev

        @pl.when(c == n_chunks - 1)
        def _():
            hl_ref[...] = h_prev

    g = gate_ref[...]
    gelu = 0.5 * g * (1.0 + jnp.tanh(math.sqrt(2.0 / math.pi) * (g + 0.044715 * (g * g * g))))
    y_ref[...] = (h * gelu).astype(y_ref.dtype)


def _lru_scan(xc, p_lru, wa_p, wx_p, ba_p, bx_p, sp_p, h0, y_prev, *, nb, n_chunks, seg_len, first_valid,
              mark_pos0, row_block):
    t_rows, width = xc.shape
    n_pairs = width // PAIR_W
    nseg = ROWS // seg_len
    hrows = 1 if seg_len == ROWS else nseg
    h_spec = pl.BlockSpec((None, hrows, PAIR_W), lambda b, p, c: (b, 0, p))
    vec_spec = pl.BlockSpec((None, 1, PAIR_W), lambda b, p, c: (p, 0, 0))
    in_specs = [
        pl.BlockSpec((ROWS, PAIR_W), lambda b, p, c: (row_block(b, c), p)),
        pl.BlockSpec((ROWS, PAIR_W), lambda b, p, c: (row_block(b, c), p)),
        pl.BlockSpec((None, PAIR_W, PAIR_W), lambda b, p, c: (p, 0, 0)),
        pl.BlockSpec((None, PAIR_W, PAIR_W), lambda b, p, c: (p, 0, 0)),
        vec_spec, vec_spec, vec_spec, h_spec,
    ]
    args = [xc, p_lru, wa_p, wx_p, ba_p, bx_p, sp_p, h0]
    aliases = _alias_prev(in_specs, args, y_prev)
    return pl.pallas_call(
        functools.partial(_lru_body, seg_len=seg_len, n_chunks=n_chunks, first_valid=first_valid,
                          mark_pos0=mark_pos0, has_prev=y_prev is not None),
        grid=(nb, n_pairs, n_chunks),
        in_specs=in_specs,
        out_specs=[pl.BlockSpec((ROWS, PAIR_W), lambda b, p, c: (row_block(b, c), p)), h_spec],
        out_shape=[jax.ShapeDtypeStruct((t_rows, width), BF16),
                   jax.ShapeDtypeStruct((nb, hrows, width), F32)],
        scratch_shapes=[pltpu.VMEM((1, PAIR_W), F32)],
        input_output_aliases=aliases,
        compiler_params=_cparams(("arbitrary", "arbitrary", "arbitrary")),
        name="rglru",
    )(*args)


def _router_body(h_ref, g_ref, wr_ref, br_ref, hn_ref, route_ref, cnt_ref, carry, *, tm):
    i = pl.program_id(0)

    @pl.when(i == 0)
    def _():
        carry[...] = jnp.zeros_like(carry)

    x = h_ref[...]
    hn = x * lax.rsqrt(jnp.mean(x * x, axis=-1, keepdims=True) + EPS) * g_ref[...]
    hn_ref[...] = hn

    h_hi = hn.astype(BF16)
    h_lo = (hn - h_hi.astype(F32)).astype(BF16)
    w = wr_ref[...]
    w_hi = w.astype(BF16)
    w_lo = (w - w_hi.astype(F32)).astype(BF16)
    logits = (jnp.dot(h_hi, w_hi, preferred_element_type=F32)
              + jnp.dot(h_hi, w_lo, preferred_element_type=F32)
              + jnp.dot(h_lo, w_hi, preferred_element_type=F32)) + br_ref[...]

    lane_i = lax.broadcasted_iota(I32, (tm, V7X_LANES), 1)
    lane = lane_i.astype(F32)
    big = float(4 * V7X_LANES)
    neg = -jnp.inf
    is_g = jnp.logical_and(lane_i >= N_EXPERTS, lane_i < N_EXPERTS + MOE_GROUPS)
    glog = jnp.where(is_g, logits, neg)
    gmax = jnp.max(glog, axis=-1, keepdims=True)
    gsel = jnp.min(jnp.where(glog == gmax, lane, big), axis=-1, keepdims=True) - float(N_EXPERTS)
    gsum = jnp.sum(jnp.exp(glog - gmax), axis=-1, keepdims=True)
    pg = 1.0 / gsum
    lo = gsel * float(MOE_PER_GROUP)
    in_grp = jnp.logical_and(lane >= lo, lane < lo + MOE_PER_GROUP)
    elog = jnp.where(in_grp, logits, neg)
    emax = jnp.max(elog, axis=-1, keepdims=True)
    eexp = jnp.exp(elog - emax)
    ep = eexp / jnp.sum(eexp, axis=-1, keepdims=True)
    ep = jnp.where(in_grp, ep, -1.0)
    v1 = jnp.max(ep, axis=-1, keepdims=True)
    i1 = jnp.min(jnp.where(ep == v1, lane, big), axis=-1, keepdims=True)
    ep2 = jnp.where(lane == i1, -1.0, ep)
    v2 = jnp.max(ep2, axis=-1, keepdims=True)
    i2 = jnp.min(jnp.where(ep2 == v2, lane, big), axis=-1, keepdims=True)
    vs = v1 + v2
    w1 = v1 / vs * pg
    w2 = v2 / vs * pg

    oh1 = lane == i1
    oh2 = lane == i2
    oh = jnp.logical_or(oh1, oh2).astype(BF16)
    ti = lax.broadcasted_iota(I32, (tm, tm), 0)
    si = lax.broadcasted_iota(I32, (tm, tm), 1)
    before = (si < ti).astype(BF16)
    tot = jnp.dot(before, oh, preferred_element_type=F32) + carry[...]
    rank1 = jnp.sum(jnp.where(oh1, tot, 0.0), axis=-1, keepdims=True)
    rank2 = jnp.sum(jnp.where(oh2, tot, 0.0), axis=-1, keepdims=True)
    carry[...] = carry[...] + jnp.sum(oh.astype(F32), axis=0, keepdims=True)
    cnt_ref[...] = carry[...]

    out = jnp.where(lane_i == 0, i1, 0.0)
    out = jnp.where(lane_i == 1, i2, out)
    out = jnp.where(lane_i == 2, rank1, out)
    out = jnp.where(lane_i == 3, rank2, out)
    out = jnp.where(lane_i == 4, w1, out)
    out = jnp.where(lane_i == 5, w2, out)
    route_ref[...] = out


def _router(h, g, wr, br, tm):
    t_rows, d = h.shape
    return pl.pallas_call(
        functools.partial(_router_body, tm=tm),
        grid=(t_rows // tm,),
        in_specs=[pl.BlockSpec((tm, d), lambda i: (i, 0)),
                  pl.BlockSpec((1, d), lambda i: (0, 0)),
                  pl.BlockSpec((d, V7X_LANES), lambda i: (0, 0)),
                  pl.BlockSpec((1, V7X_LANES), lambda i: (0, 0))],
        out_specs=[pl.BlockSpec((tm, d), lambda i: (i, 0)),
                   pl.BlockSpec((tm, V7X_LANES), lambda i: (i, 0)),
                   pl.BlockSpec((1, V7X_LANES), lambda i: (0, 0))],
        out_shape=[jax.ShapeDtypeStruct((t_rows, d), F32),
                   jax.ShapeDtypeStruct((t_rows, V7X_LANES), F32),
                   jax.ShapeDtypeStruct((1, V7X_LANES), F32)],
        scratch_shapes=[pltpu.VMEM((1, V7X_LANES), F32)],
        compiler_params=_cparams(("arbitrary",)),
        name="moe_router",
    )(h, g.reshape(1, d), wr, br)


def _row_copy(src_hbm, dst_buf, sem, tok, r):
    return pltpu.make_async_copy(src_hbm.at[pl.ds(tok, 1)], dst_buf.at[pl.ds(r, 1)], sem)


def _expert_body(te_ref, nt_ref, src_cur, src_nxt, hn_hbm, w1_ref, w3_ref, w2_ref, y_ref, xbuf, xn, sems,
                 *, tm):
    i = pl.program_id(0)
    j = pl.program_id(1)
    nt = nt_ref[0]
    slot = lax.rem(i, 2)

    def start_rows(src_ref, s):
        def body(r, carry_):
            _row_copy(hn_hbm, xbuf.at[s], sems.at[s], src_ref[0, 0, r], r).start()
            return carry_
        lax.fori_loop(0, tm, body, 0)

    def wait_rows(s):
        def body(r, carry_):
            _row_copy(hn_hbm, xbuf.at[s], sems.at[s], 0, r).wait()
            return carry_
        lax.fori_loop(0, tm, body, 0)

    @pl.when(jnp.logical_and(j == 0, i == 0))
    def _():
        start_rows(src_cur, 0)

    @pl.when(jnp.logical_and(j == 0, i + 1 < nt))
    def _():
        start_rows(src_nxt, 1 - slot)

    @pl.when(jnp.logical_and(j == 0, i < nt))
    def _():
        wait_rows(slot)

        xn[...] = xbuf[slot].astype(BF16)

    @pl.when(i < nt)
    def _():
        xb = xn[...]
        a = jnp.dot(xb, w1_ref[...].astype(BF16), preferred_element_type=F32)
        b = jnp.dot(xb, w3_ref[...].astype(BF16), preferred_element_type=F32)
        hdn = (_silu(a) * b).astype(BF16)
        contrib = jnp.dot(hdn, w2_ref[...].astype(BF16), preferred_element_type=F32)

        @pl.when(j == 0)
        def _():
            y_ref[...] = contrib

        @pl.when(j != 0)
        def _():
            y_ref[...] = y_ref[...] + contrib


def _experts(hn, src, tile_expert, n_tiles, w1, w3, w2, *, tm, nt_max):
    d = hn.shape[1]
    ff = w1.shape[2]
    n_f = ff // MOE_TF

    def live_j(i, j, nt):
        return jnp.where(i < nt[0], j, n_f - 1)

    grid_spec = pltpu.PrefetchScalarGridSpec(
        num_scalar_prefetch=2,
        grid=(nt_max, n_f),
        in_specs=[
            pl.BlockSpec((1, 1, tm), lambda i, j, te, nt: (i, 0, 0), memory_space=pltpu.SMEM),
            pl.BlockSpec((1, 1, tm), lambda i, j, te, nt: (jnp.minimum(i + 1, nt_max - 1), 0, 0),
                         memory_space=pltpu.SMEM),
            pl.BlockSpec(memory_space=pl.ANY),
            pl.BlockSpec((None, d, MOE_TF), lambda i, j, te, nt: (te[i], 0, live_j(i, j, nt))),
            pl.BlockSpec((None, d, MOE_TF), lambda i, j, te, nt: (te[i], 0, live_j(i, j, nt))),
            pl.BlockSpec((None, MOE_TF, d), lambda i, j, te, nt: (te[i], live_j(i, j, nt), 0)),
        ],
        out_specs=pl.BlockSpec((tm, d), lambda i, j, te, nt: (jnp.minimum(i, nt[0] - 1), 0)),
        scratch_shapes=[pltpu.VMEM((2, tm, d), F32), pltpu.VMEM((tm, d), BF16),
                        pltpu.SemaphoreType.DMA((2,))],
    )
    return pl.pallas_call(
        functools.partial(_expert_body, tm=tm),
        grid_spec=grid_spec,
        out_shape=jax.ShapeDtypeStruct((nt_max * tm, d), F32),
        compiler_params=_cparams(("arbitrary", "arbitrary")),
        name="moe_experts",
    )(tile_expert, n_tiles, src, src, hn, w1, w3, w2)


def _combine_body(idx_cur, idx_nxt, y_hbm, h_ref, route_ref, o_ref, buf, sems, *, tc):
    i = pl.program_id(0)
    n = pl.num_programs(0)
    slot = lax.rem(i, 2)

    def start_rows(idx_ref, s):
        def body(r, carry_):
            _row_copy(y_hbm, buf.at[s], sems.at[s], idx_ref[0, 0, r], r).start()
            return carry_
        lax.fori_loop(0, 2 * tc, body, 0)

    def wait_rows(s):
        def body(r, carry_):
            _row_copy(y_hbm, buf.at[s], sems.at[s], 0, r).wait()
            return carry_
        lax.fori_loop(0, 2 * tc, body, 0)

    @pl.when(i == 0)
    def _():
        start_rows(idx_cur, 0)

    @pl.when(i + 1 < n)
    def _():
        start_rows(idx_nxt, 1 - slot)

    wait_rows(slot)
    route = route_ref[...]
    w1 = route[:, 4:5]
    w2 = route[:, 5:6]
    y1 = buf[slot, pl.ds(0, tc)]
    y2 = buf[slot, pl.ds(tc, tc)]
    o_ref[...] = h_ref[...] + w1 * y1 + w2 * y2


def _combine(h, y, route, idx, *, tc):
    t_rows, d = h.shape
    n = t_rows // tc
    return pl.pallas_call(
        functools.partial(_combine_body, tc=tc),
        grid=(n,),
        in_specs=[
            pl.BlockSpec((1, 1, 2 * tc), lambda i: (i, 0, 0), memory_space=pltpu.SMEM),
            pl.BlockSpec((1, 1, 2 * tc), lambda i: (jnp.minimum(i + 1, n - 1), 0, 0),
                         memory_space=pltpu.SMEM),
            pl.BlockSpec(memory_space=pl.ANY),
            pl.BlockSpec((tc, d), lambda i: (i, 0)),
            pl.BlockSpec((tc, V7X_LANES), lambda i: (i, 0)),
        ],
        out_specs=pl.BlockSpec((tc, d), lambda i: (i, 0)),
        out_shape=jax.ShapeDtypeStruct((t_rows, d), F32),
        scratch_shapes=[pltpu.VMEM((2, 2 * tc, d), F32), pltpu.SemaphoreType.DMA((2,))],
        compiler_params=_cparams(("arbitrary",)),
        name="moe_combine",
    )(idx, idx, y, h, route)


def _moe(h, g, w_rg, b_rg, w_re, b_re, w1, w3, w2, *, tm_tok, tm_e, tc):
    t_rows, d = h.shape
    pad = V7X_LANES - N_EXPERTS - MOE_GROUPS
    wr = jnp.concatenate([w_re, w_rg, jnp.zeros((d, pad), F32)], axis=1)
    br = jnp.concatenate([b_re, b_rg, jnp.zeros((pad,), F32)]).reshape(1, V7X_LANES)
    hn, route, counts = _router(h, g, wr, br, tm_tok)

    nt_max = (2 * t_rows) // tm_e + N_EXPERTS
    e1 = route[:, 0].astype(I32)
    e2 = route[:, 1].astype(I32)
    r1 = route[:, 2].astype(I32)
    r2 = route[:, 3].astype(I32)
    cnt = counts[0, :N_EXPERTS].astype(I32)
    nt_e = (cnt + tm_e - 1) // tm_e
    tile_end = jnp.cumsum(nt_e)
    n_tiles = tile_end[-1]
    row_off = (tile_end - nt_e) * tm_e
    tile_ids = jnp.arange(nt_max, dtype=I32)
    te = jnp.minimum(jnp.sum((tile_ids[:, None] >= tile_end[None, :]).astype(I32), axis=1), N_EXPERTS - 1)
    te = jnp.where(tile_ids < n_tiles, te, te[jnp.maximum(n_tiles - 1, 0)])
    d1 = row_off[e1] + r1
    d2 = row_off[e2] + r2
    tok = jnp.arange(t_rows, dtype=I32)
    src = jnp.zeros((nt_max * tm_e,), I32).at[d1].set(tok).at[d2].set(tok)

    y = _experts(hn, src.reshape(nt_max, 1, tm_e), te, n_tiles.reshape(1), w1, w3, w2, tm=tm_e, nt_max=nt_max)
    idx = jnp.concatenate([d1.reshape(-1, 1, tc), d2.reshape(-1, 1, tc)], axis=2)
    return _combine(h, y, route, idx, tc=tc)


def _conv_seq(u, buf, w, b):
    full = jnp.concatenate([buf, u], axis=1)
    l = u.shape[1]
    y = b + w[0] * full[:, 0:l]
    for t in range(1, CONV_W):
        y = y + w[t] * full[:, t:t + l]
    return y


def _conv_flat(pre, w, b, buf_s, lay):
    nb, seq, lp_pad, nbs, ls = lay
    c = pre.shape[1]
    tp = nb * lp_pad
    st = pre[:tp].reshape(nb, lp_pad, c)
    sq = jnp.concatenate([st[:, seq:seq + N_META], st[:, :seq]], axis=1)
    yp = _conv_seq(sq, jnp.zeros((nb, CONV_W - 1, c), F32), w, b)
    yp = jnp.concatenate([yp[:, N_META:], yp[:, :N_META],
                          jnp.zeros((nb, lp_pad - seq - N_META, c), F32)], axis=1)
    us = pre[tp:].reshape(nbs, ls, c)
    ys = _conv_seq(us, buf_s, w, b)
    y = jnp.concatenate([yp.reshape(tp, c), ys.reshape(nbs * ls, c)], axis=0)
    new_p = st[:, seq - (CONV_W - 1):seq]
    new_s = jnp.concatenate([buf_s, us], axis=1)[:, -(CONV_W - 1):]
    return y, new_p, new_s


def _pair_blocks(w):
    nblk, bw, _ = w.shape
    w = w.reshape(nblk // 2, 2, bw, bw)
    z = jnp.zeros((nblk // 2, bw, bw), w.dtype)
    top = jnp.concatenate([w[:, 0], z], axis=2)
    bot = jnp.concatenate([z, w[:, 1]], axis=2)
    return jnp.concatenate([top, bot], axis=1).astype(BF16)


PAST_LEN = 16384


def kernel(x_prompt, x_sample, state_ssd_conv, state_ssd, state_ret, state_lru_conv, state_lru, meta_tokens, norm_mix, norm_ffn, norm_final, w_in0, ssd_conv_w, ssd_conv_b, ssd_dt_bias, ssd_a_log, ssd_d, ssd_norm_g, w_out0, w_in1, lru_conv_w, lru_conv_b, lru_wa, lru_ba, lru_wx, lru_bx, lru_lambda, w_out1, moe_w_rg, moe_b_rg, moe_w_re, moe_b_re, moe_w1, moe_w3, moe_w2):
    nb, seq, d = x_prompt.shape
    nbs, ls, _ = x_sample.shape
    assert seq % ROWS == 0 and ROWS % ls == 0 and (nbs * ls) % ROWS == 0
    n_xblk = seq // ROWS
    blk_p = n_xblk + 1
    lp_pad = blk_p * ROWS
    tp = nb * lp_pad
    ts = nbs * ls
    t_all = tp + ts
    nbs_blk = ts // ROWS
    seg_s = ls
    lay = (nb, seq, lp_pad, nbs, ls)

    def rb_prompt(b, c):
        return b * blk_p + lax.rem(c + n_xblk, blk_p)

    def rb_sample(b, c):
        return nb * blk_p + b

    prompt_kw = dict(nb=nb, n_chunks=blk_p, seg_len=ROWS, first_valid=N_META, row_block=rb_prompt)
    sample_kw = dict(nb=nbs_blk, n_chunks=1, seg_len=seg_s, first_valid=ROWS, row_block=rb_sample)

    xp = jnp.concatenate([x_prompt, jnp.broadcast_to(meta_tokens[None], (nb, N_META, d)),
                          jnp.zeros((nb, lp_pad - seq - N_META, d), F32)], axis=1)
    h = jnp.concatenate([xp.reshape(tp, d), x_sample.reshape(ts, d)], axis=0)

    tm_tok = _pick(t_all, (512, 256, 128))
    tm_mm = _pick(t_all, (1216, 608, 512, 256, 128))

    d_inner = SSD_GROUPS * GROUP_W
    conv_dim = d_inner + 2 * SSD_GROUPS * SSD_STATE
    n_heads = SSD_GROUPS * SSD_HPG
    c_zx = d_inner + conv_dim
    hn = _rmsnorm(h, norm_mix[0], BF16, tm_tok)
    p1 = _matmul(hn, w_in0, c_zx, tm_mm, 512)
    w_dt = jnp.concatenate([w_in0[:, c_zx:c_zx + n_heads], jnp.zeros((d, V7X_LANES - n_heads), F32)], axis=1)
    pdt = _matmul(hn, w_dt, V7X_LANES, tm_mm, V7X_LANES)
    w_qkvg = w_in0[:, c_zx + n_heads:].astype(BF16)
    p3 = _matmul(hn, w_qkvg, w_qkvg.shape[1], tm_mm, 512)

    xbc_conv, conv_p, conv_s = _conv_flat(p1[:, d_inner:], ssd_conv_w, ssd_conv_b, state_ssd_conv, lay)
    xbc = jax.nn.silu(xbc_conv)
    dt = jax.nn.softplus(pdt[:, :n_heads] + ssd_dt_bias)
    la = dt * (-jnp.exp(ssd_a_log))
    dt3 = dt.reshape(t_all, SSD_GROUPS, SSD_HPG)
    la3 = la.reshape(t_all, SSD_GROUPS, SSD_HPG)
    dtc = dt3.transpose(1, 0, 2)[:, None]
    lac = la3.transpose(1, 0, 2)[:, None]
    lar = la3.transpose(1, 2, 0)[:, None]
    dx = jnp.repeat(ssd_d, SSD_HEAD_DIM).reshape(SSD_GROUPS, 1, GROUP_W)
    ng = ssd_norm_g.reshape(SSD_GROUPS, 1, GROUP_W)

    mix_w = d_inner + RET_HEADS * RET_DIM
    mix, ssd_p = _ssd_scan(xbc, p1, dtc, lac, lar, dx, ng, None, None, out_width=mix_w, **prompt_kw)
    mix, ssd_s = _ssd_scan(xbc, p1, dtc, lac, lar, dx, ng, state_ssd, mix, out_width=mix_w, **sample_kw)

    half = RET_DIM // 2
    inv = 1.0 / (ROPE_BASE ** (jnp.arange(half, dtype=F32) / half))
    pos_p = jnp.concatenate([N_META + jnp.arange(seq, dtype=I32), jnp.arange(N_META, dtype=I32),
                             jnp.zeros((lp_pad - seq - N_META,), I32)])
    pos_s = PAST_LEN + (jnp.arange(ROWS, dtype=I32) % ls)
    ang = jnp.concatenate([pos_p, pos_s]).astype(F32)[:, None] * inv[None, :]
    cos, sin = jnp.cos(ang), jnp.sin(ang)
    log_gamma = jnp.log1p(-jnp.exp2(-5.0 - jnp.arange(RET_HEADS, dtype=F32)))
    lg = jnp.broadcast_to(log_gamma[:, None, None], (RET_HEADS, 1, V7X_LANES))
    ret_kw = dict(out_width=mix_w, out_blk0=d_inner // RET_DIM)
    mix, ret_p = _ret_scan(p3, cos, sin, lg, None, mix,
                           cs_block=lambda b, c: lax.rem(c + n_xblk, blk_p), **ret_kw, **prompt_kw)
    mix, ret_s = _ret_scan(p3, cos, sin, lg, state_ret, mix, cs_block=lambda b, c: blk_p,
                           **ret_kw, **sample_kw)
    tm_o = _pick(t_all, (608, 512, 256, 128))
    h = _matmul(mix, w_out0, d, tm_o, 256, res=h)
    moe_kw = dict(tm_tok=tm_tok, tm_e=min(MOE_TM, tm_tok), tc=_pick(t_all, (256, 128)))
    h = _moe(h, norm_ffn[0], moe_w_rg[0], moe_b_rg[0], moe_w_re[0], moe_b_re[0],
             moe_w1[0], moe_w3[0], moe_w2[0], **moe_kw)

    width = lru_lambda.shape[0]
    hn = _rmsnorm(h, norm_mix[1], BF16, tm_tok)
    p_lru = _matmul(hn, w_in1, 2 * width, tm_mm, 512)
    xc, lconv_p, lconv_s = _conv_flat(p_lru[:, width:], lru_conv_w, lru_conv_b, state_lru_conv, lay)
    n_pairs = width // PAIR_W
    wa_p = _pair_blocks(lru_wa)
    wx_p = _pair_blocks(lru_wx)
    ba_p = lru_ba.reshape(n_pairs, 1, PAIR_W)
    bx_p = lru_bx.reshape(n_pairs, 1, PAIR_W)
    sp_p = jax.nn.softplus(-lru_lambda).reshape(n_pairs, 1, PAIR_W)
    y_lru, lru_p = _lru_scan(xc, p_lru, wa_p, wx_p, ba_p, bx_p, sp_p, jnp.zeros((nb, 1, width), F32), None,
                             mark_pos0=True, **prompt_kw)
    y_lru, lru_s = _lru_scan(xc, p_lru, wa_p, wx_p, ba_p, bx_p, sp_p,
                             state_lru.reshape(nbs_blk, ROWS // ls, width), y_lru, mark_pos0=False,
                             **sample_kw)
    h = _matmul(y_lru, w_out1, d, tm_mm, 256, res=h)
    h = _moe(h, norm_ffn[1], moe_w_rg[1], moe_b_rg[1], moe_w_re[1], moe_b_re[1],
             moe_w1[1], moe_w3[1], moe_w2[1], **moe_kw)

    y_prompt = _rmsnorm(h, norm_final, F32, ROWS, n_out_blocks=nb * n_xblk,
                        in_block=lambda i: (i // n_xblk) * blk_p + lax.rem(i, n_xblk))
    y_sample = _rmsnorm(h, norm_final, F32, ROWS, n_out_blocks=nbs_blk, in_block=lambda i: nb * blk_p + i)
    return (y_prompt.reshape(nb, seq, d), y_sample.reshape(nbs, ls, d),
            conv_p, ssd_p, ret_p, lconv_p, lru_p.reshape(nb, width),
            conv_s, ssd_s, ret_s, lconv_s, lru_s.reshape(nbs, width))
```

```python
import functools
import math

import jax
import jax.numpy as jnp
from jax import lax
from jax.experimental import pallas as pl
from jax.experimental.pallas import tpu as pltpu

F32, BF16, I32 = jnp.float32, jnp.bfloat16, jnp.int32

N_META = 16
CONV_W = 4
EPS = 1e-6
PAST_LEN = 16384
SSD_HEAD_DIM = 64
SSD_GROUPS = 8
SSD_HPG = 8
SSD_STATE = 128
RET_HEADS = 16
RET_DIM = 256
ROPE_BASE = 10000.0
LRU_C = 8.0
MOE_GROUPS = 4
MOE_PER_GROUP = 8
N_EXPERTS = MOE_GROUPS * MOE_PER_GROUP

V7X_LANES = 128
V7X_SUBLANES = 8
V7X_VMEM_LIMIT = 56 * 1024 * 1024
ROWS = 128
GROUP_W = SSD_HPG * SSD_HEAD_DIM
XBC_W = GROUP_W + 2 * SSD_STATE
PAIR_W = 2 * 320
MOE_TM = 512
W_RING = 4
W13_ROWS = 512
W2_ROWS = 64


def _cparams(sem):
    return pltpu.CompilerParams(dimension_semantics=sem, vmem_limit_bytes=V7X_VMEM_LIMIT)


def _pick(n, cands):
    for c in cands:
        if n % c == 0:
            return c
    raise ValueError(f"no tile for {n} in {cands}")


def _sigmoid(x):
    return 1.0 / (1.0 + jnp.exp(-x))


def _silu(x):
    return x * _sigmoid(x)


def _rmsnorm_body(x_ref, g_ref, o_ref):
    x = x_ref[...]
    y = x * lax.rsqrt(jnp.mean(x * x, axis=-1, keepdims=True) + EPS)
    o_ref[...] = (y * g_ref[...]).astype(o_ref.dtype)


def _rmsnorm(x, g, out_dtype, tm, n_out_blocks=None, in_block=None):
    m, d = x.shape
    n_blocks = m // tm if n_out_blocks is None else n_out_blocks
    in_map = (lambda i: (i, 0)) if in_block is None else (lambda i: (in_block(i), 0))
    return pl.pallas_call(
        _rmsnorm_body,
        grid=(n_blocks,),
        in_specs=[pl.BlockSpec((tm, d), in_map), pl.BlockSpec((1, d), lambda i: (0, 0))],
        out_specs=pl.BlockSpec((tm, d), lambda i: (i, 0)),
        out_shape=jax.ShapeDtypeStruct((n_blocks * tm, d), out_dtype),
        compiler_params=_cparams(("arbitrary",)),
        name="rmsnorm",
    )(x, g.reshape(1, d))


def _mm_body(*refs, has_res):
    a_ref, b_ref = refs[0], refs[1]
    r_ref = refs[2] if has_res else None
    o_ref = refs[3] if has_res else refs[2]
    bs_ref = refs[-1]

    @pl.when(pl.program_id(1) == 0)
    def _():
        bs_ref[...] = b_ref[...].astype(BF16)

    acc = jnp.dot(a_ref[...], bs_ref[...], preferred_element_type=F32)
    if has_res:
        acc = acc + r_ref[...]
    o_ref[...] = acc


def _matmul(a, b, n_cols, tm, tn, res=None):
    m, k = a.shape
    in_specs = [pl.BlockSpec((tm, k), lambda j, i: (i, 0)),
                pl.BlockSpec((k, tn), lambda j, i: (0, j))]
    args = [a, b]
    if res is not None:
        in_specs.append(pl.BlockSpec((tm, tn), lambda j, i: (i, j)))
        args.append(res)
    return pl.pallas_call(
        functools.partial(_mm_body, has_res=res is not None),
        grid=(n_cols // tn, m // tm),
        in_specs=in_specs,
        out_specs=pl.BlockSpec((tm, tn), lambda j, i: (i, j)),
        out_shape=jax.ShapeDtypeStruct((m, n_cols), F32),
        scratch_shapes=[pltpu.VMEM((k, tn), BF16)],
        compiler_params=_cparams(("arbitrary", "arbitrary")),
        name="proj",
    )(*args)


def _mm_t_body(a_ref, bt_ref, o_ref, bs_ref):
    @pl.when(pl.program_id(1) == 0)
    def _():
        bs_ref[...] = bt_ref[...].astype(BF16)

    o_ref[...] = lax.dot_general(a_ref[...], bs_ref[...], (((1,), (1,)), ((), ())),
                                 preferred_element_type=F32)


def _matmul_t(a, bt, row0, n_cols, tm, tn):
    m, k = a.shape
    return pl.pallas_call(
        _mm_t_body,
        grid=(n_cols // tn, m // tm),
        in_specs=[pl.BlockSpec((tm, k), lambda j, i: (i, 0)),
                  pl.BlockSpec((pl.Element(tn), pl.Element(k)),
                               lambda j, i: (pl.multiple_of(row0 + j * tn, math.gcd(row0, tn)), 0))],
        out_specs=pl.BlockSpec((tm, tn), lambda j, i: (i, j)),
        out_shape=jax.ShapeDtypeStruct((m, n_cols), F32),
        scratch_shapes=[pltpu.VMEM((tn, k), BF16)],
        compiler_params=_cparams(("arbitrary", "arbitrary")),
        name="proj_t",
    )(a, bt)


def _split3(a):
    hi = a.astype(BF16)
    r1 = a - hi.astype(F32)
    mid = r1.astype(BF16)
    lo = (r1 - mid.astype(F32)).astype(BF16)
    return hi, mid, lo


def _dot_a01(a, m01):
    hi, mid, lo = _split3(a)
    return (jnp.dot(hi, m01, preferred_element_type=F32)
            + jnp.dot(mid, m01, preferred_element_type=F32)
            + jnp.dot(lo, m01, preferred_element_type=F32))


def _dot_01a(m01, a):
    hi, mid, lo = _split3(a)
    return (jnp.dot(m01, hi, preferred_element_type=F32)
            + jnp.dot(m01, mid, preferred_element_type=F32)
            + jnp.dot(m01, lo, preferred_element_type=F32))


def _seg_masks(seg_len):
    ti = lax.broadcasted_iota(I32, (ROWS, ROWS), 0)
    si = lax.broadcasted_iota(I32, (ROWS, ROWS), 1)
    if seg_len == ROWS:
        causal = si <= ti
        causal_t = ti <= si
        sel = si == ROWS - 1
    else:
        shift = int(math.log2(seg_len))
        tseg = lax.shift_right_logical(ti, shift)
        sseg = lax.shift_right_logical(si, shift)
        same = tseg == sseg
        causal = jnp.logical_and(si <= ti, same)
        causal_t = jnp.logical_and(ti <= si, same)
        sel = si == lax.shift_left(tseg, shift) + (seg_len - 1)
    return causal, causal_t, sel


def _conv_block(u, ext_scr, w, b, c, first_valid):
    halo = V7X_SUBLANES

    @pl.when(c == 0)
    def _():
        ext_scr[0:halo, :] = jnp.zeros((halo, u.shape[1]), F32)

    ext_scr[halo:halo + ROWS, :] = u
    y = b + w[0:1] * ext_scr[pl.ds(halo - 3, ROWS), :]
    y = y + w[1:2] * ext_scr[pl.ds(halo - 2, ROWS), :]
    y = y + w[2:3] * ext_scr[pl.ds(halo - 1, ROWS), :]
    y = y + w[3:4] * u

    @pl.when(c == 0)
    def _():
        ext_scr[0:halo, :] = ext_scr[first_valid:first_valid + halo, :]

    @pl.when(c != 0)
    def _():
        ext_scr[0:halo, :] = ext_scr[ROWS:ROWS + halo, :]

    return y


def _alias_prev(in_specs, args, y_prev):
    if y_prev is None:
        return {}
    in_specs.append(pl.BlockSpec(memory_space=pl.ANY))
    args.append(y_prev)
    return {len(args) - 1: 0}


def _ssd_body(xs_ref, bm_ref, cm_ref, z_ref, dtc_ref, lac_ref, lar_ref, dx_ref, ng_ref, *rest,
              seg_len, n_chunks, first_valid, has_s0, has_prev, conv):
    rest = list(rest)
    cw = [rest.pop(0) for _ in range(6)] if conv else None
    s0_ref = rest.pop(0) if has_s0 else None
    if has_prev:
        rest.pop(0)
    y_ref, sf_ref, s_scr = rest[:3]
    nseg = ROWS // seg_len
    c = pl.program_id(2)

    @pl.when(c == 0)
    def _():
        for i in range(nseg):
            if has_s0:
                s_scr[i] = jnp.concatenate([s0_ref[i, j] for j in range(SSD_HPG)], axis=0)
            else:
                s_scr[i] = jnp.zeros((GROUP_W, SSD_STATE), F32)

    if conv:
        u = jnp.concatenate([xs_ref[...], bm_ref[...], cm_ref[...]], axis=1)
        w = jnp.concatenate([cw[0][...], cw[1][...], cw[2][...]], axis=1)
        b = jnp.concatenate([cw[3][...], cw[4][...], cw[5][...]], axis=1)
        act = _silu(_conv_block(u, rest[3], w, b, c, first_valid))
        xs_in = act[:, :GROUP_W]
        bm_in = act[:, GROUP_W:GROUP_W + SSD_STATE]
        cm_in = act[:, GROUP_W + SSD_STATE:]
    else:
        xs_in, bm_in, cm_in = xs_ref[...], bm_ref[...], cm_ref[...]

    nvalid = jnp.where(c == 0, first_valid, ROWS) if first_valid != ROWS else ROWS
    row_c = lax.broadcasted_iota(I32, (ROWS, 1), 0)
    row_r = lax.broadcasted_iota(I32, (1, ROWS), 1)
    valid = row_c < nvalid
    valid_r = row_r < nvalid
    causal, causal_t, sel = _seg_masks(seg_len)
    tril = causal.astype(BF16)
    triu = causal_t.astype(BF16)
    e_i = lax.broadcasted_iota(I32, (SSD_HPG, GROUP_W), 0)
    e_l = lax.shift_right_logical(lax.broadcasted_iota(I32, (SSD_HPG, GROUP_W), 1), 6)
    expand = (e_i == e_l).astype(BF16)
    lane_head = lax.shift_right_logical(lax.broadcasted_iota(I32, (1, GROUP_W), 1), 6)

    xs = jnp.where(valid, xs_in, 0.0)
    bm = jnp.where(valid, bm_in, 0.0).astype(BF16)
    cm = jnp.where(valid, cm_in, 0.0).astype(BF16)
    z = jnp.where(valid, z_ref[...], 0.0)
    dt = jnp.where(valid, dtc_ref[0], 0.0)
    la = jnp.where(valid, lac_ref[0], 0.0)
    la_r = jnp.where(valid_r, lar_ref[0], 0.0)

    cum = _dot_01a(tril, la)
    cum_r = _dot_a01(la_r, triu)
    if seg_len == ROWS:
        cum_last = jnp.broadcast_to(cum[ROWS - 1:ROWS, :], (ROWS, SSD_HPG))
    else:
        cum_last = _dot_01a(sel.astype(BF16), cum)
    ecum_x = _dot_a01(jnp.exp(cum), expand)
    v = xs * _dot_a01(dt, expand)
    kvw = (v * _dot_a01(jnp.exp(cum_last - cum), expand)).astype(BF16)

    scores = lax.dot_general(cm, bm, (((1,), (1,)), ((), ())), preferred_element_type=F32)
    y = jnp.zeros((ROWS, GROUP_W), F32)
    for j in range(SSD_HPG):
        dj = jnp.exp(jnp.where(causal, cum[:, j:j + 1] - cum_r[j:j + 1, :], -jnp.inf))
        pj = (scores * dj).astype(BF16)
        vj = jnp.where(lane_head == j, v, 0.0).astype(BF16)
        y = y + jnp.dot(pj, vj, preferred_element_type=F32)

    y_inter = []
    for i in range(nseg):
        r0 = i * seg_len
        r_last = r0 + seg_len - 1
        s_old = s_scr[i]
        y_inter.append(lax.dot_general(cm[r0:r0 + seg_len], s_old.astype(BF16), (((1,), (1,)), ((), ())),
                                       preferred_element_type=F32))
        cs = lax.dot_general(kvw[r0:r0 + seg_len], bm[r0:r0 + seg_len], (((0,), (0,)), ((), ())),
                             preferred_element_type=F32)
        e_last = jnp.exp(cum_r[:, r_last:r_last + 1])
        dec = jnp.concatenate([jnp.broadcast_to(e_last[j:j + 1, :], (SSD_HEAD_DIM, SSD_STATE))
                               for j in range(SSD_HPG)], axis=0)
        s_scr[i] = dec * s_old + cs
    y_inter = y_inter[0] if nseg == 1 else jnp.concatenate(y_inter, axis=0)
    y = y + y_inter * ecum_x
    y = y + dx_ref[...] * xs
    y = y * _silu(z)
    y = y * lax.rsqrt(jnp.mean(y * y, axis=-1, keepdims=True) + EPS) * ng_ref[...]
    y_ref[...] = y.astype(y_ref.dtype)

    @pl.when(c == n_chunks - 1)
    def _():
        for i in range(nseg):
            s_fin = s_scr[i]
            for j in range(SSD_HPG):
                sf_ref[i, j] = s_fin[j * SSD_HEAD_DIM:(j + 1) * SSD_HEAD_DIM, :]


def _ssd_scan(xbc_src, xbc_blk0, data_block, p1, dtc, lac, lar, dx, ng, conv_wb, s0, y_prev, *, out_width,
              nb, n_chunks, seg_len, first_valid, row_block):
    t_rows = p1.shape[0]
    nseg = ROWS // seg_len
    conv = conv_wb is not None
    d_inner = SSD_GROUPS * GROUP_W
    per128 = GROUP_W // SSD_STATE
    b_blk0 = (xbc_blk0 + SSD_GROUPS) * per128
    c_blk0 = b_blk0 + SSD_GROUPS
    in_specs = [
        pl.BlockSpec((ROWS, GROUP_W), lambda b, g, c: (data_block(b, c), xbc_blk0 + g)),
        pl.BlockSpec((ROWS, SSD_STATE), lambda b, g, c: (data_block(b, c), b_blk0 + g)),
        pl.BlockSpec((ROWS, SSD_STATE), lambda b, g, c: (data_block(b, c), c_blk0 + g)),
        pl.BlockSpec((ROWS, GROUP_W), lambda b, g, c: (row_block(b, c), g)),
        pl.BlockSpec((None, 1, ROWS, SSD_HPG), lambda b, g, c: (g, 0, row_block(b, c), 0)),
        pl.BlockSpec((None, 1, ROWS, SSD_HPG), lambda b, g, c: (g, 0, row_block(b, c), 0)),
        pl.BlockSpec((None, 1, SSD_HPG, ROWS), lambda b, g, c: (g, 0, 0, row_block(b, c))),
        pl.BlockSpec((None, 1, GROUP_W), lambda b, g, c: (g, 0, 0)),
        pl.BlockSpec((None, 1, GROUP_W), lambda b, g, c: (g, 0, 0)),
    ]
    args = [xbc_src, xbc_src, xbc_src, p1, dtc, lac, lar, dx, ng]
    scratch = [pltpu.VMEM((nseg, GROUP_W, SSD_STATE), F32)]
    if conv:
        cw, cb = conv_wb
        nbw = SSD_GROUPS * per128
        for arr, rows in ((cw, CONV_W), (cb, 1)):
            in_specs += [pl.BlockSpec((rows, GROUP_W), lambda b, g, c: (0, g)),
                         pl.BlockSpec((rows, SSD_STATE), lambda b, g, c: (0, nbw + g)),
                         pl.BlockSpec((rows, SSD_STATE), lambda b, g, c: (0, nbw + SSD_GROUPS + g))]
            args += [arr, arr, arr]
        scratch.append(pltpu.VMEM((ROWS + V7X_SUBLANES, XBC_W), F32))
    state_spec = pl.BlockSpec((nseg, SSD_HPG, SSD_HEAD_DIM, SSD_STATE), lambda b, g, c: (b, g, 0, 0))
    if s0 is not None:
        in_specs.append(state_spec)
        args.append(s0)
    aliases = _alias_prev(in_specs, args, y_prev)
    n_heads = SSD_GROUPS * SSD_HPG
    return pl.pallas_call(
        functools.partial(_ssd_body, seg_len=seg_len, n_chunks=n_chunks, first_valid=first_valid,
                          has_s0=s0 is not None, has_prev=y_prev is not None, conv=conv),
        grid=(nb, SSD_GROUPS, n_chunks),
        in_specs=in_specs,
        out_specs=[pl.BlockSpec((ROWS, GROUP_W), lambda b, g, c: (row_block(b, c), g)), state_spec],
        out_shape=[jax.ShapeDtypeStruct((t_rows, out_width), BF16),
                   jax.ShapeDtypeStruct((nb * nseg, n_heads, SSD_HEAD_DIM, SSD_STATE), F32)],
        scratch_shapes=scratch,
        input_output_aliases=aliases,
        compiler_params=_cparams(("arbitrary", "arbitrary", "arbitrary")),
        name="ssd_scan",
    )(*args)


def _ret_body(q_ref, k_ref, v_ref, g_ref, cos_ref, sin_ref, lg_ref, *rest,
              seg_len, n_chunks, first_valid, has_s0, has_prev):
    s0_ref = rest[0] if has_s0 else None
    y_ref, sf_ref, s_scr = rest[int(has_s0) + int(has_prev):]
    nseg = ROWS // seg_len
    half = RET_DIM // 2
    c = pl.program_id(2)

    @pl.when(c == 0)
    def _():
        for i in range(nseg):
            if has_s0:
                s_scr[i] = s0_ref[i, 0]
            else:
                s_scr[i] = jnp.zeros((RET_DIM, RET_DIM), F32)

    nvalid = jnp.where(c == 0, first_valid, ROWS) if first_valid != ROWS else ROWS
    row_c = lax.broadcasted_iota(I32, (ROWS, 1), 0)
    row_r = lax.broadcasted_iota(I32, (1, ROWS), 1)
    valid = row_c < nvalid
    causal, _, _ = _seg_masks(seg_len)
    lg = lg_ref[...]
    lg_c = lg[:, 0:1]
    pos_c = jnp.minimum(jnp.bitwise_and(row_c, seg_len - 1) + 1, nvalid).astype(F32)
    pos_r = jnp.minimum(jnp.bitwise_and(row_r, seg_len - 1) + 1, nvalid).astype(F32)
    last = jnp.minimum(seg_len, nvalid).astype(F32) if first_valid != ROWS else float(seg_len)
    cum = pos_c * lg_c
    cum_r = pos_r * lg_c
    cum_last = last * lg_c

    cos = cos_ref[...]
    sin = sin_ref[...]

    def rot(x):
        x1, x2 = x[:, :half], x[:, half:]
        return jnp.concatenate([x1 * cos - x2 * sin, x1 * sin + x2 * cos], axis=1)

    q = rot(jnp.where(valid, q_ref[...], 0.0)).astype(BF16)
    k = (rot(jnp.where(valid, k_ref[...], 0.0)) * (RET_DIM ** -0.5)).astype(BF16)
    v = jnp.where(valid, v_ref[...], 0.0)
    gate = jnp.where(valid, g_ref[...], 0.0)

    decay = jnp.exp(jnp.where(causal, cum - cum_r, -jnp.inf))
    scores = lax.dot_general(q, k, (((1,), (1,)), ((), ())), preferred_element_type=F32)
    y = jnp.dot((scores * decay).astype(BF16), v.astype(BF16), preferred_element_type=F32)
    kvw = (v * jnp.exp(cum_last - cum)).astype(BF16)
    ecum = jnp.exp(cum)
    e_last = jnp.exp(cum_last)

    y_inter = []
    for i in range(nseg):
        r0 = i * seg_len
        s_old = s_scr[i]
        y_inter.append(jnp.dot(q[r0:r0 + seg_len], s_old.astype(BF16), preferred_element_type=F32))
        cs = lax.dot_general(k[r0:r0 + seg_len], kvw[r0:r0 + seg_len], (((0,), (0,)), ((), ())),
                             preferred_element_type=F32)
        s_scr[i] = e_last * s_old + cs
    y_inter = y_inter[0] if nseg == 1 else jnp.concatenate(y_inter, axis=0)
    y = y + y_inter * ecum
    y = y * lax.rsqrt(jnp.mean(y * y, axis=-1, keepdims=True) + EPS)
    y = y * _silu(gate)
    y_ref[...] = y.astype(y_ref.dtype)

    @pl.when(c == n_chunks - 1)
    def _():
        for i in range(nseg):
            sf_ref[i, 0] = s_scr[i]


def _ret_scan(p3, cos, sin, lg, s0, y_prev, *, out_width, out_blk0, nb, n_chunks, seg_len, first_valid,
              row_block, cs_block):
    t_rows = p3.shape[0]
    nseg = ROWS // seg_len
    half = RET_DIM // 2
    in_specs = [
        pl.BlockSpec((ROWS, RET_DIM), lambda b, h, c: (row_block(b, c), h)),
        pl.BlockSpec((ROWS, RET_DIM), lambda b, h, c: (row_block(b, c), RET_HEADS + h)),
        pl.BlockSpec((ROWS, RET_DIM), lambda b, h, c: (row_block(b, c), 2 * RET_HEADS + h)),
        pl.BlockSpec((ROWS, RET_DIM), lambda b, h, c: (row_block(b, c), 3 * RET_HEADS + h)),
        pl.BlockSpec((ROWS, half), lambda b, h, c: (cs_block(b, c), 0)),
        pl.BlockSpec((ROWS, half), lambda b, h, c: (cs_block(b, c), 0)),
        pl.BlockSpec((None, 1, V7X_LANES), lambda b, h, c: (h, 0, 0)),
    ]
    args = [p3, p3, p3, p3, cos, sin, lg]
    state_spec = pl.BlockSpec((nseg, 1, RET_DIM, RET_DIM), lambda b, h, c: (b, h, 0, 0))
    if s0 is not None:
        in_specs.append(state_spec)
        args.append(s0)
    aliases = _alias_prev(in_specs, args, y_prev)
    return pl.pallas_call(
        functools.partial(_ret_body, seg_len=seg_len, n_chunks=n_chunks, first_valid=first_valid,
                          has_s0=s0 is not None, has_prev=y_prev is not None),
        grid=(nb, RET_HEADS, n_chunks),
        in_specs=in_specs,
        out_specs=[pl.BlockSpec((ROWS, RET_DIM), lambda b, h, c: (row_block(b, c), out_blk0 + h)),
                   state_spec],
        out_shape=[jax.ShapeDtypeStruct((t_rows, out_width), BF16),
                   jax.ShapeDtypeStruct((nb * nseg, RET_HEADS, RET_DIM, RET_DIM), F32)],
        scratch_shapes=[pltpu.VMEM((nseg, RET_DIM, RET_DIM), F32)],
        input_output_aliases=aliases,
        compiler_params=_cparams(("arbitrary", "arbitrary", "arbitrary")),
        name="ret_scan",
    )(*args)


def _lru_body(x_ref, gate_ref, wa_ref, wx_ref, ba_ref, bx_ref, sp_ref, h0_ref, *rest,
              seg_len, n_chunks, first_valid, mark_pos0, has_prev, conv):
    rest = list(rest)
    cw = [rest.pop(0) for _ in range(2)] if conv else None
    if has_prev:
        rest.pop(0)
    y_ref, hl_ref, carry = rest[:3]
    c = pl.program_id(2)
    nvalid = jnp.where(c == 0, first_valid, ROWS) if first_valid != ROWS else ROWS
    row_c = lax.broadcasted_iota(I32, (ROWS, 1), 0)
    valid = row_c < nvalid

    if conv:
        x_in = _conv_block(x_ref[...], rest[3], cw[0][...], cw[1][...], c, first_valid)
    else:
        x_in = x_ref[...]
    x = jnp.where(valid, x_in, 0.0)
    xb = x.astype(BF16)
    r = _sigmoid(jnp.dot(xb, wa_ref[...], preferred_element_type=F32) + ba_ref[...])
    ig = _sigmoid(jnp.dot(xb, wx_ref[...], preferred_element_type=F32) + bx_ref[...])
    log_a = -LRU_C * r * sp_ref[...]
    a = jnp.exp(log_a)
    mult = jnp.sqrt(-jnp.tanh(log_a) * (a * a + 1.0))
    if mark_pos0:
        mult = jnp.where(jnp.logical_and(c == 0, row_c == 0), 1.0, mult)
    bterm = mult * ig * x
    a = jnp.where(valid, a, 1.0)
    bterm = jnp.where(valid, bterm, 0.0)

    sub = lax.broadcasted_iota(I32, (V7X_SUBLANES, PAIR_W), 0)
    if seg_len == ROWS:
        @pl.when(c == 0)
        def _():
            carry[...] = h0_ref[...]
        h_prev = carry[...]
    tiles = []
    for t in range(ROWS // V7X_SUBLANES):
        at = a[8 * t:8 * t + 8]
        bt = bterm[8 * t:8 * t + 8]
        for d in (1, 2, 4):
            a_sh = jnp.where(sub >= d, pltpu.roll(at, d, 0), 1.0)
            b_sh = jnp.where(sub >= d, pltpu.roll(bt, d, 0), 0.0)
            bt = at * b_sh + bt
            at = at * a_sh
        if seg_len != ROWS:
            h_prev = h0_ref[t:t + 1, :]
        ht = bt + at * h_prev
        h_prev = ht[7:8, :]
        if seg_len != ROWS:
            hl_ref[t:t + 1, :] = h_prev
        tiles.append(ht)
    h = jnp.concatenate(tiles, axis=0)
    if seg_len == ROWS:
        carry[...] = h_prev

        @pl.when(c == n_chunks - 1)
        def _():
            hl_ref[...] = h_prev

    g = gate_ref[...]
    gelu = 0.5 * g * (1.0 + jnp.tanh(math.sqrt(2.0 / math.pi) * (g + 0.044715 * (g * g * g))))
    y_ref[...] = (h * gelu).astype(y_ref.dtype)


def _lru_scan(x_src, x_blk0, data_block, p_lru, wa_p, wx_p, ba_p, bx_p, sp_p, conv_wb, h0, y_prev, *,
              nb, n_chunks, seg_len, first_valid, mark_pos0, row_block):
    t_rows = p_lru.shape[0]
    width = p_lru.shape[1] // 2
    n_pairs = width // PAIR_W
    nseg = ROWS // seg_len
    conv = conv_wb is not None
    hrows = 1 if seg_len == ROWS else nseg
    h_spec = pl.BlockSpec((None, hrows, PAIR_W), lambda b, p, c: (b, 0, p))
    vec_spec = pl.BlockSpec((None, 1, PAIR_W), lambda b, p, c: (p, 0, 0))
    in_specs = [
        pl.BlockSpec((ROWS, PAIR_W), lambda b, p, c: (data_block(b, c), x_blk0 + p)),
        pl.BlockSpec((ROWS, PAIR_W), lambda b, p, c: (row_block(b, c), p)),
        pl.BlockSpec((None, PAIR_W, PAIR_W), lambda b, p, c: (p, 0, 0)),
        pl.BlockSpec((None, PAIR_W, PAIR_W), lambda b, p, c: (p, 0, 0)),
        vec_spec, vec_spec, vec_spec, h_spec,
    ]
    args = [x_src, p_lru, wa_p, wx_p, ba_p, bx_p, sp_p, h0]
    scratch = [pltpu.VMEM((1, PAIR_W), F32)]
    if conv:
        cw, cb = conv_wb
        in_specs += [pl.BlockSpec((CONV_W, PAIR_W), lambda b, p, c: (0, p)),
                     pl.BlockSpec((1, PAIR_W), lambda b, p, c: (0, p))]
        args += [cw, cb]
        scratch.append(pltpu.VMEM((ROWS + V7X_SUBLANES, PAIR_W), F32))
    aliases = _alias_prev(in_specs, args, y_prev)
    return pl.pallas_call(
        functools.partial(_lru_body, seg_len=seg_len, n_chunks=n_chunks, first_valid=first_valid,
                          mark_pos0=mark_pos0, has_prev=y_prev is not None, conv=conv),
        grid=(nb, n_pairs, n_chunks),
        in_specs=in_specs,
        out_specs=[pl.BlockSpec((ROWS, PAIR_W), lambda b, p, c: (row_block(b, c), p)), h_spec],
        out_shape=[jax.ShapeDtypeStruct((t_rows, width), BF16),
                   jax.ShapeDtypeStruct((nb, hrows, width), F32)],
        scratch_shapes=scratch,
        input_output_aliases=aliases,
        compiler_params=_cparams(("arbitrary", "arbitrary", "arbitrary")),
        name="rglru",
    )(*args)


def _router_body(h_ref, g_ref, wr_ref, br_ref, hn_ref, route_ref, cnt_ref, carry, *, tm):
    i = pl.program_id(0)

    @pl.when(i == 0)
    def _():
        carry[...] = jnp.zeros_like(carry)

    x = h_ref[...]
    hn = x * lax.rsqrt(jnp.mean(x * x, axis=-1, keepdims=True) + EPS) * g_ref[...]
    hn_ref[...] = hn

    h_hi = hn.astype(BF16)
    h_lo = (hn - h_hi.astype(F32)).astype(BF16)
    w = wr_ref[...]
    w_hi = w.astype(BF16)
    w_lo = (w - w_hi.astype(F32)).astype(BF16)
    logits = (jnp.dot(h_hi, w_hi, preferred_element_type=F32)
              + jnp.dot(h_hi, w_lo, preferred_element_type=F32)
              + jnp.dot(h_lo, w_hi, preferred_element_type=F32)) + br_ref[...]

    lane_i = lax.broadcasted_iota(I32, (tm, V7X_LANES), 1)
    lane = lane_i.astype(F32)
    big = float(4 * V7X_LANES)
    neg = -jnp.inf
    is_g = jnp.logical_and(lane_i >= N_EXPERTS, lane_i < N_EXPERTS + MOE_GROUPS)
    glog = jnp.where(is_g, logits, neg)
    gmax = jnp.max(glog, axis=-1, keepdims=True)
    gsel = jnp.min(jnp.where(glog == gmax, lane, big), axis=-1, keepdims=True) - float(N_EXPERTS)
    gsum = jnp.sum(jnp.exp(glog - gmax), axis=-1, keepdims=True)
    pg = 1.0 / gsum
    lo = gsel * float(MOE_PER_GROUP)
    in_grp = jnp.logical_and(lane >= lo, lane < lo + MOE_PER_GROUP)
    elog = jnp.where(in_grp, logits, neg)
    emax = jnp.max(elog, axis=-1, keepdims=True)
    eexp = jnp.exp(elog - emax)
    ep = eexp / jnp.sum(eexp, axis=-1, keepdims=True)
    ep = jnp.where(in_grp, ep, -1.0)
    v1 = jnp.max(ep, axis=-1, keepdims=True)
    i1 = jnp.min(jnp.where(ep == v1, lane, big), axis=-1, keepdims=True)
    ep2 = jnp.where(lane == i1, -1.0, ep)
    v2 = jnp.max(ep2, axis=-1, keepdims=True)
    i2 = jnp.min(jnp.where(ep2 == v2, lane, big), axis=-1, keepdims=True)
    vs = v1 + v2
    w1 = v1 / vs * pg
    w2 = v2 / vs * pg

    oh1 = lane == i1
    oh2 = lane == i2
    oh = jnp.logical_or(oh1, oh2).astype(BF16)
    ti = lax.broadcasted_iota(I32, (tm, tm), 0)
    si = lax.broadcasted_iota(I32, (tm, tm), 1)
    before = (si < ti).astype(BF16)
    tot = jnp.dot(before, oh, preferred_element_type=F32) + carry[...]
    rank1 = jnp.sum(jnp.where(oh1, tot, 0.0), axis=-1, keepdims=True)
    rank2 = jnp.sum(jnp.where(oh2, tot, 0.0), axis=-1, keepdims=True)
    carry[...] = carry[...] + jnp.sum(oh.astype(F32), axis=0, keepdims=True)
    cnt_ref[...] = carry[...]

    out = jnp.where(lane_i == 0, i1, 0.0)
    out = jnp.where(lane_i == 1, i2, out)
    out = jnp.where(lane_i == 2, rank1, out)
    out = jnp.where(lane_i == 3, rank2, out)
    out = jnp.where(lane_i == 4, w1, out)
    out = jnp.where(lane_i == 5, w2, out)
    route_ref[...] = out


def _router(h, g, wr, br, tm):
    t_rows, d = h.shape
    return pl.pallas_call(
        functools.partial(_router_body, tm=tm),
        grid=(t_rows // tm,),
        in_specs=[pl.BlockSpec((tm, d), lambda i: (i, 0)),
                  pl.BlockSpec((1, d), lambda i: (0, 0)),
                  pl.BlockSpec((d, V7X_LANES), lambda i: (0, 0)),
                  pl.BlockSpec((1, V7X_LANES), lambda i: (0, 0))],
        out_specs=[pl.BlockSpec((tm, d), lambda i: (i, 0)),
                   pl.BlockSpec((tm, V7X_LANES), lambda i: (i, 0)),
                   pl.BlockSpec((1, V7X_LANES), lambda i: (0, 0))],
        out_shape=[jax.ShapeDtypeStruct((t_rows, d), F32),
                   jax.ShapeDtypeStruct((t_rows, V7X_LANES), F32),
                   jax.ShapeDtypeStruct((1, V7X_LANES), F32)],
        scratch_shapes=[pltpu.VMEM((1, V7X_LANES), F32)],
        compiler_params=_cparams(("arbitrary",)),
        name="moe_router",
    )(h, g.reshape(1, d), wr, br)


def _expert_body(te_ref, tfirst_ref, nval_ref, nt_ref, dst_cur, dst_nxt, hn_hbm, w1_hbm, w3_hbm, w2_hbm,
                 o_hbm, xbuf, ybuf, wb1, wb3, wb2, st13, st2, gsem, ssem, wsem13, wsem2,
                 *, tm, layer, nt_max, t_rows):
    i = pl.program_id(0)
    nt = nt_ref[0]
    slot = lax.rem(i, 2)

    def token_of(d):
        return jnp.where(d >= t_rows, d - t_rows, d)

    def gather_copy(s, tok, r):
        return pltpu.make_async_copy(hn_hbm.at[pl.ds(tok, 1)], xbuf.at[s, pl.ds(r, 1)], gsem.at[s])

    def scatter_copy(d, r):
        return pltpu.make_async_copy(ybuf.at[pl.ds(r, 1)], o_hbm.at[pl.ds(d, 1)], ssem.at[0])

    def start_gather(dst_ref, s, n):
        def body(r, carry_):
            gather_copy(s, token_of(dst_ref[0, 0, r]), r).start()
            return carry_
        lax.fori_loop(0, n, body, 0)

    def wait_gather(s, n):
        def body(r, carry_):
            gather_copy(s, 0, r).wait()
            return carry_
        lax.fori_loop(0, n, body, 0)

    def start_scatter(n):
        def body(r, carry_):
            scatter_copy(dst_cur[0, 0, r], r).start()
            return carry_
        lax.fori_loop(0, n, body, 0)

    def wait_scatter(n):
        def body(r, carry_):
            scatter_copy(0, r).wait()
            return carry_
        lax.fori_loop(0, n, body, 0)

    @pl.when(i == 0)
    def _():
        xbuf[...] = jnp.zeros_like(xbuf)
        start_gather(dst_cur, 0, nval_ref[0])

    @pl.when(i + 1 < nt)
    def _():
        start_gather(dst_nxt, 1 - slot, nval_ref[jnp.minimum(i + 1, nt_max - 1)])

    @pl.when(jnp.logical_and(i < nt, tfirst_ref[i] == 1))
    def _():
        e = te_ref[i]
        n13 = 2 * (wb1.shape[0] // W13_ROWS)
        n2 = wb2.shape[0] // W2_ROWS

        def c13(k):
            src = w1_hbm if k % 2 == 0 else w3_hbm
            r0 = (k // 2) * W13_ROWS
            return pltpu.make_async_copy(src.at[layer, e, pl.ds(r0, W13_ROWS)], st13.at[k % W_RING],
                                         wsem13.at[k % W_RING])

        def c2(k):
            return pltpu.make_async_copy(w2_hbm.at[layer, e, pl.ds(k * W2_ROWS, W2_ROWS)],
                                         st2.at[k % W_RING], wsem2.at[k % W_RING])

        for k in range(W_RING):
            c13(k).start()
        for k in range(W_RING):
            c2(k).start()
        for k in range(n13):
            c13(k).wait()
            dstw = wb1 if k % 2 == 0 else wb3
            r0 = (k // 2) * W13_ROWS
            dstw[r0:r0 + W13_ROWS, :] = st13[k % W_RING].astype(BF16)
            if k + W_RING < n13:
                c13(k + W_RING).start()
        for k in range(n2):
            c2(k).wait()
            wb2[k * W2_ROWS:(k + 1) * W2_ROWS, :] = st2[k % W_RING].astype(BF16)
            if k + W_RING < n2:
                c2(k + W_RING).start()

    @pl.when(jnp.logical_and(i > 0, i <= nt))
    def _():
        wait_scatter(nval_ref[jnp.maximum(i - 1, 0)])

    @pl.when(i < nt)
    def _():
        nv = nval_ref[i]
        wait_gather(slot, nv)
        x = xbuf[slot].astype(BF16)
        a = jnp.dot(x, wb1[...], preferred_element_type=F32)
        b = jnp.dot(x, wb3[...], preferred_element_type=F32)
        hdn = (_silu(a) * b).astype(BF16)
        ybuf[...] = jnp.dot(hdn, wb2[...], preferred_element_type=F32)
        start_scatter(nv)

    @pl.when(jnp.logical_and(i == nt_max - 1, i < nt))
    def _():
        wait_scatter(nval_ref[i])


def _experts(hn, dst, tile_expert, tile_first, tile_nval, n_tiles, w1, w3, w2, *, layer, tm, nt_max):
    t_rows, d = hn.shape
    ff = w1.shape[3]
    smem_spec = lambda f: pl.BlockSpec((1, 1, tm), f, memory_space=pltpu.SMEM)
    grid_spec = pltpu.PrefetchScalarGridSpec(
        num_scalar_prefetch=4,
        grid=(nt_max,),
        in_specs=[
            smem_spec(lambda i, *_: (i, 0, 0)),
            smem_spec(lambda i, *_: (jnp.minimum(i + 1, nt_max - 1), 0, 0)),
            pl.BlockSpec(memory_space=pl.ANY),
            pl.BlockSpec(memory_space=pl.ANY),
            pl.BlockSpec(memory_space=pl.ANY),
            pl.BlockSpec(memory_space=pl.ANY),
        ],
        out_specs=pl.BlockSpec(memory_space=pl.ANY),
        scratch_shapes=[
            pltpu.VMEM((2, tm, d), F32), pltpu.VMEM((tm, d), F32),
            pltpu.VMEM((d, ff), BF16), pltpu.VMEM((d, ff), BF16), pltpu.VMEM((ff, d), BF16),
            pltpu.VMEM((W_RING, W13_ROWS, ff), F32), pltpu.VMEM((W_RING, W2_ROWS, d), F32),
            pltpu.SemaphoreType.DMA((2,)), pltpu.SemaphoreType.DMA((1,)),
            pltpu.SemaphoreType.DMA((W_RING,)), pltpu.SemaphoreType.DMA((W_RING,)),
        ],
    )
    return pl.pallas_call(
        functools.partial(_expert_body, tm=tm, layer=layer, nt_max=nt_max, t_rows=t_rows),
        grid_spec=grid_spec,
        out_shape=jax.ShapeDtypeStruct((2 * t_rows, d), F32),
        compiler_params=_cparams(("arbitrary",)),
        name="moe_experts",
    )(tile_expert, tile_first, tile_nval, n_tiles, dst, dst, hn, w1, w3, w2)


def _combine_body(h_ref, y0_ref, y1_ref, route_ref, *rest, with_norm):
    route = route_ref[...]
    h_new = h_ref[...] + route[:, 4:5] * y0_ref[...] + route[:, 5:6] * y1_ref[...]
    if with_norm:
        g_ref, o_ref, n_ref = rest
        o_ref[...] = h_new
        y = h_new * lax.rsqrt(jnp.mean(h_new * h_new, axis=-1, keepdims=True) + EPS)
        n_ref[...] = (y * g_ref[...]).astype(n_ref.dtype)
    else:
        rest[0][...] = h_new


def _combine(h, y2, route, g_next, *, tm):
    t_rows, d = h.shape
    y3 = y2.reshape(2, t_rows, d)
    row = pl.BlockSpec((tm, d), lambda i: (i, 0))
    in_specs = [row,
                pl.BlockSpec((None, tm, d), lambda i: (0, i, 0)),
                pl.BlockSpec((None, tm, d), lambda i: (1, i, 0)),
                pl.BlockSpec((tm, V7X_LANES), lambda i: (i, 0))]
    args = [h, y3, y3, route]
    with_norm = g_next is not None
    if with_norm:
        in_specs.append(pl.BlockSpec((1, d), lambda i: (0, 0)))
        args.append(g_next.reshape(1, d))
        out_specs = [row, row]
        out_shape = [jax.ShapeDtypeStruct((t_rows, d), F32), jax.ShapeDtypeStruct((t_rows, d), BF16)]
    else:
        out_specs = row
        out_shape = jax.ShapeDtypeStruct((t_rows, d), F32)
    return pl.pallas_call(
        functools.partial(_combine_body, with_norm=with_norm),
        grid=(t_rows // tm,),
        in_specs=in_specs,
        out_specs=out_specs,
        out_shape=out_shape,
        compiler_params=_cparams(("arbitrary",)),
        name="moe_combine",
    )(*args)


def _moe(h, layer, g, w_rg, b_rg, w_re, b_re, w1, w3, w2, g_next, *, tm_tok, tm_e):
    t_rows, d = h.shape
    pad = V7X_LANES - N_EXPERTS - MOE_GROUPS
    wr = jnp.concatenate([w_re[layer], w_rg[layer], jnp.zeros((d, pad), F32)], axis=1)
    br = jnp.concatenate([b_re[layer], b_rg[layer], jnp.zeros((pad,), F32)]).reshape(1, V7X_LANES)
    hn, route, counts = _router(h, g, wr, br, tm_tok)

    nt_max = (2 * t_rows) // tm_e + N_EXPERTS
    e1 = route[:, 0].astype(I32)
    e2 = route[:, 1].astype(I32)
    r1 = route[:, 2].astype(I32)
    r2 = route[:, 3].astype(I32)
    cnt = counts[0, :N_EXPERTS].astype(I32)
    nt_e = (cnt + tm_e - 1) // tm_e
    tile_end = jnp.cumsum(nt_e)
    tile_start = tile_end - nt_e
    n_tiles = tile_end[-1]
    tile_ids = jnp.arange(nt_max, dtype=I32)
    te = jnp.minimum(jnp.sum((tile_ids[:, None] >= tile_end[None, :]).astype(I32), axis=1), N_EXPERTS - 1)
    live = tile_ids < n_tiles
    k_in_e = tile_ids - tile_start[te]
    tfirst = jnp.where(jnp.logical_and(live, k_in_e == 0), 1, 0).astype(I32)
    nval = jnp.where(live, jnp.clip(cnt[te] - k_in_e * tm_e, 0, tm_e), 0).astype(I32)
    row_off = tile_start * tm_e
    d1 = row_off[e1] + r1
    d2 = row_off[e2] + r2
    tok = jnp.arange(t_rows, dtype=I32)
    dst = jnp.zeros((nt_max * tm_e,), I32).at[d1].set(tok).at[d2].set(t_rows + tok)

    y2 = _experts(hn, dst.reshape(nt_max, 1, tm_e), te, tfirst, nval, n_tiles.reshape(1), w1, w3, w2,
                  layer=layer, tm=tm_e, nt_max=nt_max)
    return _combine(h, y2, route, g_next, tm=_pick(t_rows, (256, 128)))


def _conv_sample(u, buf, w, b):
    full = jnp.concatenate([buf, u], axis=1)
    l = u.shape[1]
    y = b + w[0] * full[:, 0:l]
    for t in range(1, CONV_W):
        y = y + w[t] * full[:, t:t + l]
    return y, full[:, -(CONV_W - 1):]


def _pair_blocks(w):
    nblk, bw, _ = w.shape
    w = w.reshape(nblk // 2, 2, bw, bw)
    z = jnp.zeros((nblk // 2, bw, bw), w.dtype)
    top = jnp.concatenate([w[:, 0], z], axis=2)
    bot = jnp.concatenate([z, w[:, 1]], axis=2)
    return jnp.concatenate([top, bot], axis=1).astype(BF16)


def kernel(x_prompt, x_sample, state_ssd_conv, state_ssd, state_ret, state_lru_conv, state_lru, meta_tokens, norm_mix, norm_ffn, norm_final, w_in0, ssd_conv_w, ssd_conv_b, ssd_dt_bias, ssd_a_log, ssd_d, ssd_norm_g, w_out0, w_in1, lru_conv_w, lru_conv_b, lru_wa, lru_ba, lru_wx, lru_bx, lru_lambda, w_out1, moe_w_rg, moe_b_rg, moe_w_re, moe_b_re, moe_w1, moe_w3, moe_w2):
    nb, seq, d = x_prompt.shape
    nbs, ls, _ = x_sample.shape
    assert seq % ROWS == 0 and ROWS % ls == 0 and (nbs * ls) % ROWS == 0 and ls >= CONV_W - 1
    n_xblk = seq // ROWS
    blk_p = n_xblk + 1
    lp_pad = blk_p * ROWS
    tp = nb * lp_pad
    ts = nbs * ls
    t_all = tp + ts
    nbs_blk = ts // ROWS

    def rb_prompt(b, c):
        return b * blk_p + lax.rem(c + n_xblk, blk_p)

    def rb_sample(b, c):
        return nb * blk_p + b

    def rb_own(b, c):
        return b

    def _last_prompt_rows(p, c0, c1):
        return jnp.stack([p[b * lp_pad + seq - (CONV_W - 1):b * lp_pad + seq, c0:c1] for b in range(nb)])

    prompt_kw = dict(nb=nb, n_chunks=blk_p, seg_len=ROWS, first_valid=N_META, row_block=rb_prompt)
    sample_kw = dict(nb=nbs_blk, n_chunks=1, seg_len=ls, first_valid=ROWS, row_block=rb_sample)

    xp = jnp.concatenate([x_prompt, jnp.broadcast_to(meta_tokens[None], (nb, N_META, d)),
                          jnp.zeros((nb, lp_pad - seq - N_META, d), F32)], axis=1)
    h = jnp.concatenate([xp.reshape(tp, d), x_sample.reshape(ts, d)], axis=0)

    tm_tok = _pick(t_all, (512, 256, 128))
    tm_mm = _pick(t_all, (1216, 608, 512, 256, 128))

    d_inner = SSD_GROUPS * GROUP_W
    conv_dim = d_inner + 2 * SSD_GROUPS * SSD_STATE
    n_heads = SSD_GROUPS * SSD_HPG
    c_zx = d_inner + conv_dim
    c_qkvg = w_in0.shape[1] - c_zx - n_heads
    hn = _rmsnorm(h, norm_mix[0], BF16, tm_tok)
    w_in0_t = w_in0.T
    p1 = _matmul_t(hn, w_in0_t, 0, c_zx, tm_mm, 512)
    pdt = _matmul_t(hn, w_in0_t, c_zx, n_heads, tm_mm, n_heads)
    p3 = _matmul_t(hn, w_in0_t, c_zx + n_heads, c_qkvg, tm_mm, 512)

    dt = jax.nn.softplus(pdt + ssd_dt_bias)
    la = dt * (-jnp.exp(ssd_a_log))
    dt3 = dt.reshape(t_all, SSD_GROUPS, SSD_HPG)
    la3 = la.reshape(t_all, SSD_GROUPS, SSD_HPG)
    dtc = dt3.transpose(1, 0, 2)[:, None]
    lac = la3.transpose(1, 0, 2)[:, None]
    lar = la3.transpose(1, 2, 0)[:, None]
    dx = jnp.repeat(ssd_d, SSD_HEAD_DIM).reshape(SSD_GROUPS, 1, GROUP_W)
    ng = ssd_norm_g.reshape(SSD_GROUPS, 1, GROUP_W)

    xbc_s, conv_s = _conv_sample(p1[tp:, d_inner:].reshape(nbs, ls, conv_dim), state_ssd_conv,
                                 ssd_conv_w, ssd_conv_b)
    xbc_s = jax.nn.silu(xbc_s).reshape(ts, conv_dim)
    conv_p = _last_prompt_rows(p1, d_inner, c_zx)

    mix_w = d_inner + RET_HEADS * RET_DIM
    ssd_common = (p1, dtc, lac, lar, dx, ng)
    mix, ssd_p = _ssd_scan(p1, d_inner // GROUP_W, rb_prompt, *ssd_common,
                           (ssd_conv_w, ssd_conv_b.reshape(1, conv_dim)), None, None,
                           out_width=mix_w, **prompt_kw)
    mix, ssd_s = _ssd_scan(xbc_s, 0, rb_own, *ssd_common, None, jnp.swapaxes(state_ssd, 2, 3), mix,
                           out_width=mix_w, **sample_kw)

    half = RET_DIM // 2
    inv = 1.0 / (ROPE_BASE ** (jnp.arange(half, dtype=F32) / half))
    pos_p = jnp.concatenate([N_META + jnp.arange(seq, dtype=I32), jnp.arange(N_META, dtype=I32),
                             jnp.zeros((lp_pad - seq - N_META,), I32)])
    pos_s = PAST_LEN + (jnp.arange(ROWS, dtype=I32) % ls)
    ang = jnp.concatenate([pos_p, pos_s]).astype(F32)[:, None] * inv[None, :]
    cos, sin = jnp.cos(ang), jnp.sin(ang)
    log_gamma = jnp.log1p(-jnp.exp2(-5.0 - jnp.arange(RET_HEADS, dtype=F32)))
    lg = jnp.broadcast_to(log_gamma[:, None, None], (RET_HEADS, 1, V7X_LANES))
    ret_kw = dict(out_width=mix_w, out_blk0=d_inner // RET_DIM)
    mix, ret_p = _ret_scan(p3, cos, sin, lg, None, mix,
                           cs_block=lambda b, c: lax.rem(c + n_xblk, blk_p), **ret_kw, **prompt_kw)
    mix, ret_s = _ret_scan(p3, cos, sin, lg, state_ret, mix, cs_block=lambda b, c: blk_p,
                           **ret_kw, **sample_kw)
    tm_o = _pick(t_all, (608, 512, 256, 128))
    h = _matmul(mix, w_out0, d, tm_o, 256, res=h)
    moe_kw = dict(tm_tok=tm_tok, tm_e=min(MOE_TM, tm_tok))
    moe_w = (moe_w_rg, moe_b_rg, moe_w_re, moe_b_re, moe_w1, moe_w3, moe_w2)
    h, hn = _moe(h, 0, norm_ffn[0], *moe_w, norm_mix[1], **moe_kw)

    width = lru_lambda.shape[0]
    p_lru = _matmul(hn, w_in1, 2 * width, tm_mm, 512)
    xc_s, lconv_s = _conv_sample(p_lru[tp:, width:].reshape(nbs, ls, width), state_lru_conv,
                                 lru_conv_w, lru_conv_b)
    xc_s = xc_s.reshape(ts, width)
    lconv_p = _last_prompt_rows(p_lru, width, 2 * width)
    n_pairs = width // PAIR_W
    lru_common = (p_lru, _pair_blocks(lru_wa), _pair_blocks(lru_wx), lru_ba.reshape(n_pairs, 1, PAIR_W),
                  lru_bx.reshape(n_pairs, 1, PAIR_W),
                  jax.nn.softplus(-lru_lambda).reshape(n_pairs, 1, PAIR_W))
    y_lru, lru_p = _lru_scan(p_lru, n_pairs, rb_prompt, *lru_common,
                             (lru_conv_w, lru_conv_b.reshape(1, width)), jnp.zeros((nb, 1, width), F32),
                             None, mark_pos0=True, **prompt_kw)
    y_lru, lru_s = _lru_scan(xc_s, 0, rb_own, *lru_common, None,
                             state_lru.reshape(nbs_blk, ROWS // ls, width), y_lru, mark_pos0=False,
                             **sample_kw)
    h = _matmul(y_lru, w_out1, d, tm_mm, 256, res=h)
    h = _moe(h, 1, norm_ffn[1], *moe_w, None, **moe_kw)

    y_prompt = _rmsnorm(h, norm_final, F32, ROWS, n_out_blocks=nb * n_xblk,
                        in_block=lambda i: (i // n_xblk) * blk_p + lax.rem(i, n_xblk))
    y_sample = _rmsnorm(h, norm_final, F32, ROWS, n_out_blocks=nbs_blk, in_block=lambda i: nb * blk_p + i)
    return (y_prompt.reshape(nb, seq, d), y_sample.reshape(nbs, ls, d),
            conv_p, jnp.swapaxes(ssd_p, 2, 3), ret_p, lconv_p, lru_p.reshape(nb, width),
            conv_s, jnp.swapaxes(ssd_s, 2, 3), ret_s, lconv_s, lru_s.reshape(nbs, width))
```

```python
import functools
import math

import jax
import jax.numpy as jnp
from jax import lax
from jax.experimental import pallas as pl
from jax.experimental.pallas import tpu as pltpu

F32, BF16, I32 = jnp.float32, jnp.bfloat16, jnp.int32

N_META = 16
CONV_W = 4
EPS = 1e-6
PAST_LEN = 16384
SSD_HEAD_DIM = 64
SSD_GROUPS = 8
SSD_HPG = 8
SSD_STATE = 128
RET_HEADS = 16
RET_DIM = 256
RET_HB = 2
ROPE_BASE = 10000.0
LRU_C = 8.0
MOE_GROUPS = 4
MOE_PER_GROUP = 8
N_EXPERTS = MOE_GROUPS * MOE_PER_GROUP

V7X_LANES = 128
V7X_SUBLANES = 8
V7X_VMEM_LIMIT = 56 * 1024 * 1024
ROWS = 128
GROUP_W = SSD_HPG * SSD_HEAD_DIM
XBC_W = GROUP_W + 2 * SSD_STATE
PAIR_W = 2 * 320
MOE_TM = 512
W_RING = 4
W13_ROWS = 512
W2_ROWS = 64
ROW_UNROLL = 8


def _cparams(sem):
    return pltpu.CompilerParams(dimension_semantics=sem, vmem_limit_bytes=V7X_VMEM_LIMIT)


def _pick(n, cands):
    for c in cands:
        if n % c == 0:
            return c
    raise ValueError(f"no tile for {n} in {cands}")


def _sigmoid(x):
    return 1.0 / (1.0 + jnp.exp(-x))


def _silu(x):
    return x * _sigmoid(x)


def _rmsnorm_body(x_ref, g_ref, o_ref):
    x = x_ref[...]
    y = x * lax.rsqrt(jnp.mean(x * x, axis=-1, keepdims=True) + EPS)
    o_ref[...] = (y * g_ref[...]).astype(o_ref.dtype)


def _rmsnorm(x, g, out_dtype, tm, n_out_blocks=None, in_block=None):
    m, d = x.shape
    n_blocks = m // tm if n_out_blocks is None else n_out_blocks
    in_map = (lambda i: (i, 0)) if in_block is None else (lambda i: (in_block(i), 0))
    return pl.pallas_call(
        _rmsnorm_body,
        grid=(n_blocks,),
        in_specs=[pl.BlockSpec((tm, d), in_map), pl.BlockSpec((1, d), lambda i: (0, 0))],
        out_specs=pl.BlockSpec((tm, d), lambda i: (i, 0)),
        out_shape=jax.ShapeDtypeStruct((n_blocks * tm, d), out_dtype),
        compiler_params=_cparams(("arbitrary",)),
        name="rmsnorm",
    )(x, g.reshape(1, d))


def _mm_body(*refs, has_res, cast_b):
    a_ref, b_ref = refs[0], refs[1]
    r_ref = refs[2] if has_res else None
    o_ref = refs[3] if has_res else refs[2]
    if cast_b:
        bs_ref = refs[-1]

        @pl.when(pl.program_id(1) == 0)
        def _():
            bs_ref[...] = b_ref[...].astype(BF16)

        b = bs_ref[...]
    else:
        b = b_ref[...]
    acc = jnp.dot(a_ref[...], b, preferred_element_type=F32)
    if has_res:
        acc = acc + r_ref[...]
    o_ref[...] = acc


def _matmul(a, b, n_cols, tm, tn, res=None):
    m, k = a.shape
    cast_b = b.dtype != BF16
    in_specs = [pl.BlockSpec((tm, k), lambda j, i: (i, 0)),
                pl.BlockSpec((k, tn), lambda j, i: (0, j))]
    args = [a, b]
    if res is not None:
        in_specs.append(pl.BlockSpec((tm, tn), lambda j, i: (i, j)))
        args.append(res)
    return pl.pallas_call(
        functools.partial(_mm_body, has_res=res is not None, cast_b=cast_b),
        grid=(n_cols // tn, m // tm),
        in_specs=in_specs,
        out_specs=pl.BlockSpec((tm, tn), lambda j, i: (i, j)),
        out_shape=jax.ShapeDtypeStruct((m, n_cols), F32),
        scratch_shapes=[pltpu.VMEM((k, tn), BF16)] if cast_b else [],
        compiler_params=_cparams(("arbitrary", "arbitrary")),
        name="proj",
    )(*args)


def _mm_t_body(a_ref, bt_ref, o_ref, bs_ref):
    @pl.when(pl.program_id(1) == 0)
    def _():
        bs_ref[...] = bt_ref[...].astype(BF16)

    o_ref[...] = lax.dot_general(a_ref[...], bs_ref[...], (((1,), (1,)), ((), ())),
                                 preferred_element_type=F32)


def _matmul_t(a, bt, row0, n_cols, tm, tn):
    m, k = a.shape
    return pl.pallas_call(
        _mm_t_body,
        grid=(n_cols // tn, m // tm),
        in_specs=[pl.BlockSpec((tm, k), lambda j, i: (i, 0)),
                  pl.BlockSpec((pl.Element(tn), pl.Element(k)),
                               lambda j, i: (pl.multiple_of(row0 + j * tn, math.gcd(row0, tn)), 0))],
        out_specs=pl.BlockSpec((tm, tn), lambda j, i: (i, j)),
        out_shape=jax.ShapeDtypeStruct((m, n_cols), F32),
        scratch_shapes=[pltpu.VMEM((tn, k), BF16)],
        compiler_params=_cparams(("arbitrary", "arbitrary")),
        name="proj_t",
    )(a, bt)


def _split3(a):
    hi = a.astype(BF16)
    r1 = a - hi.astype(F32)
    mid = r1.astype(BF16)
    lo = (r1 - mid.astype(F32)).astype(BF16)
    return hi, mid, lo


def _dot_a01(a, m01):
    hi, mid, lo = _split3(a)
    return (jnp.dot(hi, m01, preferred_element_type=F32)
            + jnp.dot(mid, m01, preferred_element_type=F32)
            + jnp.dot(lo, m01, preferred_element_type=F32))


def _dot_01a(m01, a):
    hi, mid, lo = _split3(a)
    return (jnp.dot(m01, hi, preferred_element_type=F32)
            + jnp.dot(m01, mid, preferred_element_type=F32)
            + jnp.dot(m01, lo, preferred_element_type=F32))


def _seg_masks(seg_len):
    ti = lax.broadcasted_iota(I32, (ROWS, ROWS), 0)
    si = lax.broadcasted_iota(I32, (ROWS, ROWS), 1)
    if seg_len == ROWS:
        causal = si <= ti
        causal_t = ti <= si
        sel = si == ROWS - 1
    else:
        shift = int(math.log2(seg_len))
        tseg = lax.shift_right_logical(ti, shift)
        sseg = lax.shift_right_logical(si, shift)
        same = tseg == sseg
        causal = jnp.logical_and(si <= ti, same)
        causal_t = jnp.logical_and(ti <= si, same)
        sel = si == lax.shift_left(tseg, shift) + (seg_len - 1)
    return causal, causal_t, sel


def _conv_block(u, ext_scr, w, b, c, first_valid):
    halo = V7X_SUBLANES

    @pl.when(c == 0)
    def _():
        ext_scr[0:halo, :] = jnp.zeros((halo, u.shape[1]), F32)

    ext_scr[halo:halo + ROWS, :] = u
    y = b + w[0:1] * ext_scr[pl.ds(halo - 3, ROWS), :]
    y = y + w[1:2] * ext_scr[pl.ds(halo - 2, ROWS), :]
    y = y + w[2:3] * ext_scr[pl.ds(halo - 1, ROWS), :]
    y = y + w[3:4] * u

    @pl.when(c == 0)
    def _():
        ext_scr[0:halo, :] = ext_scr[first_valid:first_valid + halo, :]

    @pl.when(c != 0)
    def _():
        ext_scr[0:halo, :] = ext_scr[ROWS:ROWS + halo, :]

    return y


def _alias_prev(in_specs, args, y_prev):
    if y_prev is None:
        return {}
    in_specs.append(pl.BlockSpec(memory_space=pl.ANY))
    args.append(y_prev)
    return {len(args) - 1: 0}


def _ssd_body(xs_ref, bm_ref, cm_ref, z_ref, dtc_ref, lac_ref, lar_ref, dx_ref, ng_ref, *rest,
              seg_len, n_chunks, first_valid, has_s0, has_prev, conv):
    rest = list(rest)
    cw = [rest.pop(0) for _ in range(6)] if conv else None
    s0_ref = rest.pop(0) if has_s0 else None
    if has_prev:
        rest.pop(0)
    y_ref, sf_ref, s_scr = rest[:3]
    nseg = ROWS // seg_len
    c = pl.program_id(2)

    @pl.when(c == 0)
    def _():
        for i in range(nseg):
            if has_s0:
                s_scr[i] = jnp.concatenate([s0_ref[i, j] for j in range(SSD_HPG)], axis=0)
            else:
                s_scr[i] = jnp.zeros((GROUP_W, SSD_STATE), F32)

    if conv:
        u = jnp.concatenate([xs_ref[...], bm_ref[...], cm_ref[...]], axis=1)
        w = jnp.concatenate([cw[0][...], cw[1][...], cw[2][...]], axis=1)
        b = jnp.concatenate([cw[3][...], cw[4][...], cw[5][...]], axis=1)
        act = _silu(_conv_block(u, rest[3], w, b, c, first_valid))
        xs_in = act[:, :GROUP_W]
        bm_in = act[:, GROUP_W:GROUP_W + SSD_STATE]
        cm_in = act[:, GROUP_W + SSD_STATE:]
    else:
        xs_in, bm_in, cm_in = xs_ref[...], bm_ref[...], cm_ref[...]

    nvalid = jnp.where(c == 0, first_valid, ROWS) if first_valid != ROWS else ROWS
    row_c = lax.broadcasted_iota(I32, (ROWS, 1), 0)
    row_r = lax.broadcasted_iota(I32, (1, ROWS), 1)
    valid = row_c < nvalid
    valid_r = row_r < nvalid
    causal, causal_t, sel = _seg_masks(seg_len)
    tril = causal.astype(BF16)
    triu = causal_t.astype(BF16)
    e_i = lax.broadcasted_iota(I32, (SSD_HPG, GROUP_W), 0)
    e_l = lax.shift_right_logical(lax.broadcasted_iota(I32, (SSD_HPG, GROUP_W), 1), 6)
    expand = (e_i == e_l).astype(BF16)
    lane_head = lax.shift_right_logical(lax.broadcasted_iota(I32, (1, GROUP_W), 1), 6)

    xs = jnp.where(valid, xs_in, 0.0)
    bm = jnp.where(valid, bm_in, 0.0).astype(BF16)
    cm = jnp.where(valid, cm_in, 0.0).astype(BF16)
    z = jnp.where(valid, z_ref[...], 0.0)
    dt = jnp.where(valid, dtc_ref[0], 0.0)
    la = jnp.where(valid, lac_ref[0], 0.0)
    la_r = jnp.where(valid_r, lar_ref[0], 0.0)

    cum = _dot_01a(tril, la)
    cum_r = _dot_a01(la_r, triu)
    if seg_len == ROWS:
        cum_last = jnp.broadcast_to(cum[ROWS - 1:ROWS, :], (ROWS, SSD_HPG))
    else:
        cum_last = _dot_01a(sel.astype(BF16), cum)
    ecum_x = _dot_a01(jnp.exp(cum), expand)
    v = xs * _dot_a01(dt, expand)
    kvw = (v * _dot_a01(jnp.exp(cum_last - cum), expand)).astype(BF16)

    scores = lax.dot_general(cm, bm, (((1,), (1,)), ((), ())), preferred_element_type=F32)
    y = jnp.zeros((ROWS, GROUP_W), F32)
    for j in range(SSD_HPG):
        dj = jnp.exp(jnp.where(causal, cum[:, j:j + 1] - cum_r[j:j + 1, :], -jnp.inf))
        pj = (scores * dj).astype(BF16)
        vj = jnp.where(lane_head == j, v, 0.0).astype(BF16)
        y = y + jnp.dot(pj, vj, preferred_element_type=F32)

    y_inter = []
    for i in range(nseg):
        r0 = i * seg_len
        r_last = r0 + seg_len - 1
        s_old = s_scr[i]
        y_inter.append(lax.dot_general(cm[r0:r0 + seg_len], s_old.astype(BF16), (((1,), (1,)), ((), ())),
                                       preferred_element_type=F32))
        cs = lax.dot_general(kvw[r0:r0 + seg_len], bm[r0:r0 + seg_len], (((0,), (0,)), ((), ())),
                             preferred_element_type=F32)
        e_last = jnp.exp(cum_r[:, r_last:r_last + 1])
        dec = jnp.concatenate([jnp.broadcast_to(e_last[j:j + 1, :], (SSD_HEAD_DIM, SSD_STATE))
                               for j in range(SSD_HPG)], axis=0)
        s_scr[i] = dec * s_old + cs
    y_inter = y_inter[0] if nseg == 1 else jnp.concatenate(y_inter, axis=0)
    y = y + y_inter * ecum_x
    y = y + dx_ref[...] * xs
    y = y * _silu(z)
    y = y * lax.rsqrt(jnp.mean(y * y, axis=-1, keepdims=True) + EPS) * ng_ref[...]
    y_ref[...] = y.astype(y_ref.dtype)

    @pl.when(c == n_chunks - 1)
    def _():
        for i in range(nseg):
            s_fin = s_scr[i]
            for j in range(SSD_HPG):
                sf_ref[i, j] = s_fin[j * SSD_HEAD_DIM:(j + 1) * SSD_HEAD_DIM, :]


def _ssd_scan(xbc_src, xbc_blk0, data_block, p1, dtc, lac, lar, dx, ng, conv_wb, s0, y_prev, *, out_width,
              nb, n_chunks, seg_len, first_valid, row_block):
    t_rows = p1.shape[0]
    nseg = ROWS // seg_len
    conv = conv_wb is not None
    d_inner = SSD_GROUPS * GROUP_W
    per128 = GROUP_W // SSD_STATE
    b_blk0 = (xbc_blk0 + SSD_GROUPS) * per128
    c_blk0 = b_blk0 + SSD_GROUPS
    in_specs = [
        pl.BlockSpec((ROWS, GROUP_W), lambda b, g, c: (data_block(b, c), xbc_blk0 + g)),
        pl.BlockSpec((ROWS, SSD_STATE), lambda b, g, c: (data_block(b, c), b_blk0 + g)),
        pl.BlockSpec((ROWS, SSD_STATE), lambda b, g, c: (data_block(b, c), c_blk0 + g)),
        pl.BlockSpec((ROWS, GROUP_W), lambda b, g, c: (row_block(b, c), g)),
        pl.BlockSpec((None, 1, ROWS, SSD_HPG), lambda b, g, c: (g, 0, row_block(b, c), 0)),
        pl.BlockSpec((None, 1, ROWS, SSD_HPG), lambda b, g, c: (g, 0, row_block(b, c), 0)),
        pl.BlockSpec((None, 1, SSD_HPG, ROWS), lambda b, g, c: (g, 0, 0, row_block(b, c))),
        pl.BlockSpec((None, 1, GROUP_W), lambda b, g, c: (g, 0, 0)),
        pl.BlockSpec((None, 1, GROUP_W), lambda b, g, c: (g, 0, 0)),
    ]
    args = [xbc_src, xbc_src, xbc_src, p1, dtc, lac, lar, dx, ng]
    scratch = [pltpu.VMEM((nseg, GROUP_W, SSD_STATE), F32)]
    if conv:
        cw, cb = conv_wb
        nbw = SSD_GROUPS * per128
        for arr, rows in ((cw, CONV_W), (cb, 1)):
            in_specs += [pl.BlockSpec((rows, GROUP_W), lambda b, g, c: (0, g)),
                         pl.BlockSpec((rows, SSD_STATE), lambda b, g, c: (0, nbw + g)),
                         pl.BlockSpec((rows, SSD_STATE), lambda b, g, c: (0, nbw + SSD_GROUPS + g))]
            args += [arr, arr, arr]
        scratch.append(pltpu.VMEM((ROWS + V7X_SUBLANES, XBC_W), F32))
    state_spec = pl.BlockSpec((nseg, SSD_HPG, SSD_HEAD_DIM, SSD_STATE), lambda b, g, c: (b, g, 0, 0))
    if s0 is not None:
        in_specs.append(state_spec)
        args.append(s0)
    aliases = _alias_prev(in_specs, args, y_prev)
    n_heads = SSD_GROUPS * SSD_HPG
    return pl.pallas_call(
        functools.partial(_ssd_body, seg_len=seg_len, n_chunks=n_chunks, first_valid=first_valid,
                          has_s0=s0 is not None, has_prev=y_prev is not None, conv=conv),
        grid=(nb, SSD_GROUPS, n_chunks),
        in_specs=in_specs,
        out_specs=[pl.BlockSpec((ROWS, GROUP_W), lambda b, g, c: (row_block(b, c), g)), state_spec],
        out_shape=[jax.ShapeDtypeStruct((t_rows, out_width), BF16),
                   jax.ShapeDtypeStruct((nb * nseg, n_heads, SSD_HEAD_DIM, SSD_STATE), F32)],
        scratch_shapes=scratch,
        input_output_aliases=aliases,
        compiler_params=_cparams(("arbitrary", "arbitrary", "arbitrary")),
        name="ssd_scan",
    )(*args)


def _ret_body(q_ref, k_ref, v_ref, g_ref, cos_ref, sin_ref, lg_ref, *rest,
              seg_len, n_chunks, first_valid, has_s0, has_prev):
    s0_ref = rest[0] if has_s0 else None
    y_ref, sf_ref, s_scr = rest[int(has_s0) + int(has_prev):]
    nseg = ROWS // seg_len
    half = RET_DIM // 2
    c = pl.program_id(2)

    @pl.when(c == 0)
    def _():
        for i in range(nseg):
            for hh in range(RET_HB):
                if has_s0:
                    s_scr[i, hh] = s0_ref[i, hh]
                else:
                    s_scr[i, hh] = jnp.zeros((RET_DIM, RET_DIM), F32)

    nvalid = jnp.where(c == 0, first_valid, ROWS) if first_valid != ROWS else ROWS
    row_c = lax.broadcasted_iota(I32, (ROWS, 1), 0)
    row_r = lax.broadcasted_iota(I32, (1, ROWS), 1)
    valid = row_c < nvalid
    causal, _, _ = _seg_masks(seg_len)
    pos_c = jnp.minimum(jnp.bitwise_and(row_c, seg_len - 1) + 1, nvalid).astype(F32)
    pos_r = jnp.minimum(jnp.bitwise_and(row_r, seg_len - 1) + 1, nvalid).astype(F32)
    last = jnp.minimum(seg_len, nvalid).astype(F32) if first_valid != ROWS else float(seg_len)
    cos = cos_ref[...]
    sin = sin_ref[...]

    def rot(x):
        x1, x2 = x[:, :half], x[:, half:]
        return jnp.concatenate([x1 * cos - x2 * sin, x1 * sin + x2 * cos], axis=1)

    for hh in range(RET_HB):
        cols = slice(hh * RET_DIM, (hh + 1) * RET_DIM)
        lg_c = lg_ref[hh][:, 0:1]
        cum = pos_c * lg_c
        cum_r = pos_r * lg_c
        cum_last = last * lg_c
        q = rot(jnp.where(valid, q_ref[:, cols], 0.0)).astype(BF16)
        k = (rot(jnp.where(valid, k_ref[:, cols], 0.0)) * (RET_DIM ** -0.5)).astype(BF16)
        v = jnp.where(valid, v_ref[:, cols], 0.0)
        gate = jnp.where(valid, g_ref[:, cols], 0.0)

        decay = jnp.exp(jnp.where(causal, cum - cum_r, -jnp.inf))
        scores = lax.dot_general(q, k, (((1,), (1,)), ((), ())), preferred_element_type=F32)
        y = jnp.dot((scores * decay).astype(BF16), v.astype(BF16), preferred_element_type=F32)
        kvw = (v * jnp.exp(cum_last - cum)).astype(BF16)
        ecum = jnp.exp(cum)
        e_last = jnp.exp(cum_last)

        y_inter = []
        for i in range(nseg):
            r0 = i * seg_len
            s_old = s_scr[i, hh]
            y_inter.append(jnp.dot(q[r0:r0 + seg_len], s_old.astype(BF16), preferred_element_type=F32))
            cs = lax.dot_general(k[r0:r0 + seg_len], kvw[r0:r0 + seg_len], (((0,), (0,)), ((), ())),
                                 preferred_element_type=F32)
            s_scr[i, hh] = e_last * s_old + cs
        y_inter = y_inter[0] if nseg == 1 else jnp.concatenate(y_inter, axis=0)
        y = y + y_inter * ecum
        y = y * lax.rsqrt(jnp.mean(y * y, axis=-1, keepdims=True) + EPS)
        y = y * _silu(gate)
        y_ref[:, cols] = y.astype(y_ref.dtype)

    @pl.when(c == n_chunks - 1)
    def _():
        for i in range(nseg):
            for hh in range(RET_HB):
                sf_ref[i, hh] = s_scr[i, hh]


def _ret_scan(p3, cos, sin, lg, s0, y_prev, *, out_width, out_blk0, nb, n_chunks, seg_len, first_valid,
              row_block, cs_block):
    t_rows = p3.shape[0]
    nseg = ROWS // seg_len
    half = RET_DIM // 2
    wide = RET_HB * RET_DIM
    hblk = RET_HEADS // RET_HB
    in_specs = [
        pl.BlockSpec((ROWS, wide), lambda b, h, c: (row_block(b, c), h)),
        pl.BlockSpec((ROWS, wide), lambda b, h, c: (row_block(b, c), hblk + h)),
        pl.BlockSpec((ROWS, wide), lambda b, h, c: (row_block(b, c), 2 * hblk + h)),
        pl.BlockSpec((ROWS, wide), lambda b, h, c: (row_block(b, c), 3 * hblk + h)),
        pl.BlockSpec((ROWS, half), lambda b, h, c: (cs_block(b, c), 0)),
        pl.BlockSpec((ROWS, half), lambda b, h, c: (cs_block(b, c), 0)),
        pl.BlockSpec((RET_HB, 1, V7X_LANES), lambda b, h, c: (h, 0, 0)),
    ]
    args = [p3, p3, p3, p3, cos, sin, lg]
    state_spec = pl.BlockSpec((nseg, RET_HB, RET_DIM, RET_DIM), lambda b, h, c: (b, h, 0, 0))
    if s0 is not None:
        in_specs.append(state_spec)
        args.append(s0)
    aliases = _alias_prev(in_specs, args, y_prev)
    return pl.pallas_call(
        functools.partial(_ret_body, seg_len=seg_len, n_chunks=n_chunks, first_valid=first_valid,
                          has_s0=s0 is not None, has_prev=y_prev is not None),
        grid=(nb, hblk, n_chunks),
        in_specs=in_specs,
        out_specs=[pl.BlockSpec((ROWS, wide), lambda b, h, c: (row_block(b, c), out_blk0 + h)),
                   state_spec],
        out_shape=[jax.ShapeDtypeStruct((t_rows, out_width), BF16),
                   jax.ShapeDtypeStruct((nb * nseg, RET_HEADS, RET_DIM, RET_DIM), F32)],
        scratch_shapes=[pltpu.VMEM((nseg, RET_HB, RET_DIM, RET_DIM), F32)],
        input_output_aliases=aliases,
        compiler_params=_cparams(("arbitrary", "arbitrary", "arbitrary")),
        name="ret_scan",
    )(*args)


def _lru_body(x_ref, gate_ref, wa_ref, wx_ref, ba_ref, bx_ref, sp_ref, h0_ref, *rest,
              seg_len, n_chunks, first_valid, mark_pos0, has_prev, conv):
    rest = list(rest)
    cw = [rest.pop(0) for _ in range(2)] if conv else None
    if has_prev:
        rest.pop(0)
    y_ref, hl_ref, carry = rest[:3]
    c = pl.program_id(2)
    nvalid = jnp.where(c == 0, first_valid, ROWS) if first_valid != ROWS else ROWS
    row_c = lax.broadcasted_iota(I32, (ROWS, 1), 0)
    valid = row_c < nvalid

    if conv:
        x_in = _conv_block(x_ref[...], rest[3], cw[0][...], cw[1][...], c, first_valid)
    else:
        x_in = x_ref[...]
    x = jnp.where(valid, x_in, 0.0)
    xb = x.astype(BF16)
    r = _sigmoid(jnp.dot(xb, wa_ref[...], preferred_element_type=F32) + ba_ref[...])
    ig = _sigmoid(jnp.dot(xb, wx_ref[...], preferred_element_type=F32) + bx_ref[...])
    log_a = -LRU_C * r * sp_ref[...]
    a = jnp.exp(log_a)
    mult = jnp.sqrt(-jnp.tanh(log_a) * (a * a + 1.0))
    if mark_pos0:
        mult = jnp.where(jnp.logical_and(c == 0, row_c == 0), 1.0, mult)
    bterm = mult * ig * x
    a = jnp.where(valid, a, 1.0)
    bterm = jnp.where(valid, bterm, 0.0)

    sub = lax.broadcasted_iota(I32, (V7X_SUBLANES, PAIR_W), 0)
    if seg_len == ROWS:
        @pl.when(c == 0)
        def _():
            carry[...] = h0_ref[...]
        h_prev = carry[...]
    tiles = []
    for t in range(ROWS // V7X_SUBLANES):
        at = a[8 * t:8 * t + 8]
        bt = bterm[8 * t:8 * t + 8]
        for d in (1, 2, 4):
            a_sh = jnp.where(sub >= d, pltpu.roll(at, d, 0), 1.0)
            b_sh = jnp.where(sub >= d, pltpu.roll(bt, d, 0), 0.0)
            bt = at * b_sh + bt
            at = at * a_sh
        if seg_len != ROWS:
            h_prev = h0_ref[t:t + 1, :]
        ht = bt + at * h_prev
        h_prev = ht[7:8, :]
        if seg_len != ROWS:
            hl_ref[t:t + 1, :] = h_prev
        tiles.append(ht)
    h = jnp.concatenate(tiles, axis=0)
    if seg_len == ROWS:
        carry[...] = h_prev

        @pl.when(c == n_chunks - 1)
        def _():
            hl_ref[...] = h_prev

    g = gate_ref[...]
    gelu = 0.5 * g * (1.0 + jnp.tanh(math.sqrt(2.0 / math.pi) * (g + 0.044715 * (g * g * g))))
    y_ref[...] = (h * gelu).astype(y_ref.dtype)


def _lru_scan(x_src, x_blk0, data_block, p_lru, wa_p, wx_p, ba_p, bx_p, sp_p, conv_wb, h0, y_prev, *,
              nb, n_chunks, seg_len, first_valid, mark_pos0, row_block):
    t_rows = p_lru.shape[0]
    width = p_lru.shape[1] // 2
    n_pairs = width // PAIR_W
    nseg = ROWS // seg_len
    conv = conv_wb is not None
    hrows = 1 if seg_len == ROWS else nseg
    h_spec = pl.BlockSpec((None, hrows, PAIR_W), lambda b, p, c: (b, 0, p))
    vec_spec = pl.BlockSpec((None, 1, PAIR_W), lambda b, p, c: (p, 0, 0))
    in_specs = [
        pl.BlockSpec((ROWS, PAIR_W), lambda b, p, c: (data_block(b, c), x_blk0 + p)),
        pl.BlockSpec((ROWS, PAIR_W), lambda b, p, c: (row_block(b, c), p)),
        pl.BlockSpec((None, PAIR_W, PAIR_W), lambda b, p, c: (p, 0, 0)),
        pl.BlockSpec((None, PAIR_W, PAIR_W), lambda b, p, c: (p, 0, 0)),
        vec_spec, vec_spec, vec_spec, h_spec,
    ]
    args = [x_src, p_lru, wa_p, wx_p, ba_p, bx_p, sp_p, h0]
    scratch = [pltpu.VMEM((1, PAIR_W), F32)]
    if conv:
        cw, cb = conv_wb
        in_specs += [pl.BlockSpec((CONV_W, PAIR_W), lambda b, p, c: (0, p)),
                     pl.BlockSpec((1, PAIR_W), lambda b, p, c: (0, p))]
        args += [cw, cb]
        scratch.append(pltpu.VMEM((ROWS + V7X_SUBLANES, PAIR_W), F32))
    aliases = _alias_prev(in_specs, args, y_prev)
    return pl.pallas_call(
        functools.partial(_lru_body, seg_len=seg_len, n_chunks=n_chunks, first_valid=first_valid,
                          mark_pos0=mark_pos0, has_prev=y_prev is not None, conv=conv),
        grid=(nb, n_pairs, n_chunks),
        in_specs=in_specs,
        out_specs=[pl.BlockSpec((ROWS, PAIR_W), lambda b, p, c: (row_block(b, c), p)), h_spec],
        out_shape=[jax.ShapeDtypeStruct((t_rows, width), BF16),
                   jax.ShapeDtypeStruct((nb, hrows, width), F32)],
        scratch_shapes=scratch,
        input_output_aliases=aliases,
        compiler_params=_cparams(("arbitrary", "arbitrary", "arbitrary")),
        name="rglru",
    )(*args)


def _router_body(h_ref, g_ref, wr_ref, br_ref, hn_ref, route_ref, cnt_ref, carry, *, tm):
    i = pl.program_id(0)

    @pl.when(i == 0)
    def _():
        carry[...] = jnp.zeros_like(carry)

    x = h_ref[...]
    hn = x * lax.rsqrt(jnp.mean(x * x, axis=-1, keepdims=True) + EPS) * g_ref[...]
    hn_ref[...] = hn

    h_hi = hn.astype(BF16)
    h_lo = (hn - h_hi.astype(F32)).astype(BF16)
    w = wr_ref[...]
    w_hi = w.astype(BF16)
    w_lo = (w - w_hi.astype(F32)).astype(BF16)
    logits = (jnp.dot(h_hi, w_hi, preferred_element_type=F32)
              + jnp.dot(h_hi, w_lo, preferred_element_type=F32)
              + jnp.dot(h_lo, w_hi, preferred_element_type=F32)) + br_ref[...]

    lane_i = lax.broadcasted_iota(I32, (tm, V7X_LANES), 1)
    lane = lane_i.astype(F32)
    big = float(4 * V7X_LANES)
    neg = -jnp.inf
    is_g = jnp.logical_and(lane_i >= N_EXPERTS, lane_i < N_EXPERTS + MOE_GROUPS)
    glog = jnp.where(is_g, logits, neg)
    gmax = jnp.max(glog, axis=-1, keepdims=True)
    gsel = jnp.min(jnp.where(glog == gmax, lane, big), axis=-1, keepdims=True) - float(N_EXPERTS)
    gsum = jnp.sum(jnp.exp(glog - gmax), axis=-1, keepdims=True)
    pg = 1.0 / gsum
    lo = gsel * float(MOE_PER_GROUP)
    in_grp = jnp.logical_and(lane >= lo, lane < lo + MOE_PER_GROUP)
    elog = jnp.where(in_grp, logits, neg)
    emax = jnp.max(elog, axis=-1, keepdims=True)
    eexp = jnp.exp(elog - emax)
    ep = eexp / jnp.sum(eexp, axis=-1, keepdims=True)
    ep = jnp.where(in_grp, ep, -1.0)
    v1 = jnp.max(ep, axis=-1, keepdims=True)
    i1 = jnp.min(jnp.where(ep == v1, lane, big), axis=-1, keepdims=True)
    ep2 = jnp.where(lane == i1, -1.0, ep)
    v2 = jnp.max(ep2, axis=-1, keepdims=True)
    i2 = jnp.min(jnp.where(ep2 == v2, lane, big), axis=-1, keepdims=True)
    vs = v1 + v2
    w1 = v1 / vs * pg
    w2 = v2 / vs * pg

    oh1 = lane == i1
    oh2 = lane == i2
    oh = jnp.logical_or(oh1, oh2).astype(BF16)
    ti = lax.broadcasted_iota(I32, (tm, tm), 0)
    si = lax.broadcasted_iota(I32, (tm, tm), 1)
    before = (si < ti).astype(BF16)
    tot = jnp.dot(before, oh, preferred_element_type=F32) + carry[...]
    rank1 = jnp.sum(jnp.where(oh1, tot, 0.0), axis=-1, keepdims=True)
    rank2 = jnp.sum(jnp.where(oh2, tot, 0.0), axis=-1, keepdims=True)
    carry[...] = carry[...] + jnp.sum(oh.astype(F32), axis=0, keepdims=True)
    cnt_ref[...] = carry[...]

    out = jnp.where(lane_i == 0, i1, 0.0)
    out = jnp.where(lane_i == 1, i2, out)
    out = jnp.where(lane_i == 2, rank1, out)
    out = jnp.where(lane_i == 3, rank2, out)
    out = jnp.where(lane_i == 4, w1, out)
    out = jnp.where(lane_i == 5, w2, out)
    route_ref[...] = out


def _router(h, g, wr, br, tm):
    t_rows, d = h.shape
    return pl.pallas_call(
        functools.partial(_router_body, tm=tm),
        grid=(t_rows // tm,),
        in_specs=[pl.BlockSpec((tm, d), lambda i: (i, 0)),
                  pl.BlockSpec((1, d), lambda i: (0, 0)),
                  pl.BlockSpec((d, V7X_LANES), lambda i: (0, 0)),
                  pl.BlockSpec((1, V7X_LANES), lambda i: (0, 0))],
        out_specs=[pl.BlockSpec((tm, d), lambda i: (i, 0)),
                   pl.BlockSpec((tm, V7X_LANES), lambda i: (i, 0)),
                   pl.BlockSpec((1, V7X_LANES), lambda i: (0, 0))],
        out_shape=[jax.ShapeDtypeStruct((t_rows, d), F32),
                   jax.ShapeDtypeStruct((t_rows, V7X_LANES), F32),
                   jax.ShapeDtypeStruct((1, V7X_LANES), F32)],
        scratch_shapes=[pltpu.VMEM((1, V7X_LANES), F32)],
        compiler_params=_cparams(("arbitrary",)),
        name="moe_router",
    )(h, g.reshape(1, d), wr, br)


def _expert_body(te_ref, tfirst_ref, nt_ref, dst_cur, dst_nxt, hn_hbm, w1_hbm, w3_hbm, w2_hbm,
                 o_hbm, xbuf, ybuf, wb1, wb3, wb2, st13, st2, gsem, ssem, wsem13, wsem2,
                 *, tm, layer, nt_max, t_rows):
    i = pl.program_id(0)
    nt = nt_ref[0]
    slot = lax.rem(i, 2)

    def token_of(d):
        return jnp.where(d >= 2 * t_rows, 0, jnp.where(d >= t_rows, d - t_rows, d))

    def start_gather(dst_ref, s):
        def body(r8, carry_):
            for u in range(ROW_UNROLL):
                r = r8 * ROW_UNROLL + u
                pltpu.make_async_copy(hn_hbm.at[pl.ds(token_of(dst_ref[0, 0, r]), 1)],
                                      xbuf.at[s, pl.ds(r, 1)], gsem.at[s]).start()
            return carry_
        lax.fori_loop(0, tm // ROW_UNROLL, body, 0)

    def wait_gather(s):
        pltpu.make_async_copy(hn_hbm.at[pl.ds(0, tm)], xbuf.at[s], gsem.at[s]).wait()

    def start_scatter():
        def body(r8, carry_):
            for u in range(ROW_UNROLL):
                r = r8 * ROW_UNROLL + u
                pltpu.make_async_copy(ybuf.at[pl.ds(r, 1)], o_hbm.at[pl.ds(dst_cur[0, 0, r], 1)],
                                      ssem.at[0]).start()
            return carry_
        lax.fori_loop(0, tm // ROW_UNROLL, body, 0)

    def wait_scatter():
        pltpu.make_async_copy(ybuf, o_hbm.at[pl.ds(0, tm)], ssem.at[0]).wait()

    @pl.when(i == 0)
    def _():
        start_gather(dst_cur, 0)

    @pl.when(i + 1 < nt)
    def _():
        start_gather(dst_nxt, 1 - slot)

    @pl.when(jnp.logical_and(i < nt, tfirst_ref[i] == 1))
    def _():
        e = te_ref[i]
        n13 = 2 * (wb1.shape[0] // W13_ROWS)
        n2 = wb2.shape[0] // W2_ROWS

        def c13(k):
            src = w1_hbm if k % 2 == 0 else w3_hbm
            r0 = (k // 2) * W13_ROWS
            return pltpu.make_async_copy(src.at[layer, e, pl.ds(r0, W13_ROWS)], st13.at[k % W_RING],
                                         wsem13.at[k % W_RING])

        def c2(k):
            return pltpu.make_async_copy(w2_hbm.at[layer, e, pl.ds(k * W2_ROWS, W2_ROWS)],
                                         st2.at[k % W_RING], wsem2.at[k % W_RING])

        for k in range(W_RING):
            c13(k).start()
        for k in range(W_RING):
            c2(k).start()
        for k in range(n13):
            c13(k).wait()
            dstw = wb1 if k % 2 == 0 else wb3
            r0 = (k // 2) * W13_ROWS
            dstw[r0:r0 + W13_ROWS, :] = st13[k % W_RING].astype(BF16)
            if k + W_RING < n13:
                c13(k + W_RING).start()
        for k in range(n2):
            c2(k).wait()
            wb2[k * W2_ROWS:(k + 1) * W2_ROWS, :] = st2[k % W_RING].astype(BF16)
            if k + W_RING < n2:
                c2(k + W_RING).start()

    @pl.when(i < nt)
    def _():
        wait_gather(slot)
        x = xbuf[slot].astype(BF16)
        a = jnp.dot(x, wb1[...], preferred_element_type=F32)
        b = jnp.dot(x, wb3[...], preferred_element_type=F32)
        hdn = (_silu(a) * b).astype(BF16)

        @pl.when(i > 0)
        def _():
            wait_scatter()

        ybuf[...] = jnp.dot(hdn, wb2[...], preferred_element_type=F32)
        start_scatter()

    @pl.when(jnp.logical_or(i == nt, jnp.logical_and(i == nt_max - 1, i < nt)))
    def _():
        wait_scatter()


def _experts(hn, dst, tile_expert, tile_first, n_tiles, w1, w3, w2, *, layer, tm, nt_max):
    t_rows, d = hn.shape
    ff = w1.shape[3]
    smem_spec = lambda f: pl.BlockSpec((1, 1, tm), f, memory_space=pltpu.SMEM)
    grid_spec = pltpu.PrefetchScalarGridSpec(
        num_scalar_prefetch=3,
        grid=(nt_max,),
        in_specs=[
            smem_spec(lambda i, *_: (i, 0, 0)),
            smem_spec(lambda i, *_: (jnp.minimum(i + 1, nt_max - 1), 0, 0)),
            pl.BlockSpec(memory_space=pl.ANY),
            pl.BlockSpec(memory_space=pl.ANY),
            pl.BlockSpec(memory_space=pl.ANY),
            pl.BlockSpec(memory_space=pl.ANY),
        ],
        out_specs=pl.BlockSpec(memory_space=pl.ANY),
        scratch_shapes=[
            pltpu.VMEM((2, tm, d), F32), pltpu.VMEM((tm, d), F32),
            pltpu.VMEM((d, ff), BF16), pltpu.VMEM((d, ff), BF16), pltpu.VMEM((ff, d), BF16),
            pltpu.VMEM((W_RING, W13_ROWS, ff), F32), pltpu.VMEM((W_RING, W2_ROWS, d), F32),
            pltpu.SemaphoreType.DMA((2,)), pltpu.SemaphoreType.DMA((1,)),
            pltpu.SemaphoreType.DMA((W_RING,)), pltpu.SemaphoreType.DMA((W_RING,)),
        ],
    )
    return pl.pallas_call(
        functools.partial(_expert_body, tm=tm, layer=layer, nt_max=nt_max, t_rows=t_rows),
        grid_spec=grid_spec,
        out_shape=jax.ShapeDtypeStruct((2 * t_rows + tm, d), F32),
        compiler_params=_cparams(("arbitrary",)),
        name="moe_experts",
    )(tile_expert, tile_first, n_tiles, dst, dst, hn, w1, w3, w2)


def _combine_body(h_ref, y0_ref, y1_ref, route_ref, *rest, with_norm):
    route = route_ref[...]
    h_new = h_ref[...] + route[:, 4:5] * y0_ref[...] + route[:, 5:6] * y1_ref[...]
    if with_norm:
        g_ref, o_ref, n_ref = rest
        o_ref[...] = h_new
        y = h_new * lax.rsqrt(jnp.mean(h_new * h_new, axis=-1, keepdims=True) + EPS)
        n_ref[...] = (y * g_ref[...]).astype(n_ref.dtype)
    else:
        rest[0][...] = h_new


def _combine(h, y2, route, g_next, *, tm):
    t_rows, d = h.shape
    n = t_rows // tm
    row = pl.BlockSpec((tm, d), lambda i: (i, 0))
    in_specs = [row, row,
                pl.BlockSpec((tm, d), lambda i: (n + i, 0)),
                pl.BlockSpec((tm, V7X_LANES), lambda i: (i, 0))]
    args = [h, y2, y2, route]
    with_norm = g_next is not None
    if with_norm:
        in_specs.append(pl.BlockSpec((1, d), lambda i: (0, 0)))
        args.append(g_next.reshape(1, d))
        out_specs = [row, row]
        out_shape = [jax.ShapeDtypeStruct((t_rows, d), F32), jax.ShapeDtypeStruct((t_rows, d), BF16)]
    else:
        out_specs = row
        out_shape = jax.ShapeDtypeStruct((t_rows, d), F32)
    return pl.pallas_call(
        functools.partial(_combine_body, with_norm=with_norm),
        grid=(t_rows // tm,),
        in_specs=in_specs,
        out_specs=out_specs,
        out_shape=out_shape,
        compiler_params=_cparams(("arbitrary",)),
        name="moe_combine",
    )(*args)


def _moe(h, layer, g, w_rg, b_rg, w_re, b_re, w1, w3, w2, g_next, *, tm_tok, tm_e):
    t_rows, d = h.shape
    pad = V7X_LANES - N_EXPERTS - MOE_GROUPS
    wr = jnp.concatenate([w_re[layer], w_rg[layer], jnp.zeros((d, pad), F32)], axis=1)
    br = jnp.concatenate([b_re[layer], b_rg[layer], jnp.zeros((pad,), F32)]).reshape(1, V7X_LANES)
    hn, route, counts = _router(h, g, wr, br, tm_tok)

    nt_max = (2 * t_rows) // tm_e + N_EXPERTS
    e1 = route[:, 0].astype(I32)
    e2 = route[:, 1].astype(I32)
    r1 = route[:, 2].astype(I32)
    r2 = route[:, 3].astype(I32)
    cnt = counts[0, :N_EXPERTS].astype(I32)
    nt_e = (cnt + tm_e - 1) // tm_e
    tile_end = jnp.cumsum(nt_e)
    tile_start = tile_end - nt_e
    n_tiles = tile_end[-1]
    tile_ids = jnp.arange(nt_max, dtype=I32)
    te = jnp.minimum(jnp.sum((tile_ids[:, None] >= tile_end[None, :]).astype(I32), axis=1), N_EXPERTS - 1)
    live = tile_ids < n_tiles
    k_in_e = tile_ids - tile_start[te]
    tfirst = jnp.where(jnp.logical_and(live, k_in_e == 0), 1, 0).astype(I32)
    row_off = tile_start * tm_e
    d1 = row_off[e1] + r1
    d2 = row_off[e2] + r2
    tok = jnp.arange(t_rows, dtype=I32)
    dump = 2 * t_rows + jnp.arange(nt_max * tm_e, dtype=I32) % tm_e
    dst = dump.at[d1].set(tok).at[d2].set(t_rows + tok)

    y2 = _experts(hn, dst.reshape(nt_max, 1, tm_e), te, tfirst, n_tiles.reshape(1), w1, w3, w2,
                  layer=layer, tm=tm_e, nt_max=nt_max)
    return _combine(h, y2, route, g_next, tm=_pick(t_rows, (256, 128)))


def _conv_sample(u, buf, w, b):
    full = jnp.concatenate([buf, u], axis=1)
    l = u.shape[1]
    y = b + w[0] * full[:, 0:l]
    for t in range(1, CONV_W):
        y = y + w[t] * full[:, t:t + l]
    return y, full[:, -(CONV_W - 1):]


def _pair_blocks(w):
    nblk, bw, _ = w.shape
    w = w.reshape(nblk // 2, 2, bw, bw)
    z = jnp.zeros((nblk // 2, bw, bw), w.dtype)
    top = jnp.concatenate([w[:, 0], z], axis=2)
    bot = jnp.concatenate([z, w[:, 1]], axis=2)
    return jnp.concatenate([top, bot], axis=1).astype(BF16)


def kernel(x_prompt, x_sample, state_ssd_conv, state_ssd, state_ret, state_lru_conv, state_lru, meta_tokens, norm_mix, norm_ffn, norm_final, w_in0, ssd_conv_w, ssd_conv_b, ssd_dt_bias, ssd_a_log, ssd_d, ssd_norm_g, w_out0, w_in1, lru_conv_w, lru_conv_b, lru_wa, lru_ba, lru_wx, lru_bx, lru_lambda, w_out1, moe_w_rg, moe_b_rg, moe_w_re, moe_b_re, moe_w1, moe_w3, moe_w2):
    nb, seq, d = x_prompt.shape
    nbs, ls, _ = x_sample.shape
    assert seq % ROWS == 0 and ROWS % ls == 0 and (nbs * ls) % ROWS == 0 and ls >= CONV_W - 1
    n_xblk = seq // ROWS
    blk_p = n_xblk + 1
    lp_pad = blk_p * ROWS
    tp = nb * lp_pad
    ts = nbs * ls
    t_all = tp + ts
    nbs_blk = ts // ROWS

    def rb_prompt(b, c):
        return b * blk_p + lax.rem(c + n_xblk, blk_p)

    def rb_sample(b, c):
        return nb * blk_p + b

    def rb_own(b, c):
        return b

    def _last_prompt_rows(p, c0, c1):
        return jnp.stack([p[b * lp_pad + seq - (CONV_W - 1):b * lp_pad + seq, c0:c1] for b in range(nb)])

    prompt_kw = dict(nb=nb, n_chunks=blk_p, seg_len=ROWS, first_valid=N_META, row_block=rb_prompt)
    sample_kw = dict(nb=nbs_blk, n_chunks=1, seg_len=ls, first_valid=ROWS, row_block=rb_sample)

    xp = jnp.concatenate([x_prompt, jnp.broadcast_to(meta_tokens[None], (nb, N_META, d)),
                          jnp.zeros((nb, lp_pad - seq - N_META, d), F32)], axis=1)
    h = jnp.concatenate([xp.reshape(tp, d), x_sample.reshape(ts, d)], axis=0)

    tm_tok = _pick(t_all, (512, 256, 128))
    tm_mm = _pick(t_all, (1216, 608, 512, 256, 128))

    d_inner = SSD_GROUPS * GROUP_W
    conv_dim = d_inner + 2 * SSD_GROUPS * SSD_STATE
    n_heads = SSD_GROUPS * SSD_HPG
    c_zx = d_inner + conv_dim
    c_qkvg = w_in0.shape[1] - c_zx - n_heads
    hn = _rmsnorm(h, norm_mix[0], BF16, tm_tok)
    w_in0_t = w_in0.T
    p1 = _matmul_t(hn, w_in0_t, 0, c_zx, tm_mm, 512)
    pdt = _matmul_t(hn, w_in0_t, c_zx, n_heads, tm_mm, n_heads)
    p3 = _matmul_t(hn, w_in0_t, c_zx + n_heads, c_qkvg, tm_mm, 512)

    dt = jax.nn.softplus(pdt + ssd_dt_bias)
    la = dt * (-jnp.exp(ssd_a_log))
    dt3 = dt.reshape(t_all, SSD_GROUPS, SSD_HPG)
    la3 = la.reshape(t_all, SSD_GROUPS, SSD_HPG)
    dtc = dt3.transpose(1, 0, 2)[:, None]
    lac = la3.transpose(1, 0, 2)[:, None]
    lar = la3.transpose(1, 2, 0)[:, None]
    dx = jnp.repeat(ssd_d, SSD_HEAD_DIM).reshape(SSD_GROUPS, 1, GROUP_W)
    ng = ssd_norm_g.reshape(SSD_GROUPS, 1, GROUP_W)

    xbc_s, conv_s = _conv_sample(p1[tp:, d_inner:].reshape(nbs, ls, conv_dim), state_ssd_conv,
                                 ssd_conv_w, ssd_conv_b)
    xbc_s = jax.nn.silu(xbc_s).reshape(ts, conv_dim)
    conv_p = _last_prompt_rows(p1, d_inner, c_zx)

    mix_w = d_inner + RET_HEADS * RET_DIM
    ssd_common = (p1, dtc, lac, lar, dx, ng)
    mix, ssd_p = _ssd_scan(p1, d_inner // GROUP_W, rb_prompt, *ssd_common,
                           (ssd_conv_w, ssd_conv_b.reshape(1, conv_dim)), None, None,
                           out_width=mix_w, **prompt_kw)
    mix, ssd_s = _ssd_scan(xbc_s, 0, rb_own, *ssd_common, None, jnp.swapaxes(state_ssd, 2, 3), mix,
                           out_width=mix_w, **sample_kw)

    half = RET_DIM // 2
    inv = 1.0 / (ROPE_BASE ** (jnp.arange(half, dtype=F32) / half))
    pos_p = jnp.concatenate([N_META + jnp.arange(seq, dtype=I32), jnp.arange(N_META, dtype=I32),
                             jnp.zeros((lp_pad - seq - N_META,), I32)])
    pos_s = PAST_LEN + (jnp.arange(ROWS, dtype=I32) % ls)
    ang = jnp.concatenate([pos_p, pos_s]).astype(F32)[:, None] * inv[None, :]
    cos, sin = jnp.cos(ang), jnp.sin(ang)
    log_gamma = jnp.log1p(-jnp.exp2(-5.0 - jnp.arange(RET_HEADS, dtype=F32)))
    lg = jnp.broadcast_to(log_gamma[:, None, None], (RET_HEADS, 1, V7X_LANES))
    ret_kw = dict(out_width=mix_w, out_blk0=d_inner // (RET_HB * RET_DIM))
    mix, ret_p = _ret_scan(p3, cos, sin, lg, None, mix,
                           cs_block=lambda b, c: lax.rem(c + n_xblk, blk_p), **ret_kw, **prompt_kw)
    mix, ret_s = _ret_scan(p3, cos, sin, lg, state_ret, mix, cs_block=lambda b, c: blk_p,
                           **ret_kw, **sample_kw)
    tm_o = _pick(t_all, (608, 512, 256, 128))
    h = _matmul(mix, w_out0.astype(BF16), d, tm_o, 512, res=h)
    moe_kw = dict(tm_tok=tm_tok, tm_e=min(MOE_TM, tm_tok))
    moe_w = (moe_w_rg, moe_b_rg, moe_w_re, moe_b_re, moe_w1, moe_w3, moe_w2)
    h, hn = _moe(h, 0, norm_ffn[0], *moe_w, norm_mix[1], **moe_kw)

    width = lru_lambda.shape[0]
    p_lru = _matmul(hn, w_in1, 2 * width, tm_mm, 512)
    xc_s, lconv_s = _conv_sample(p_lru[tp:, width:].reshape(nbs, ls, width), state_lru_conv,
                                 lru_conv_w, lru_conv_b)
    xc_s = xc_s.reshape(ts, width)
    lconv_p = _last_prompt_rows(p_lru, width, 2 * width)
    n_pairs = width // PAIR_W
    lru_common = (p_lru, _pair_blocks(lru_wa), _pair_blocks(lru_wx), lru_ba.reshape(n_pairs, 1, PAIR_W),
                  lru_bx.reshape(n_pairs, 1, PAIR_W),
                  jax.nn.softplus(-lru_lambda).reshape(n_pairs, 1, PAIR_W))
    y_lru, lru_p = _lru_scan(p_lru, n_pairs, rb_prompt, *lru_common,
                             (lru_conv_w, lru_conv_b.reshape(1, width)), jnp.zeros((nb, 1, width), F32),
                             None, mark_pos0=True, **prompt_kw)
    y_lru, lru_s = _lru_scan(xc_s, 0, rb_own, *lru_common, None,
                             state_lru.reshape(nbs_blk, ROWS // ls, width), y_lru, mark_pos0=False,
                             **sample_kw)
    h = _matmul(y_lru, w_out1, d, tm_o, 512, res=h)
    h = _moe(h, 1, norm_ffn[1], *moe_w, None, **moe_kw)

    y_prompt = _rmsnorm(h, norm_final, F32, ROWS, n_out_blocks=nb * n_xblk,
                        in_block=lambda i: (i // n_xblk) * blk_p + lax.rem(i, n_xblk))
    y_sample = _rmsnorm(h, norm_final, F32, ROWS, n_out_blocks=nbs_blk, in_block=lambda i: nb * blk_p + i)
    return (y_prompt.reshape(nb, seq, d), y_sample.reshape(nbs, ls, d),
            conv_p, jnp.swapaxes(ssd_p, 2, 3), ret_p, lconv_p, lru_p.reshape(nb, width),
            conv_s, jnp.swapaxes(ssd_s, 2, 3), ret_s, lconv_s, lru_s.reshape(nbs, width))
```

```python
import functools
import math

import jax
import jax.numpy as jnp
from jax import lax
from jax.experimental import pallas as pl
from jax.experimental.pallas import tpu as pltpu

F32, BF16, I32 = jnp.float32, jnp.bfloat16, jnp.int32

N_META = 16
CONV_W = 4
EPS = 1e-6
PAST_LEN = 16384
SSD_HEAD_DIM = 64
SSD_GROUPS = 8
SSD_HPG = 8
SSD_STATE = 128
RET_HEADS = 16
RET_DIM = 256
RET_HB_PROMPT = 4
RET_HB_SAMPLE = 2
ROPE_BASE = 10000.0
LRU_C = 8.0
MOE_GROUPS = 4
MOE_PER_GROUP = 8
N_EXPERTS = MOE_GROUPS * MOE_PER_GROUP

V7X_LANES = 128
V7X_SUBLANES = 8
V7X_VMEM_LIMIT = 56 * 1024 * 1024
ROWS = 128
GROUP_W = SSD_HPG * SSD_HEAD_DIM
XBC_W = GROUP_W + 2 * SSD_STATE
PAIR_W = 2 * 320
MOE_TM = 512
W_RING = 4
W13_ROWS = 512
W2_ROWS = 64
ROW_UNROLL = 8


def _cparams(sem):
    return pltpu.CompilerParams(dimension_semantics=sem, vmem_limit_bytes=V7X_VMEM_LIMIT)


def _pick(n, cands):
    for c in cands:
        if n % c == 0:
            return c
    raise ValueError(f"no tile for {n} in {cands}")


def _sigmoid(x):
    return 1.0 / (1.0 + jnp.exp(-x))


def _silu(x):
    return x * _sigmoid(x)


def _rmsnorm_body(x_ref, g_ref, o_ref):
    x = x_ref[...]
    y = x * lax.rsqrt(jnp.mean(x * x, axis=-1, keepdims=True) + EPS)
    o_ref[...] = (y * g_ref[...]).astype(o_ref.dtype)


def _rmsnorm(x, g, out_dtype, tm, n_out_blocks=None, in_block=None):
    m, d = x.shape
    n_blocks = m // tm if n_out_blocks is None else n_out_blocks
    in_map = (lambda i: (i, 0)) if in_block is None else (lambda i: (in_block(i), 0))
    return pl.pallas_call(
        _rmsnorm_body,
        grid=(n_blocks,),
        in_specs=[pl.BlockSpec((tm, d), in_map), pl.BlockSpec((1, d), lambda i: (0, 0))],
        out_specs=pl.BlockSpec((tm, d), lambda i: (i, 0)),
        out_shape=jax.ShapeDtypeStruct((n_blocks * tm, d), out_dtype),
        compiler_params=_cparams(("arbitrary",)),
        name="rmsnorm",
    )(x, g.reshape(1, d))


def _embed_body(xp_ref, xs_ref, meta_ref, g_ref, h_ref, hn_ref, *, nb, blk_p, n_xblk):
    i = pl.program_id(0)
    k = lax.rem(i, blk_p)
    is_prompt = i < nb * blk_p

    def emit(x):
        h_ref[...] = x
        y = x * lax.rsqrt(jnp.mean(x * x, axis=-1, keepdims=True) + EPS)
        hn_ref[...] = (y * g_ref[...]).astype(hn_ref.dtype)

    @pl.when(jnp.logical_and(is_prompt, k < n_xblk))
    def _():
        emit(xp_ref[...])

    @pl.when(jnp.logical_and(is_prompt, k == n_xblk))
    def _():
        meta = meta_ref[...]
        emit(jnp.concatenate([meta, jnp.zeros((ROWS - meta.shape[0], meta.shape[1]), F32)], axis=0))

    @pl.when(jnp.logical_not(is_prompt))
    def _():
        emit(xs_ref[...])


def _embed_norm(x_prompt, x_sample, meta, g):
    nb, seq, d = x_prompt.shape
    ts = x_sample.shape[0] * x_sample.shape[1]
    n_xblk = seq // ROWS
    blk_p = n_xblk + 1
    n_blocks = nb * blk_p + ts // ROWS
    row = pl.BlockSpec((ROWS, d), lambda i: (i, 0))
    return pl.pallas_call(
        functools.partial(_embed_body, nb=nb, blk_p=blk_p, n_xblk=n_xblk),
        grid=(n_blocks,),
        in_specs=[
            pl.BlockSpec((ROWS, d), lambda i: (jnp.minimum(i // blk_p, nb - 1) * n_xblk
                                               + jnp.minimum(lax.rem(i, blk_p), n_xblk - 1), 0)),
            pl.BlockSpec((ROWS, d), lambda i: (jnp.clip(i - nb * blk_p, 0, ts // ROWS - 1), 0)),
            pl.BlockSpec(meta.shape, lambda i: (0, 0)),
            pl.BlockSpec((1, d), lambda i: (0, 0)),
        ],
        out_specs=[row, row],
        out_shape=[jax.ShapeDtypeStruct((n_blocks * ROWS, d), F32),
                   jax.ShapeDtypeStruct((n_blocks * ROWS, d), BF16)],
        compiler_params=_cparams(("arbitrary",)),
        name="embed_norm",
    )(x_prompt.reshape(nb * seq, d), x_sample.reshape(ts, d), meta, g.reshape(1, d))


def _mm_body(*refs, has_res, cast_b):
    a_ref, b_ref = refs[0], refs[1]
    r_ref = refs[2] if has_res else None
    o_ref = refs[3] if has_res else refs[2]
    if cast_b:
        bs_ref = refs[-1]

        @pl.when(pl.program_id(1) == 0)
        def _():
            bs_ref[...] = b_ref[...].astype(BF16)

        b = bs_ref[...]
    else:
        b = b_ref[...]
    acc = jnp.dot(a_ref[...], b, preferred_element_type=F32)
    if has_res:
        acc = acc + r_ref[...]
    o_ref[...] = acc


def _matmul(a, b, n_cols, tm, tn, res=None):
    m, k = a.shape
    cast_b = b.dtype != BF16
    in_specs = [pl.BlockSpec((tm, k), lambda j, i: (i, 0)),
                pl.BlockSpec((k, tn), lambda j, i: (0, j))]
    args = [a, b]
    if res is not None:
        in_specs.append(pl.BlockSpec((tm, tn), lambda j, i: (i, j)))
        args.append(res)
    return pl.pallas_call(
        functools.partial(_mm_body, has_res=res is not None, cast_b=cast_b),
        grid=(n_cols // tn, m // tm),
        in_specs=in_specs,
        out_specs=pl.BlockSpec((tm, tn), lambda j, i: (i, j)),
        out_shape=jax.ShapeDtypeStruct((m, n_cols), F32),
        scratch_shapes=[pltpu.VMEM((k, tn), BF16)] if cast_b else [],
        compiler_params=_cparams(("arbitrary", "arbitrary")),
        name="proj",
    )(*args)


def _mm_t_body(a_ref, bt_ref, o_ref, bs_ref):
    @pl.when(pl.program_id(1) == 0)
    def _():
        bs_ref[...] = bt_ref[...].astype(BF16)

    o_ref[...] = lax.dot_general(a_ref[...], bs_ref[...], (((1,), (1,)), ((), ())),
                                 preferred_element_type=F32)


def _matmul_t(a, bt, row0, n_cols, tm, tn):
    m, k = a.shape
    return pl.pallas_call(
        _mm_t_body,
        grid=(n_cols // tn, m // tm),
        in_specs=[pl.BlockSpec((tm, k), lambda j, i: (i, 0)),
                  pl.BlockSpec((pl.Element(tn), pl.Element(k)),
                               lambda j, i: (pl.multiple_of(row0 + j * tn, math.gcd(row0, tn)), 0))],
        out_specs=pl.BlockSpec((tm, tn), lambda j, i: (i, j)),
        out_shape=jax.ShapeDtypeStruct((m, n_cols), F32),
        scratch_shapes=[pltpu.VMEM((tn, k), BF16)],
        compiler_params=_cparams(("arbitrary", "arbitrary")),
        name="proj_t",
    )(a, bt)


def _split3(a):
    hi = a.astype(BF16)
    r1 = a - hi.astype(F32)
    mid = r1.astype(BF16)
    lo = (r1 - mid.astype(F32)).astype(BF16)
    return hi, mid, lo


def _dot_a01(a, m01):
    hi, mid, lo = _split3(a)
    return (jnp.dot(hi, m01, preferred_element_type=F32)
            + jnp.dot(mid, m01, preferred_element_type=F32)
            + jnp.dot(lo, m01, preferred_element_type=F32))


def _dot_01a(m01, a):
    hi, mid, lo = _split3(a)
    return (jnp.dot(m01, hi, preferred_element_type=F32)
            + jnp.dot(m01, mid, preferred_element_type=F32)
            + jnp.dot(m01, lo, preferred_element_type=F32))


def _seg_masks(seg_len):
    ti = lax.broadcasted_iota(I32, (ROWS, ROWS), 0)
    si = lax.broadcasted_iota(I32, (ROWS, ROWS), 1)
    if seg_len == ROWS:
        causal = si <= ti
        causal_t = ti <= si
        sel = si == ROWS - 1
    else:
        shift = int(math.log2(seg_len))
        tseg = lax.shift_right_logical(ti, shift)
        sseg = lax.shift_right_logical(si, shift)
        same = tseg == sseg
        causal = jnp.logical_and(si <= ti, same)
        causal_t = jnp.logical_and(ti <= si, same)
        sel = si == lax.shift_left(tseg, shift) + (seg_len - 1)
    return causal, causal_t, sel


def _conv_block(u, ext_scr, w, b, c, first_valid):
    halo = V7X_SUBLANES

    @pl.when(c == 0)
    def _():
        ext_scr[0:halo, :] = jnp.zeros((halo, u.shape[1]), F32)

    ext_scr[halo:halo + ROWS, :] = u
    y = b + w[0:1] * ext_scr[pl.ds(halo - 3, ROWS), :]
    y = y + w[1:2] * ext_scr[pl.ds(halo - 2, ROWS), :]
    y = y + w[2:3] * ext_scr[pl.ds(halo - 1, ROWS), :]
    y = y + w[3:4] * u

    @pl.when(c == 0)
    def _():
        ext_scr[0:halo, :] = ext_scr[first_valid:first_valid + halo, :]

    @pl.when(c != 0)
    def _():
        ext_scr[0:halo, :] = ext_scr[ROWS:ROWS + halo, :]

    return y


def _alias_prev(in_specs, args, y_prev):
    if y_prev is None:
        return {}
    in_specs.append(pl.BlockSpec(memory_space=pl.ANY))
    args.append(y_prev)
    return {len(args) - 1: 0}


def _ssd_body(xs_ref, bm_ref, cm_ref, z_ref, dtc_ref, lac_ref, lar_ref, dx_ref, ng_ref, *rest,
              seg_len, n_chunks, first_valid, has_s0, has_prev, conv):
    rest = list(rest)
    cw = [rest.pop(0) for _ in range(6)] if conv else None
    s0_ref = rest.pop(0) if has_s0 else None
    if has_prev:
        rest.pop(0)
    y_ref, sf_ref, s_scr = rest[:3]
    nseg = ROWS // seg_len
    c = pl.program_id(2)

    @pl.when(c == 0)
    def _():
        for i in range(nseg):
            if has_s0:
                s_scr[i] = jnp.concatenate([s0_ref[i, j] for j in range(SSD_HPG)], axis=0)
            else:
                s_scr[i] = jnp.zeros((GROUP_W, SSD_STATE), F32)

    if conv:
        u = jnp.concatenate([xs_ref[...], bm_ref[...], cm_ref[...]], axis=1)
        w = jnp.concatenate([cw[0][...], cw[1][...], cw[2][...]], axis=1)
        b = jnp.concatenate([cw[3][...], cw[4][...], cw[5][...]], axis=1)
        act = _silu(_conv_block(u, rest[3], w, b, c, first_valid))
        xs_in = act[:, :GROUP_W]
        bm_in = act[:, GROUP_W:GROUP_W + SSD_STATE]
        cm_in = act[:, GROUP_W + SSD_STATE:]
    else:
        xs_in, bm_in, cm_in = xs_ref[...], bm_ref[...], cm_ref[...]

    nvalid = jnp.where(c == 0, first_valid, ROWS) if first_valid != ROWS else ROWS
    row_c = lax.broadcasted_iota(I32, (ROWS, 1), 0)
    row_r = lax.broadcasted_iota(I32, (1, ROWS), 1)
    valid = row_c < nvalid
    valid_r = row_r < nvalid
    causal, causal_t, sel = _seg_masks(seg_len)
    tril = causal.astype(BF16)
    triu = causal_t.astype(BF16)
    e_i = lax.broadcasted_iota(I32, (SSD_HPG, GROUP_W), 0)
    e_l = lax.shift_right_logical(lax.broadcasted_iota(I32, (SSD_HPG, GROUP_W), 1), 6)
    expand = (e_i == e_l).astype(BF16)
    lane_head = lax.shift_right_logical(lax.broadcasted_iota(I32, (1, GROUP_W), 1), 6)

    xs = jnp.where(valid, xs_in, 0.0)
    bm = jnp.where(valid, bm_in, 0.0).astype(BF16)
    cm = jnp.where(valid, cm_in, 0.0).astype(BF16)
    z = jnp.where(valid, z_ref[...], 0.0)
    dt = jnp.where(valid, dtc_ref[0], 0.0)
    la = jnp.where(valid, lac_ref[0], 0.0)
    la_r = jnp.where(valid_r, lar_ref[0], 0.0)

    cum = _dot_01a(tril, la)
    cum_r = _dot_a01(la_r, triu)
    if seg_len == ROWS:
        cum_last = jnp.broadcast_to(cum[ROWS - 1:ROWS, :], (ROWS, SSD_HPG))
    else:
        cum_last = _dot_01a(sel.astype(BF16), cum)
    ecum_x = _dot_a01(jnp.exp(cum), expand)
    v = xs * _dot_a01(dt, expand)
    kvw = (v * _dot_a01(jnp.exp(cum_last - cum), expand)).astype(BF16)

    scores = lax.dot_general(cm, bm, (((1,), (1,)), ((), ())), preferred_element_type=F32)
    y = jnp.zeros((ROWS, GROUP_W), F32)
    for j in range(SSD_HPG):
        dj = jnp.exp(jnp.where(causal, cum[:, j:j + 1] - cum_r[j:j + 1, :], -jnp.inf))
        pj = (scores * dj).astype(BF16)
        vj = jnp.where(lane_head == j, v, 0.0).astype(BF16)
        y = y + jnp.dot(pj, vj, preferred_element_type=F32)

    y_inter = []
    for i in range(nseg):
        r0 = i * seg_len
        r_last = r0 + seg_len - 1
        s_old = s_scr[i]
        y_inter.append(lax.dot_general(cm[r0:r0 + seg_len], s_old.astype(BF16), (((1,), (1,)), ((), ())),
                                       preferred_element_type=F32))
        cs = lax.dot_general(kvw[r0:r0 + seg_len], bm[r0:r0 + seg_len], (((0,), (0,)), ((), ())),
                             preferred_element_type=F32)
        e_last = jnp.exp(cum_r[:, r_last:r_last + 1])
        dec = jnp.concatenate([jnp.broadcast_to(e_last[j:j + 1, :], (SSD_HEAD_DIM, SSD_STATE))
                               for j in range(SSD_HPG)], axis=0)
        s_scr[i] = dec * s_old + cs
    y_inter = y_inter[0] if nseg == 1 else jnp.concatenate(y_inter, axis=0)
    y = y + y_inter * ecum_x
    y = y + dx_ref[...] * xs
    y = y * _silu(z)
    y = y * lax.rsqrt(jnp.mean(y * y, axis=-1, keepdims=True) + EPS) * ng_ref[...]
    y_ref[...] = y.astype(y_ref.dtype)

    @pl.when(c == n_chunks - 1)
    def _():
        for i in range(nseg):
            s_fin = s_scr[i]
            for j in range(SSD_HPG):
                sf_ref[i, j] = s_fin[j * SSD_HEAD_DIM:(j + 1) * SSD_HEAD_DIM, :]


def _ssd_scan(xbc_src, xbc_blk0, data_block, p1, dtc, lac, lar, dx, ng, conv_wb, s0, y_prev, *, out_width,
              nb, n_chunks, seg_len, first_valid, row_block):
    t_rows = p1.shape[0]
    nseg = ROWS // seg_len
    conv = conv_wb is not None
    d_inner = SSD_GROUPS * GROUP_W
    per128 = GROUP_W // SSD_STATE
    b_blk0 = (xbc_blk0 + SSD_GROUPS) * per128
    c_blk0 = b_blk0 + SSD_GROUPS
    in_specs = [
        pl.BlockSpec((ROWS, GROUP_W), lambda b, g, c: (data_block(b, c), xbc_blk0 + g)),
        pl.BlockSpec((ROWS, SSD_STATE), lambda b, g, c: (data_block(b, c), b_blk0 + g)),
        pl.BlockSpec((ROWS, SSD_STATE), lambda b, g, c: (data_block(b, c), c_blk0 + g)),
        pl.BlockSpec((ROWS, GROUP_W), lambda b, g, c: (row_block(b, c), g)),
        pl.BlockSpec((None, 1, ROWS, SSD_HPG), lambda b, g, c: (g, 0, row_block(b, c), 0)),
        pl.BlockSpec((None, 1, ROWS, SSD_HPG), lambda b, g, c: (g, 0, row_block(b, c), 0)),
        pl.BlockSpec((None, 1, SSD_HPG, ROWS), lambda b, g, c: (g, 0, 0, row_block(b, c))),
        pl.BlockSpec((None, 1, GROUP_W), lambda b, g, c: (g, 0, 0)),
        pl.BlockSpec((None, 1, GROUP_W), lambda b, g, c: (g, 0, 0)),
    ]
    args = [xbc_src, xbc_src, xbc_src, p1, dtc, lac, lar, dx, ng]
    scratch = [pltpu.VMEM((nseg, GROUP_W, SSD_STATE), F32)]
    if conv:
        cw, cb = conv_wb
        nbw = SSD_GROUPS * per128
        for arr, rows in ((cw, CONV_W), (cb, 1)):
            in_specs += [pl.BlockSpec((rows, GROUP_W), lambda b, g, c: (0, g)),
                         pl.BlockSpec((rows, SSD_STATE), lambda b, g, c: (0, nbw + g)),
                         pl.BlockSpec((rows, SSD_STATE), lambda b, g, c: (0, nbw + SSD_GROUPS + g))]
            args += [arr, arr, arr]
        scratch.append(pltpu.VMEM((ROWS + V7X_SUBLANES, XBC_W), F32))
    state_spec = pl.BlockSpec((nseg, SSD_HPG, SSD_HEAD_DIM, SSD_STATE), lambda b, g, c: (b, g, 0, 0))
    if s0 is not None:
        in_specs.append(state_spec)
        args.append(s0)
    aliases = _alias_prev(in_specs, args, y_prev)
    n_heads = SSD_GROUPS * SSD_HPG
    return pl.pallas_call(
        functools.partial(_ssd_body, seg_len=seg_len, n_chunks=n_chunks, first_valid=first_valid,
                          has_s0=s0 is not None, has_prev=y_prev is not None, conv=conv),
        grid=(nb, SSD_GROUPS, n_chunks),
        in_specs=in_specs,
        out_specs=[pl.BlockSpec((ROWS, GROUP_W), lambda b, g, c: (row_block(b, c), g)), state_spec],
        out_shape=[jax.ShapeDtypeStruct((t_rows, out_width), BF16),
                   jax.ShapeDtypeStruct((nb * nseg, n_heads, SSD_HEAD_DIM, SSD_STATE), F32)],
        scratch_shapes=scratch,
        input_output_aliases=aliases,
        compiler_params=_cparams(("arbitrary", "arbitrary", "arbitrary")),
        name="ssd_scan",
    )(*args)


def _ret_body(q_ref, k_ref, v_ref, g_ref, cos_ref, sin_ref, lg_ref, *rest,
              seg_len, n_chunks, first_valid, has_s0, has_prev):
    s0_ref = rest[0] if has_s0 else None
    y_ref, sf_ref, s_scr = rest[int(has_s0) + int(has_prev):]
    nseg = ROWS // seg_len
    half = RET_DIM // 2
    hb = q_ref.shape[1] // RET_DIM
    c = pl.program_id(2)

    @pl.when(c == 0)
    def _():
        for i in range(nseg):
            for hh in range(hb):
                if has_s0:
                    s_scr[i, hh] = s0_ref[i, hh]
                else:
                    s_scr[i, hh] = jnp.zeros((RET_DIM, RET_DIM), F32)

    nvalid = jnp.where(c == 0, first_valid, ROWS) if first_valid != ROWS else ROWS
    row_c = lax.broadcasted_iota(I32, (ROWS, 1), 0)
    row_r = lax.broadcasted_iota(I32, (1, ROWS), 1)
    valid = row_c < nvalid
    causal, _, _ = _seg_masks(seg_len)
    pos_c = jnp.minimum(jnp.bitwise_and(row_c, seg_len - 1) + 1, nvalid).astype(F32)
    pos_r = jnp.minimum(jnp.bitwise_and(row_r, seg_len - 1) + 1, nvalid).astype(F32)
    last = jnp.minimum(seg_len, nvalid).astype(F32) if first_valid != ROWS else float(seg_len)
    cos = cos_ref[...]
    sin = sin_ref[...]

    def rot(x):
        x1, x2 = x[:, :half], x[:, half:]
        return jnp.concatenate([x1 * cos - x2 * sin, x1 * sin + x2 * cos], axis=1)

    for hh in range(hb):
        cols = slice(hh * RET_DIM, (hh + 1) * RET_DIM)
        lg_c = lg_ref[hh][:, 0:1]
        cum = pos_c * lg_c
        cum_r = pos_r * lg_c
        cum_last = last * lg_c
        q = rot(jnp.where(valid, q_ref[:, cols], 0.0)).astype(BF16)
        k = (rot(jnp.where(valid, k_ref[:, cols], 0.0)) * (RET_DIM ** -0.5)).astype(BF16)
        v = jnp.where(valid, v_ref[:, cols], 0.0)
        gate = jnp.where(valid, g_ref[:, cols], 0.0)

        decay = jnp.exp(jnp.where(causal, cum - cum_r, -jnp.inf))
        scores = lax.dot_general(q, k, (((1,), (1,)), ((), ())), preferred_element_type=F32)
        y = jnp.dot((scores * decay).astype(BF16), v.astype(BF16), preferred_element_type=F32)
        kvw = (v * jnp.exp(cum_last - cum)).astype(BF16)
        ecum = jnp.exp(cum)
        e_last = jnp.exp(cum_last)

        y_inter = []
        for i in range(nseg):
            r0 = i * seg_len
            s_old = s_scr[i, hh]
            y_inter.append(jnp.dot(q[r0:r0 + seg_len], s_old.astype(BF16), preferred_element_type=F32))
            cs = lax.dot_general(k[r0:r0 + seg_len], kvw[r0:r0 + seg_len], (((0,), (0,)), ((), ())),
                                 preferred_element_type=F32)
            s_scr[i, hh] = e_last * s_old + cs
        y_inter = y_inter[0] if nseg == 1 else jnp.concatenate(y_inter, axis=0)
        y = y + y_inter * ecum
        y = y * lax.rsqrt(jnp.mean(y * y, axis=-1, keepdims=True) + EPS)
        y = y * _silu(gate)
        y_ref[:, cols] = y.astype(y_ref.dtype)

    @pl.when(c == n_chunks - 1)
    def _():
        for i in range(nseg):
            for hh in range(hb):
                sf_ref[i, hh] = s_scr[i, hh]


def _ret_scan(p3, cos, sin, lg, s0, y_prev, *, hb, out_width, out_col0, nb, n_chunks, seg_len, first_valid,
              row_block, cs_block):
    t_rows = p3.shape[0]
    nseg = ROWS // seg_len
    half = RET_DIM // 2
    wide = hb * RET_DIM
    hblk = RET_HEADS // hb
    out_blk0 = out_col0 // wide
    in_specs = [
        pl.BlockSpec((ROWS, wide), lambda b, h, c: (row_block(b, c), h)),
        pl.BlockSpec((ROWS, wide), lambda b, h, c: (row_block(b, c), hblk + h)),
        pl.BlockSpec((ROWS, wide), lambda b, h, c: (row_block(b, c), 2 * hblk + h)),
        pl.BlockSpec((ROWS, wide), lambda b, h, c: (row_block(b, c), 3 * hblk + h)),
        pl.BlockSpec((ROWS, half), lambda b, h, c: (cs_block(b, c), 0)),
        pl.BlockSpec((ROWS, half), lambda b, h, c: (cs_block(b, c), 0)),
        pl.BlockSpec((hb, 1, V7X_LANES), lambda b, h, c: (h, 0, 0)),
    ]
    args = [p3, p3, p3, p3, cos, sin, lg]
    state_spec = pl.BlockSpec((nseg, hb, RET_DIM, RET_DIM), lambda b, h, c: (b, h, 0, 0))
    if s0 is not None:
        in_specs.append(state_spec)
        args.append(s0)
    aliases = _alias_prev(in_specs, args, y_prev)
    return pl.pallas_call(
        functools.partial(_ret_body, seg_len=seg_len, n_chunks=n_chunks, first_valid=first_valid,
                          has_s0=s0 is not None, has_prev=y_prev is not None),
        grid=(nb, hblk, n_chunks),
        in_specs=in_specs,
        out_specs=[pl.BlockSpec((ROWS, wide), lambda b, h, c: (row_block(b, c), out_blk0 + h)),
                   state_spec],
        out_shape=[jax.ShapeDtypeStruct((t_rows, out_width), BF16),
                   jax.ShapeDtypeStruct((nb * nseg, RET_HEADS, RET_DIM, RET_DIM), F32)],
        scratch_shapes=[pltpu.VMEM((nseg, hb, RET_DIM, RET_DIM), F32)],
        input_output_aliases=aliases,
        compiler_params=_cparams(("arbitrary", "arbitrary", "arbitrary")),
        name="ret_scan",
    )(*args)


def _lru_body(x_ref, gate_ref, wa_ref, wx_ref, ba_ref, bx_ref, sp_ref, h0_ref, *rest,
              seg_len, n_chunks, first_valid, mark_pos0, has_prev, conv):
    rest = list(rest)
    cw = [rest.pop(0) for _ in range(2)] if conv else None
    if has_prev:
        rest.pop(0)
    y_ref, hl_ref, carry = rest[:3]
    c = pl.program_id(2)
    nvalid = jnp.where(c == 0, first_valid, ROWS) if first_valid != ROWS else ROWS
    row_c = lax.broadcasted_iota(I32, (ROWS, 1), 0)
    valid = row_c < nvalid

    if conv:
        x_in = _conv_block(x_ref[...], rest[3], cw[0][...], cw[1][...], c, first_valid)
    else:
        x_in = x_ref[...]
    x = jnp.where(valid, x_in, 0.0)
    xb = x.astype(BF16)
    r = _sigmoid(jnp.dot(xb, wa_ref[...], preferred_element_type=F32) + ba_ref[...])
    ig = _sigmoid(jnp.dot(xb, wx_ref[...], preferred_element_type=F32) + bx_ref[...])
    log_a = -LRU_C * r * sp_ref[...]
    a = jnp.exp(log_a)
    mult = jnp.sqrt(-jnp.tanh(log_a) * (a * a + 1.0))
    if mark_pos0:
        mult = jnp.where(jnp.logical_and(c == 0, row_c == 0), 1.0, mult)
    bterm = mult * ig * x
    a = jnp.where(valid, a, 1.0)
    bterm = jnp.where(valid, bterm, 0.0)

    sub = lax.broadcasted_iota(I32, (V7X_SUBLANES, PAIR_W), 0)
    if seg_len == ROWS:
        @pl.when(c == 0)
        def _():
            carry[...] = h0_ref[...]
        h_prev = carry[...]
    tiles = []
    for t in range(ROWS // V7X_SUBLANES):
        at = a[8 * t:8 * t + 8]
        bt = bterm[8 * t:8 * t + 8]
        for d in (1, 2, 4):
            a_sh = jnp.where(sub >= d, pltpu.roll(at, d, 0), 1.0)
            b_sh = jnp.where(sub >= d, pltpu.roll(bt, d, 0), 0.0)
            bt = at * b_sh + bt
            at = at * a_sh
        if seg_len != ROWS:
            h_prev = h0_ref[t:t + 1, :]
        ht = bt + at * h_prev
        h_prev = ht[7:8, :]
        if seg_len != ROWS:
            hl_ref[t:t + 1, :] = h_prev
        tiles.append(ht)
    h = jnp.concatenate(tiles, axis=0)
    if seg_len == ROWS:
        carry[...] = h_prev

        @pl.when(c == n_chunks - 1)
        def _():
            hl_ref[...] = h_prev

    g = gate_ref[...]
    gelu = 0.5 * g * (1.0 + jnp.tanh(math.sqrt(2.0 / math.pi) * (g + 0.044715 * (g * g * g))))
    y_ref[...] = (h * gelu).astype(y_ref.dtype)


def _lru_scan(x_src, x_blk0, data_block, p_lru, wa_p, wx_p, ba_p, bx_p, sp_p, conv_wb, h0, y_prev, *,
              nb, n_chunks, seg_len, first_valid, mark_pos0, row_block):
    t_rows = p_lru.shape[0]
    width = p_lru.shape[1] // 2
    n_pairs = width // PAIR_W
    nseg = ROWS // seg_len
    conv = conv_wb is not None
    hrows = 1 if seg_len == ROWS else nseg
    h_spec = pl.BlockSpec((None, hrows, PAIR_W), lambda b, p, c: (b, 0, p))
    vec_spec = pl.BlockSpec((None, 1, PAIR_W), lambda b, p, c: (p, 0, 0))
    in_specs = [
        pl.BlockSpec((ROWS, PAIR_W), lambda b, p, c: (data_block(b, c), x_blk0 + p)),
        pl.BlockSpec((ROWS, PAIR_W), lambda b, p, c: (row_block(b, c), p)),
        pl.BlockSpec((None, PAIR_W, PAIR_W), lambda b, p, c: (p, 0, 0)),
        pl.BlockSpec((None, PAIR_W, PAIR_W), lambda b, p, c: (p, 0, 0)),
        vec_spec, vec_spec, vec_spec, h_spec,
    ]
    args = [x_src, p_lru, wa_p, wx_p, ba_p, bx_p, sp_p, h0]
    scratch = [pltpu.VMEM((1, PAIR_W), F32)]
    if conv:
        cw, cb = conv_wb
        in_specs += [pl.BlockSpec((CONV_W, PAIR_W), lambda b, p, c: (0, p)),
                     pl.BlockSpec((1, PAIR_W), lambda b, p, c: (0, p))]
        args += [cw, cb]
        scratch.append(pltpu.VMEM((ROWS + V7X_SUBLANES, PAIR_W), F32))
    aliases = _alias_prev(in_specs, args, y_prev)
    return pl.pallas_call(
        functools.partial(_lru_body, seg_len=seg_len, n_chunks=n_chunks, first_valid=first_valid,
                          mark_pos0=mark_pos0, has_prev=y_prev is not None, conv=conv),
        grid=(nb, n_pairs, n_chunks),
        in_specs=in_specs,
        out_specs=[pl.BlockSpec((ROWS, PAIR_W), lambda b, p, c: (row_block(b, c), p)), h_spec],
        out_shape=[jax.ShapeDtypeStruct((t_rows, width), BF16),
                   jax.ShapeDtypeStruct((nb, hrows, width), F32)],
        scratch_shapes=scratch,
        input_output_aliases=aliases,
        compiler_params=_cparams(("arbitrary", "arbitrary", "arbitrary")),
        name="rglru",
    )(*args)


def _router_body(h_ref, g_ref, wr_ref, br_ref, hn_ref, route_ref, cnt_ref, carry, *, tm):
    i = pl.program_id(0)

    @pl.when(i == 0)
    def _():
        carry[...] = jnp.zeros_like(carry)

    x = h_ref[...]
    hn = x * lax.rsqrt(jnp.mean(x * x, axis=-1, keepdims=True) + EPS) * g_ref[...]
    hn_ref[...] = hn

    h_hi = hn.astype(BF16)
    h_lo = (hn - h_hi.astype(F32)).astype(BF16)
    w = wr_ref[...]
    w_hi = w.astype(BF16)
    w_lo = (w - w_hi.astype(F32)).astype(BF16)
    logits = (jnp.dot(h_hi, w_hi, preferred_element_type=F32)
              + jnp.dot(h_hi, w_lo, preferred_element_type=F32)
              + jnp.dot(h_lo, w_hi, preferred_element_type=F32)) + br_ref[...]

    lane_i = lax.broadcasted_iota(I32, (tm, V7X_LANES), 1)
    lane = lane_i.astype(F32)
    big = float(4 * V7X_LANES)
    neg = -jnp.inf
    is_g = jnp.logical_and(lane_i >= N_EXPERTS, lane_i < N_EXPERTS + MOE_GROUPS)
    glog = jnp.where(is_g, logits, neg)
    gmax = jnp.max(glog, axis=-1, keepdims=True)
    gsel = jnp.min(jnp.where(glog == gmax, lane, big), axis=-1, keepdims=True) - float(N_EXPERTS)
    gsum = jnp.sum(jnp.exp(glog - gmax), axis=-1, keepdims=True)
    pg = 1.0 / gsum
    lo = gsel * float(MOE_PER_GROUP)
    in_grp = jnp.logical_and(lane >= lo, lane < lo + MOE_PER_GROUP)
    elog = jnp.where(in_grp, logits, neg)
    emax = jnp.max(elog, axis=-1, keepdims=True)
    eexp = jnp.exp(elog - emax)
    ep = eexp / jnp.sum(eexp, axis=-1, keepdims=True)
    ep = jnp.where(in_grp, ep, -1.0)
    v1 = jnp.max(ep, axis=-1, keepdims=True)
    i1 = jnp.min(jnp.where(ep == v1, lane, big), axis=-1, keepdims=True)
    ep2 = jnp.where(lane == i1, -1.0, ep)
    v2 = jnp.max(ep2, axis=-1, keepdims=True)
    i2 = jnp.min(jnp.where(ep2 == v2, lane, big), axis=-1, keepdims=True)
    vs = v1 + v2
    w1 = v1 / vs * pg
    w2 = v2 / vs * pg

    oh1 = lane == i1
    oh2 = lane == i2
    oh = jnp.logical_or(oh1, oh2).astype(BF16)
    ti = lax.broadcasted_iota(I32, (tm, tm), 0)
    si = lax.broadcasted_iota(I32, (tm, tm), 1)
    before = (si < ti).astype(BF16)
    tot = jnp.dot(before, oh, preferred_element_type=F32) + carry[...]
    rank1 = jnp.sum(jnp.where(oh1, tot, 0.0), axis=-1, keepdims=True)
    rank2 = jnp.sum(jnp.where(oh2, tot, 0.0), axis=-1, keepdims=True)
    carry[...] = carry[...] + jnp.sum(oh.astype(F32), axis=0, keepdims=True)
    cnt_ref[...] = carry[...]

    out = jnp.where(lane_i == 0, i1, 0.0)
    out = jnp.where(lane_i == 1, i2, out)
    out = jnp.where(lane_i == 2, rank1, out)
    out = jnp.where(lane_i == 3, rank2, out)
    out = jnp.where(lane_i == 4, w1, out)
    out = jnp.where(lane_i == 5, w2, out)
    route_ref[...] = out


def _router(h, g, wr, br, tm):
    t_rows, d = h.shape
    return pl.pallas_call(
        functools.partial(_router_body, tm=tm),
        grid=(t_rows // tm,),
        in_specs=[pl.BlockSpec((tm, d), lambda i: (i, 0)),
                  pl.BlockSpec((1, d), lambda i: (0, 0)),
                  pl.BlockSpec((d, V7X_LANES), lambda i: (0, 0)),
                  pl.BlockSpec((1, V7X_LANES), lambda i: (0, 0))],
        out_specs=[pl.BlockSpec((tm, d), lambda i: (i, 0)),
                   pl.BlockSpec((tm, V7X_LANES), lambda i: (i, 0)),
                   pl.BlockSpec((1, V7X_LANES), lambda i: (0, 0))],
        out_shape=[jax.ShapeDtypeStruct((t_rows, d), F32),
                   jax.ShapeDtypeStruct((t_rows, V7X_LANES), F32),
                   jax.ShapeDtypeStruct((1, V7X_LANES), F32)],
        scratch_shapes=[pltpu.VMEM((1, V7X_LANES), F32)],
        compiler_params=_cparams(("arbitrary",)),
        name="moe_router",
    )(h, g.reshape(1, d), wr, br)


def _expert_body(te_ref, tfirst_ref, ngrp_ref, nt_ref, dst_cur, dst_nxt, hn_hbm, w1_hbm, w3_hbm, w2_hbm,
                 o_hbm, xbuf, ybuf, wb1, wb3, wb2, st13, st2, gsem, ssem, wsem13, wsem2,
                 *, tm, layer, nt_max, t_rows):
    i = pl.program_id(0)
    nt = nt_ref[0]
    slot = lax.rem(i, 2)

    def token_of(d):
        return jnp.where(d >= 2 * t_rows, 0, jnp.where(d >= t_rows, d - t_rows, d))

    def start_gather(dst_ref, s, ngrp):
        def body(r8, carry_):
            for u in range(ROW_UNROLL):
                r = r8 * ROW_UNROLL + u
                pltpu.make_async_copy(hn_hbm.at[pl.ds(token_of(dst_ref[0, 0, r]), 1)],
                                      xbuf.at[s, pl.ds(r, 1)], gsem.at[s]).start()
            return carry_
        lax.fori_loop(0, ngrp, body, 0)

    def wait_gather(s, ngrp):
        def body(r8, carry_):
            pltpu.make_async_copy(hn_hbm.at[pl.ds(0, ROW_UNROLL)], xbuf.at[s, pl.ds(0, ROW_UNROLL)],
                                  gsem.at[s]).wait()
            return carry_
        lax.fori_loop(0, ngrp, body, 0)

    def start_scatter(ngrp):
        def body(r8, carry_):
            for u in range(ROW_UNROLL):
                r = r8 * ROW_UNROLL + u
                pltpu.make_async_copy(ybuf.at[pl.ds(r, 1)], o_hbm.at[pl.ds(dst_cur[0, 0, r], 1)],
                                      ssem.at[0]).start(priority=1)
            return carry_
        lax.fori_loop(0, ngrp, body, 0)

    def wait_scatter(ngrp):
        def body(r8, carry_):
            pltpu.make_async_copy(ybuf.at[pl.ds(0, ROW_UNROLL)], o_hbm.at[pl.ds(0, ROW_UNROLL)],
                                  ssem.at[0]).wait()
            return carry_
        lax.fori_loop(0, ngrp, body, 0)

    @pl.when(i == 0)
    def _():
        xbuf[...] = jnp.zeros_like(xbuf)
        start_gather(dst_cur, 0, ngrp_ref[0])

    @pl.when(i + 1 < nt)
    def _():
        start_gather(dst_nxt, 1 - slot, ngrp_ref[jnp.minimum(i + 1, nt_max - 1)])

    @pl.when(jnp.logical_and(i < nt, tfirst_ref[i] == 1))
    def _():
        e = te_ref[i]
        n13 = 2 * (wb1.shape[0] // W13_ROWS)
        n2 = wb2.shape[0] // W2_ROWS

        def c13(k):
            src = w1_hbm if k % 2 == 0 else w3_hbm
            r0 = (k // 2) * W13_ROWS
            return pltpu.make_async_copy(src.at[layer, e, pl.ds(r0, W13_ROWS)], st13.at[k % W_RING],
                                         wsem13.at[k % W_RING])

        def c2(k):
            return pltpu.make_async_copy(w2_hbm.at[layer, e, pl.ds(k * W2_ROWS, W2_ROWS)],
                                         st2.at[k % W_RING], wsem2.at[k % W_RING])

        for k in range(W_RING):
            c13(k).start(priority=1)
        for k in range(W_RING):
            c2(k).start(priority=1)
        for k in range(n13):
            c13(k).wait()
            dstw = wb1 if k % 2 == 0 else wb3
            r0 = (k // 2) * W13_ROWS
            dstw[r0:r0 + W13_ROWS, :] = st13[k % W_RING].astype(BF16)
            if k + W_RING < n13:
                c13(k + W_RING).start(priority=1)
        for k in range(n2):
            c2(k).wait()
            wb2[k * W2_ROWS:(k + 1) * W2_ROWS, :] = st2[k % W_RING].astype(BF16)
            if k + W_RING < n2:
                c2(k + W_RING).start(priority=1)

    @pl.when(i < nt)
    def _():
        wait_gather(slot, ngrp_ref[i])
        x = xbuf[slot].astype(BF16)
        a = jnp.dot(x, wb1[...], preferred_element_type=F32)
        b = jnp.dot(x, wb3[...], preferred_element_type=F32)
        hdn = (_silu(a) * b).astype(BF16)

        @pl.when(i > 0)
        def _():
            wait_scatter(ngrp_ref[jnp.maximum(i - 1, 0)])

        ybuf[...] = jnp.dot(hdn, wb2[...], preferred_element_type=F32)
        start_scatter(ngrp_ref[i])

    @pl.when(i == nt)
    def _():
        wait_scatter(ngrp_ref[jnp.maximum(i - 1, 0)])

    @pl.when(jnp.logical_and(i == nt_max - 1, i < nt))
    def _():
        wait_scatter(ngrp_ref[i])


def _experts(hn, dst, tile_expert, tile_first, tile_ngrp, n_tiles, w1, w3, w2, *, layer, tm, nt_max):
    t_rows, d = hn.shape
    ff = w1.shape[3]
    smem_spec = lambda f: pl.BlockSpec((1, 1, tm), f, memory_space=pltpu.SMEM)
    grid_spec = pltpu.PrefetchScalarGridSpec(
        num_scalar_prefetch=4,
        grid=(nt_max,),
        in_specs=[
            smem_spec(lambda i, *_: (i, 0, 0)),
            smem_spec(lambda i, *_: (jnp.minimum(i + 1, nt_max - 1), 0, 0)),
            pl.BlockSpec(memory_space=pl.ANY),
            pl.BlockSpec(memory_space=pl.ANY),
            pl.BlockSpec(memory_space=pl.ANY),
            pl.BlockSpec(memory_space=pl.ANY),
        ],
        out_specs=pl.BlockSpec(memory_space=pl.ANY),
        scratch_shapes=[
            pltpu.VMEM((2, tm, d), F32), pltpu.VMEM((tm, d), F32),
            pltpu.VMEM((d, ff), BF16), pltpu.VMEM((d, ff), BF16), pltpu.VMEM((ff, d), BF16),
            pltpu.VMEM((W_RING, W13_ROWS, ff), F32), pltpu.VMEM((W_RING, W2_ROWS, d), F32),
            pltpu.SemaphoreType.DMA((2,)), pltpu.SemaphoreType.DMA((1,)),
            pltpu.SemaphoreType.DMA((W_RING,)), pltpu.SemaphoreType.DMA((W_RING,)),
        ],
    )
    return pl.pallas_call(
        functools.partial(_expert_body, tm=tm, layer=layer, nt_max=nt_max, t_rows=t_rows),
        grid_spec=grid_spec,
        out_shape=jax.ShapeDtypeStruct((2 * t_rows + tm, d), F32),
        compiler_params=_cparams(("arbitrary",)),
        name="moe_experts",
    )(tile_expert, tile_first, tile_ngrp, n_tiles, dst, dst, hn, w1, w3, w2)


def _combine_body(h_ref, y0_ref, y1_ref, route_ref, *rest, with_norm):
    route = route_ref[...]
    h_new = h_ref[...] + route[:, 4:5] * y0_ref[...] + route[:, 5:6] * y1_ref[...]
    if with_norm:
        g_ref, o_ref, n_ref = rest
        o_ref[...] = h_new
        y = h_new * lax.rsqrt(jnp.mean(h_new * h_new, axis=-1, keepdims=True) + EPS)
        n_ref[...] = (y * g_ref[...]).astype(n_ref.dtype)
    else:
        rest[0][...] = h_new


def _combine(h, y2, route, g_next, *, tm):
    t_rows, d = h.shape
    n = t_rows // tm
    row = pl.BlockSpec((tm, d), lambda i: (i, 0))
    in_specs = [row, row,
                pl.BlockSpec((tm, d), lambda i: (n + i, 0)),
                pl.BlockSpec((tm, V7X_LANES), lambda i: (i, 0))]
    args = [h, y2, y2, route]
    with_norm = g_next is not None
    if with_norm:
        in_specs.append(pl.BlockSpec((1, d), lambda i: (0, 0)))
        args.append(g_next.reshape(1, d))
        out_specs = [row, row]
        out_shape = [jax.ShapeDtypeStruct((t_rows, d), F32), jax.ShapeDtypeStruct((t_rows, d), BF16)]
    else:
        out_specs = row
        out_shape = jax.ShapeDtypeStruct((t_rows, d), F32)
    return pl.pallas_call(
        functools.partial(_combine_body, with_norm=with_norm),
        grid=(t_rows // tm,),
        in_specs=in_specs,
        out_specs=out_specs,
        out_shape=out_shape,
        compiler_params=_cparams(("arbitrary",)),
        name="moe_combine",
    )(*args)


def _moe(h, layer, g, w_rg, b_rg, w_re, b_re, w1, w3, w2, g_next, *, tm_tok, tm_e):
    t_rows, d = h.shape
    pad = V7X_LANES - N_EXPERTS - MOE_GROUPS
    wr = jnp.concatenate([w_re[layer], w_rg[layer], jnp.zeros((d, pad), F32)], axis=1)
    br = jnp.concatenate([b_re[layer], b_rg[layer], jnp.zeros((pad,), F32)]).reshape(1, V7X_LANES)
    hn, route, counts = _router(h, g, wr, br, tm_tok)

    nt_max = (2 * t_rows) // tm_e + N_EXPERTS
    e1 = route[:, 0].astype(I32)
    e2 = route[:, 1].astype(I32)
    r1 = route[:, 2].astype(I32)
    r2 = route[:, 3].astype(I32)
    cnt = counts[0, :N_EXPERTS].astype(I32)
    nt_e = (cnt + tm_e - 1) // tm_e
    tile_end = jnp.cumsum(nt_e)
    tile_start = tile_end - nt_e
    n_tiles = tile_end[-1]
    tile_ids = jnp.arange(nt_max, dtype=I32)
    te = jnp.minimum(jnp.sum((tile_ids[:, None] >= tile_end[None, :]).astype(I32), axis=1), N_EXPERTS - 1)
    live = tile_ids < n_tiles
    k_in_e = tile_ids - tile_start[te]
    tfirst = jnp.where(jnp.logical_and(live, k_in_e == 0), 1, 0).astype(I32)
    nval = jnp.where(live, jnp.clip(cnt[te] - k_in_e * tm_e, 0, tm_e), 0)
    ngrp = ((nval + ROW_UNROLL - 1) // ROW_UNROLL).astype(I32)
    row_off = tile_start * tm_e
    d1 = row_off[e1] + r1
    d2 = row_off[e2] + r2
    tok = jnp.arange(t_rows, dtype=I32)
    dump = 2 * t_rows + jnp.arange(nt_max * tm_e, dtype=I32) % tm_e
    dst = dump.at[d1].set(tok).at[d2].set(t_rows + tok)

    y2 = _experts(hn, dst.reshape(nt_max, 1, tm_e), te, tfirst, ngrp, n_tiles.reshape(1), w1, w3, w2,
                  layer=layer, tm=tm_e, nt_max=nt_max)
    return _combine(h, y2, route, g_next, tm=_pick(t_rows, (256, 128)))


def _conv_sample(u, buf, w, b):
    full = jnp.concatenate([buf, u], axis=1)
    l = u.shape[1]
    y = b + w[0] * full[:, 0:l]
    for t in range(1, CONV_W):
        y = y + w[t] * full[:, t:t + l]
    return y, full[:, -(CONV_W - 1):]


def _pair_blocks(w):
    nblk, bw, _ = w.shape
    w = w.reshape(nblk // 2, 2, bw, bw)
    z = jnp.zeros((nblk // 2, bw, bw), w.dtype)
    top = jnp.concatenate([w[:, 0], z], axis=2)
    bot = jnp.concatenate([z, w[:, 1]], axis=2)
    return jnp.concatenate([top, bot], axis=1).astype(BF16)


def kernel(x_prompt, x_sample, state_ssd_conv, state_ssd, state_ret, state_lru_conv, state_lru, meta_tokens, norm_mix, norm_ffn, norm_final, w_in0, ssd_conv_w, ssd_conv_b, ssd_dt_bias, ssd_a_log, ssd_d, ssd_norm_g, w_out0, w_in1, lru_conv_w, lru_conv_b, lru_wa, lru_ba, lru_wx, lru_bx, lru_lambda, w_out1, moe_w_rg, moe_b_rg, moe_w_re, moe_b_re, moe_w1, moe_w3, moe_w2):
    nb, seq, d = x_prompt.shape
    nbs, ls, _ = x_sample.shape
    assert seq % ROWS == 0 and ROWS % ls == 0 and (nbs * ls) % ROWS == 0 and ls >= CONV_W - 1
    n_xblk = seq // ROWS
    blk_p = n_xblk + 1
    lp_pad = blk_p * ROWS
    tp = nb * lp_pad
    ts = nbs * ls
    t_all = tp + ts
    nbs_blk = ts // ROWS

    def rb_prompt(b, c):
        return b * blk_p + lax.rem(c + n_xblk, blk_p)

    def rb_sample(b, c):
        return nb * blk_p + b

    def rb_own(b, c):
        return b

    def _last_prompt_rows(p, c0, c1):
        return jnp.stack([p[b * lp_pad + seq - (CONV_W - 1):b * lp_pad + seq, c0:c1] for b in range(nb)])

    prompt_kw = dict(nb=nb, n_chunks=blk_p, seg_len=ROWS, first_valid=N_META, row_block=rb_prompt)
    sample_kw = dict(nb=nbs_blk, n_chunks=1, seg_len=ls, first_valid=ROWS, row_block=rb_sample)

    h, hn = _embed_norm(x_prompt, x_sample, meta_tokens, norm_mix[0])

    tm_tok = _pick(t_all, (512, 256, 128))
    tm_mm = _pick(t_all, (1216, 608, 512, 256, 128))

    d_inner = SSD_GROUPS * GROUP_W
    conv_dim = d_inner + 2 * SSD_GROUPS * SSD_STATE
    n_heads = SSD_GROUPS * SSD_HPG
    c_zx = d_inner + conv_dim
    c_qkvg = w_in0.shape[1] - c_zx - n_heads
    w_in0_t = w_in0.T
    p1 = _matmul_t(hn, w_in0_t, 0, c_zx, tm_mm, 512)
    pdt = _matmul_t(hn, w_in0_t, c_zx, n_heads, tm_mm, n_heads)
    p3 = _matmul_t(hn, w_in0_t, c_zx + n_heads, c_qkvg, tm_mm, 512)

    dt = jax.nn.softplus(pdt + ssd_dt_bias)
    la = dt * (-jnp.exp(ssd_a_log))
    dt3 = dt.reshape(t_all, SSD_GROUPS, SSD_HPG)
    la3 = la.reshape(t_all, SSD_GROUPS, SSD_HPG)
    dtc = dt3.transpose(1, 0, 2)[:, None]
    lac = la3.transpose(1, 0, 2)[:, None]
    lar = la3.transpose(1, 2, 0)[:, None]
    dx = jnp.repeat(ssd_d, SSD_HEAD_DIM).reshape(SSD_GROUPS, 1, GROUP_W)
    ng = ssd_norm_g.reshape(SSD_GROUPS, 1, GROUP_W)

    xbc_s, conv_s = _conv_sample(p1[tp:, d_inner:].reshape(nbs, ls, conv_dim), state_ssd_conv,
                                 ssd_conv_w, ssd_conv_b)
    xbc_s = jax.nn.silu(xbc_s).reshape(ts, conv_dim)
    conv_p = _last_prompt_rows(p1, d_inner, c_zx)

    mix_w = d_inner + RET_HEADS * RET_DIM
    ssd_common = (p1, dtc, lac, lar, dx, ng)
    mix, ssd_p = _ssd_scan(p1, d_inner // GROUP_W, rb_prompt, *ssd_common,
                           (ssd_conv_w, ssd_conv_b.reshape(1, conv_dim)), None, None,
                           out_width=mix_w, **prompt_kw)
    mix, ssd_s = _ssd_scan(xbc_s, 0, rb_own, *ssd_common, None, jnp.swapaxes(state_ssd, 2, 3), mix,
                           out_width=mix_w, **sample_kw)

    half = RET_DIM // 2
    inv = 1.0 / (ROPE_BASE ** (jnp.arange(half, dtype=F32) / half))
    pos_p = jnp.concatenate([N_META + jnp.arange(seq, dtype=I32), jnp.arange(N_META, dtype=I32),
                             jnp.zeros((lp_pad - seq - N_META,), I32)])
    pos_s = PAST_LEN + (jnp.arange(ROWS, dtype=I32) % ls)
    ang = jnp.concatenate([pos_p, pos_s]).astype(F32)[:, None] * inv[None, :]
    cos, sin = jnp.cos(ang), jnp.sin(ang)
    log_gamma = jnp.log1p(-jnp.exp2(-5.0 - jnp.arange(RET_HEADS, dtype=F32)))
    lg = jnp.broadcast_to(log_gamma[:, None, None], (RET_HEADS, 1, V7X_LANES))
    ret_kw = dict(out_width=mix_w, out_col0=d_inner)
    mix, ret_p = _ret_scan(p3, cos, sin, lg, None, mix, hb=RET_HB_PROMPT,
                           cs_block=lambda b, c: lax.rem(c + n_xblk, blk_p), **ret_kw, **prompt_kw)
    mix, ret_s = _ret_scan(p3, cos, sin, lg, state_ret, mix, hb=RET_HB_SAMPLE, cs_block=lambda b, c: blk_p,
                           **ret_kw, **sample_kw)
    tm_o = _pick(t_all, (608, 512, 256, 128))
    h = _matmul(mix, w_out0.astype(BF16), d, tm_o, 512, res=h)
    moe_kw = dict(tm_tok=tm_tok, tm_e=min(MOE_TM, tm_tok))
    moe_w = (moe_w_rg, moe_b_rg, moe_w_re, moe_b_re, moe_w1, moe_w3, moe_w2)
    h, hn = _moe(h, 0, norm_ffn[0], *moe_w, norm_mix[1], **moe_kw)

    width = lru_lambda.shape[0]
    p_lru = _matmul(hn, w_in1, 2 * width, tm_mm, 512)
    xc_s, lconv_s = _conv_sample(p_lru[tp:, width:].reshape(nbs, ls, width), state_lru_conv,
                                 lru_conv_w, lru_conv_b)
    xc_s = xc_s.reshape(ts, width)
    lconv_p = _last_prompt_rows(p_lru, width, 2 * width)
    n_pairs = width // PAIR_W
    lru_common = (p_lru, _pair_blocks(lru_wa), _pair_blocks(lru_wx), lru_ba.reshape(n_pairs, 1, PAIR_W),
                  lru_bx.reshape(n_pairs, 1, PAIR_W),
                  jax.nn.softplus(-lru_lambda).reshape(n_pairs, 1, PAIR_W))
    y_lru, lru_p = _lru_scan(p_lru, n_pairs, rb_prompt, *lru_common,
                             (lru_conv_w, lru_conv_b.reshape(1, width)), jnp.zeros((nb, 1, width), F32),
                             None, mark_pos0=True, **prompt_kw)
    y_lru, lru_s = _lru_scan(xc_s, 0, rb_own, *lru_common, None,
                             state_lru.reshape(nbs_blk, ROWS // ls, width), y_lru, mark_pos0=False,
                             **sample_kw)
    h = _matmul(y_lru, w_out1, d, tm_o, 512, res=h)
    h = _moe(h, 1, norm_ffn[1], *moe_w, None, **moe_kw)

    y_prompt = _rmsnorm(h, norm_final, F32, ROWS, n_out_blocks=nb * n_xblk,
                        in_block=lambda i: (i // n_xblk) * blk_p + lax.rem(i, n_xblk))
    y_sample = _rmsnorm(h, norm_final, F32, ROWS, n_out_blocks=nbs_blk, in_block=lambda i: nb * blk_p + i)
    return (y_prompt.reshape(nb, seq, d), y_sample.reshape(nbs, ls, d),
            conv_p, jnp.swapaxes(ssd_p, 2, 3), ret_p, lconv_p, lru_p.reshape(nb, width),
            conv_s, jnp.swapaxes(ssd_s, 2, 3), ret_s, lconv_s, lru_s.reshape(nbs, width))
```

```python
import functools
import math

import jax
import jax.numpy as jnp
from jax import lax
from jax.experimental import pallas as pl
from jax.experimental.pallas import tpu as pltpu

F32, BF16, I32 = jnp.float32, jnp.bfloat16, jnp.int32

N_META = 16
CONV_W = 4
EPS = 1e-6
F32_TINY = 1.1754944e-38
PAST_LEN = 16384
SSD_HEAD_DIM = 64
SSD_GROUPS = 8
SSD_HPG = 8
SSD_STATE = 128
RET_HEADS = 16
RET_DIM = 256
RET_HB_PROMPT = 4
RET_HB_SAMPLE = 2
ROPE_BASE = 10000.0
LRU_C = 8.0
MOE_GROUPS = 4
MOE_PER_GROUP = 8
N_EXPERTS = MOE_GROUPS * MOE_PER_GROUP

V7X_LANES = 128
V7X_SUBLANES = 8
V7X_VMEM_LIMIT = 56 * 1024 * 1024
ROWS = 128
GROUP_W = SSD_HPG * SSD_HEAD_DIM
XBC_W = GROUP_W + 2 * SSD_STATE
PAIR_W = 2 * 320
MOE_TM = 512
W_RING = 4
W13_ROWS = 512
W2_ROWS = 64
ROW_UNROLL = 8


def _cparams(sem):
    return pltpu.CompilerParams(dimension_semantics=sem, vmem_limit_bytes=V7X_VMEM_LIMIT)


def _pick(n, cands):
    for c in cands:
        if n % c == 0:
            return c
    raise ValueError(f"no tile for {n} in {cands}")


def _sigmoid(x):
    return 0.5 * jnp.tanh(0.5 * x) + 0.5


def _silu(x):
    return x * _sigmoid(x)


def _rmsnorm_body(x_ref, g_ref, o_ref):
    x = x_ref[...]
    y = x * lax.rsqrt(jnp.mean(x * x, axis=-1, keepdims=True) + EPS)
    o_ref[...] = (y * g_ref[...]).astype(o_ref.dtype)


def _rmsnorm(x, g, out_dtype, tm, n_out_blocks=None, in_block=None):
    m, d = x.shape
    n_blocks = m // tm if n_out_blocks is None else n_out_blocks
    in_map = (lambda i: (i, 0)) if in_block is None else (lambda i: (in_block(i), 0))
    return pl.pallas_call(
        _rmsnorm_body,
        grid=(n_blocks,),
        in_specs=[pl.BlockSpec((tm, d), in_map), pl.BlockSpec((1, d), lambda i: (0, 0))],
        out_specs=pl.BlockSpec((tm, d), lambda i: (i, 0)),
        out_shape=jax.ShapeDtypeStruct((n_blocks * tm, d), out_dtype),
        compiler_params=_cparams(("arbitrary",)),
        name="rmsnorm",
    )(x, g.reshape(1, d))


def _embed_body(xp_ref, xs_ref, meta_ref, g_ref, h_ref, hn_ref, *, nb, blk_p, n_xblk):
    i = pl.program_id(0)
    k = lax.rem(i, blk_p)
    is_prompt = i < nb * blk_p

    def emit(x):
        h_ref[...] = x
        y = x * lax.rsqrt(jnp.mean(x * x, axis=-1, keepdims=True) + EPS)
        hn_ref[...] = (y * g_ref[...]).astype(hn_ref.dtype)

    @pl.when(jnp.logical_and(is_prompt, k < n_xblk))
    def _():
        emit(xp_ref[...])

    @pl.when(jnp.logical_and(is_prompt, k == n_xblk))
    def _():
        meta = meta_ref[...]
        emit(jnp.concatenate([meta, jnp.zeros((ROWS - meta.shape[0], meta.shape[1]), F32)], axis=0))

    @pl.when(jnp.logical_not(is_prompt))
    def _():
        emit(xs_ref[...])


def _embed_norm(x_prompt, x_sample, meta, g):
    nb, seq, d = x_prompt.shape
    ts = x_sample.shape[0] * x_sample.shape[1]
    n_xblk = seq // ROWS
    blk_p = n_xblk + 1
    n_blocks = nb * blk_p + ts // ROWS
    row = pl.BlockSpec((ROWS, d), lambda i: (i, 0))
    return pl.pallas_call(
        functools.partial(_embed_body, nb=nb, blk_p=blk_p, n_xblk=n_xblk),
        grid=(n_blocks,),
        in_specs=[
            pl.BlockSpec((ROWS, d), lambda i: (jnp.minimum(i // blk_p, nb - 1) * n_xblk
                                               + jnp.minimum(lax.rem(i, blk_p), n_xblk - 1), 0)),
            pl.BlockSpec((ROWS, d), lambda i: (jnp.clip(i - nb * blk_p, 0, ts // ROWS - 1), 0)),
            pl.BlockSpec(meta.shape, lambda i: (0, 0)),
            pl.BlockSpec((1, d), lambda i: (0, 0)),
        ],
        out_specs=[row, row],
        out_shape=[jax.ShapeDtypeStruct((n_blocks * ROWS, d), F32),
                   jax.ShapeDtypeStruct((n_blocks * ROWS, d), BF16)],
        compiler_params=_cparams(("arbitrary",)),
        name="embed_norm",
    )(x_prompt.reshape(nb * seq, d), x_sample.reshape(ts, d), meta, g.reshape(1, d))


def _mm_body(*refs, has_res, cast_b):
    a_ref, b_ref = refs[0], refs[1]
    r_ref = refs[2] if has_res else None
    o_ref = refs[3] if has_res else refs[2]
    if cast_b:
        bs_ref = refs[-1]

        @pl.when(pl.program_id(1) == 0)
        def _():
            bs_ref[...] = b_ref[...].astype(BF16)

        b = bs_ref[...]
    else:
        b = b_ref[...]
    acc = jnp.dot(a_ref[...], b, preferred_element_type=F32)
    if has_res:
        acc = acc + r_ref[...]
    o_ref[...] = acc


def _matmul(a, b, n_cols, tm, tn, res=None):
    m, k = a.shape
    cast_b = b.dtype != BF16
    in_specs = [pl.BlockSpec((tm, k), lambda j, i: (i, 0)),
                pl.BlockSpec((k, tn), lambda j, i: (0, j))]
    args = [a, b]
    if res is not None:
        in_specs.append(pl.BlockSpec((tm, tn), lambda j, i: (i, j)))
        args.append(res)
    return pl.pallas_call(
        functools.partial(_mm_body, has_res=res is not None, cast_b=cast_b),
        grid=(n_cols // tn, m // tm),
        in_specs=in_specs,
        out_specs=pl.BlockSpec((tm, tn), lambda j, i: (i, j)),
        out_shape=jax.ShapeDtypeStruct((m, n_cols), F32),
        scratch_shapes=[pltpu.VMEM((k, tn), BF16)] if cast_b else [],
        compiler_params=_cparams(("arbitrary", "arbitrary")),
        name="proj",
    )(*args)


def _mm_t_body(a_ref, bt_ref, o_ref, bs_ref):
    @pl.when(pl.program_id(1) == 0)
    def _():
        bs_ref[...] = bt_ref[...].astype(BF16)

    o_ref[...] = lax.dot_general(a_ref[...], bs_ref[...], (((1,), (1,)), ((), ())),
                                 preferred_element_type=F32)


def _matmul_t(a, bt, row0, n_cols, tm, tn):
    m, k = a.shape
    return pl.pallas_call(
        _mm_t_body,
        grid=(n_cols // tn, m // tm),
        in_specs=[pl.BlockSpec((tm, k), lambda j, i: (i, 0)),
                  pl.BlockSpec((pl.Element(tn), pl.Element(k)),
                               lambda j, i: (pl.multiple_of(row0 + j * tn, math.gcd(row0, tn)), 0))],
        out_specs=pl.BlockSpec((tm, tn), lambda j, i: (i, j)),
        out_shape=jax.ShapeDtypeStruct((m, n_cols), F32),
        scratch_shapes=[pltpu.VMEM((tn, k), BF16)],
        compiler_params=_cparams(("arbitrary", "arbitrary")),
        name="proj_t",
    )(a, bt)


def _split3(a):
    hi = a.astype(BF16)
    r1 = a - hi.astype(F32)
    mid = r1.astype(BF16)
    lo = (r1 - mid.astype(F32)).astype(BF16)
    return hi, mid, lo


def _dot_a01(a, m01, terms=3):
    hi, mid, lo = _split3(a)
    out = jnp.dot(hi, m01, preferred_element_type=F32) + jnp.dot(mid, m01, preferred_element_type=F32)
    if terms == 3:
        out = out + jnp.dot(lo, m01, preferred_element_type=F32)
    return out


def _dot_01a(m01, a):
    hi, mid, lo = _split3(a)
    return (jnp.dot(m01, hi, preferred_element_type=F32)
            + jnp.dot(m01, mid, preferred_element_type=F32)
            + jnp.dot(m01, lo, preferred_element_type=F32))


def _seg_masks(seg_len):
    ti = lax.broadcasted_iota(I32, (ROWS, ROWS), 0)
    si = lax.broadcasted_iota(I32, (ROWS, ROWS), 1)
    if seg_len == ROWS:
        causal = si <= ti
        causal_t = ti <= si
        sel = si == ROWS - 1
    else:
        shift = int(math.log2(seg_len))
        tseg = lax.shift_right_logical(ti, shift)
        sseg = lax.shift_right_logical(si, shift)
        same = tseg == sseg
        causal = jnp.logical_and(si <= ti, same)
        causal_t = jnp.logical_and(ti <= si, same)
        sel = si == lax.shift_left(tseg, shift) + (seg_len - 1)
    return causal, causal_t, sel


def _conv_block(u, ext_scr, w, b, c, first_valid):
    halo = V7X_SUBLANES

    @pl.when(c == 0)
    def _():
        ext_scr[0:halo, :] = jnp.zeros((halo, u.shape[1]), F32)

    ext_scr[halo:halo + ROWS, :] = u
    y = b + w[0:1] * ext_scr[pl.ds(halo - 3, ROWS), :]
    y = y + w[1:2] * ext_scr[pl.ds(halo - 2, ROWS), :]
    y = y + w[2:3] * ext_scr[pl.ds(halo - 1, ROWS), :]
    y = y + w[3:4] * u

    @pl.when(c == 0)
    def _():
        ext_scr[0:halo, :] = ext_scr[first_valid:first_valid + halo, :]

    @pl.when(c != 0)
    def _():
        ext_scr[0:halo, :] = ext_scr[ROWS:ROWS + halo, :]

    return y


def _alias_prev(in_specs, args, y_prev):
    if y_prev is None:
        return {}
    in_specs.append(pl.BlockSpec(memory_space=pl.ANY))
    args.append(y_prev)
    return {len(args) - 1: 0}


def _ssd_body(xs_ref, bm_ref, cm_ref, z_ref, dtc_ref, lac_ref, lar_ref, dx_ref, ng_ref, *rest,
              seg_len, n_chunks, first_valid, has_s0, has_prev, conv):
    rest = list(rest)
    cw = [rest.pop(0) for _ in range(6)] if conv else None
    s0_ref = rest.pop(0) if has_s0 else None
    if has_prev:
        rest.pop(0)
    y_ref, sf_ref, s_scr = rest[:3]
    nseg = ROWS // seg_len
    c = pl.program_id(2)

    @pl.when(c == 0)
    def _():
        for i in range(nseg):
            if has_s0:
                s_scr[i] = jnp.concatenate([s0_ref[i, j] for j in range(SSD_HPG)], axis=0)
            else:
                s_scr[i] = jnp.zeros((GROUP_W, SSD_STATE), F32)

    if conv:
        u = jnp.concatenate([xs_ref[...], bm_ref[...], cm_ref[...]], axis=1)
        w = jnp.concatenate([cw[0][...], cw[1][...], cw[2][...]], axis=1)
        b = jnp.concatenate([cw[3][...], cw[4][...], cw[5][...]], axis=1)
        act = _silu(_conv_block(u, rest[3], w, b, c, first_valid))
        xs_in = act[:, :GROUP_W]
        bm_in = act[:, GROUP_W:GROUP_W + SSD_STATE]
        cm_in = act[:, GROUP_W + SSD_STATE:]
    else:
        xs_in, bm_in, cm_in = xs_ref[...], bm_ref[...], cm_ref[...]

    nvalid = jnp.where(c == 0, first_valid, ROWS) if first_valid != ROWS else ROWS
    row_c = lax.broadcasted_iota(I32, (ROWS, 1), 0)
    row_r = lax.broadcasted_iota(I32, (1, ROWS), 1)
    valid = row_c < nvalid
    valid_r = row_r < nvalid
    causal, causal_t, sel = _seg_masks(seg_len)
    tril = causal.astype(BF16)
    triu = causal_t.astype(BF16)
    e_i = lax.broadcasted_iota(I32, (SSD_HPG, GROUP_W), 0)
    e_l = lax.shift_right_logical(lax.broadcasted_iota(I32, (SSD_HPG, GROUP_W), 1), 6)
    expand = (e_i == e_l).astype(BF16)
    lane_head = lax.shift_right_logical(lax.broadcasted_iota(I32, (1, GROUP_W), 1), 6)

    xs = jnp.where(valid, xs_in, 0.0)
    bm = jnp.where(valid, bm_in, 0.0).astype(BF16)
    cm = jnp.where(valid, cm_in, 0.0).astype(BF16)
    z = jnp.where(valid, z_ref[...], 0.0)
    dt = jnp.where(valid, dtc_ref[0], 0.0)
    la = jnp.where(valid, lac_ref[0], 0.0)
    la_r = jnp.where(valid_r, lar_ref[0], 0.0)

    cum = _dot_01a(tril, la)
    cum_r = _dot_a01(la_r, triu)
    if seg_len == ROWS:
        cum_last = jnp.broadcast_to(cum[ROWS - 1:ROWS, :], (ROWS, SSD_HPG))
    else:
        cum_last = _dot_01a(sel.astype(BF16), cum)
    ecum_x = _dot_a01(jnp.exp(cum), expand)
    v = xs * _dot_a01(dt, expand, terms=2)
    kvw = (v * _dot_a01(jnp.exp(cum_last - cum), expand, terms=2)).astype(BF16)

    scores = lax.dot_general(cm, bm, (((1,), (1,)), ((), ())), preferred_element_type=F32)
    y = jnp.zeros((ROWS, GROUP_W), F32)
    for j in range(SSD_HPG):
        dj = jnp.exp(jnp.where(causal, cum[:, j:j + 1] - cum_r[j:j + 1, :], -jnp.inf))
        pj = (scores * dj).astype(BF16)
        vj = jnp.where(lane_head == j, v, 0.0).astype(BF16)
        y = y + jnp.dot(pj, vj, preferred_element_type=F32)

    y_inter = []
    for i in range(nseg):
        r0 = i * seg_len
        r_last = r0 + seg_len - 1
        s_old = s_scr[i]
        y_inter.append(lax.dot_general(cm[r0:r0 + seg_len], s_old.astype(BF16), (((1,), (1,)), ((), ())),
                                       preferred_element_type=F32))
        cs = lax.dot_general(kvw[r0:r0 + seg_len], bm[r0:r0 + seg_len], (((0,), (0,)), ((), ())),
                             preferred_element_type=F32)
        e_last = jnp.exp(cum_r[:, r_last:r_last + 1])
        dec = jnp.concatenate([jnp.broadcast_to(e_last[j:j + 1, :], (SSD_HEAD_DIM, SSD_STATE))
                               for j in range(SSD_HPG)], axis=0)
        s_scr[i] = dec * s_old + cs
    y_inter = y_inter[0] if nseg == 1 else jnp.concatenate(y_inter, axis=0)
    y = y + y_inter * ecum_x
    y = y + dx_ref[...] * xs
    y = y * _silu(z)
    y = y * lax.rsqrt(jnp.mean(y * y, axis=-1, keepdims=True) + EPS) * ng_ref[...]
    y_ref[...] = y.astype(y_ref.dtype)

    @pl.when(c == n_chunks - 1)
    def _():
        for i in range(nseg):
            s_fin = s_scr[i]
            for j in range(SSD_HPG):
                sf_ref[i, j] = s_fin[j * SSD_HEAD_DIM:(j + 1) * SSD_HEAD_DIM, :]


def _ssd_scan(xbc_src, xbc_blk0, data_block, p1, dtc, lac, lar, dx, ng, conv_wb, s0, y_prev, *, out_width,
              nb, n_chunks, seg_len, first_valid, row_block):
    t_rows = p1.shape[0]
    nseg = ROWS // seg_len
    conv = conv_wb is not None
    d_inner = SSD_GROUPS * GROUP_W
    per128 = GROUP_W // SSD_STATE
    b_blk0 = (xbc_blk0 + SSD_GROUPS) * per128
    c_blk0 = b_blk0 + SSD_GROUPS
    in_specs = [
        pl.BlockSpec((ROWS, GROUP_W), lambda b, g, c: (data_block(b, c), xbc_blk0 + g)),
        pl.BlockSpec((ROWS, SSD_STATE), lambda b, g, c: (data_block(b, c), b_blk0 + g)),
        pl.BlockSpec((ROWS, SSD_STATE), lambda b, g, c: (data_block(b, c), c_blk0 + g)),
        pl.BlockSpec((ROWS, GROUP_W), lambda b, g, c: (row_block(b, c), g)),
        pl.BlockSpec((None, 1, ROWS, SSD_HPG), lambda b, g, c: (g, 0, row_block(b, c), 0)),
        pl.BlockSpec((None, 1, ROWS, SSD_HPG), lambda b, g, c: (g, 0, row_block(b, c), 0)),
        pl.BlockSpec((None, 1, SSD_HPG, ROWS), lambda b, g, c: (g, 0, 0, row_block(b, c))),
        pl.BlockSpec((None, 1, GROUP_W), lambda b, g, c: (g, 0, 0)),
        pl.BlockSpec((None, 1, GROUP_W), lambda b, g, c: (g, 0, 0)),
    ]
    args = [xbc_src, xbc_src, xbc_src, p1, dtc, lac, lar, dx, ng]
    scratch = [pltpu.VMEM((nseg, GROUP_W, SSD_STATE), F32)]
    if conv:
        cw, cb = conv_wb
        nbw = SSD_GROUPS * per128
        for arr, rows in ((cw, CONV_W), (cb, 1)):
            in_specs += [pl.BlockSpec((rows, GROUP_W), lambda b, g, c: (0, g)),
                         pl.BlockSpec((rows, SSD_STATE), lambda b, g, c: (0, nbw + g)),
                         pl.BlockSpec((rows, SSD_STATE), lambda b, g, c: (0, nbw + SSD_GROUPS + g))]
            args += [arr, arr, arr]
        scratch.append(pltpu.VMEM((ROWS + V7X_SUBLANES, XBC_W), F32))
    state_spec = pl.BlockSpec((nseg, SSD_HPG, SSD_HEAD_DIM, SSD_STATE), lambda b, g, c: (b, g, 0, 0))
    if s0 is not None:
        in_specs.append(state_spec)
        args.append(s0)
    aliases = _alias_prev(in_specs, args, y_prev)
    n_heads = SSD_GROUPS * SSD_HPG
    return pl.pallas_call(
        functools.partial(_ssd_body, seg_len=seg_len, n_chunks=n_chunks, first_valid=first_valid,
                          has_s0=s0 is not None, has_prev=y_prev is not None, conv=conv),
        grid=(nb, SSD_GROUPS, n_chunks),
        in_specs=in_specs,
        out_specs=[pl.BlockSpec((ROWS, GROUP_W), lambda b, g, c: (row_block(b, c), g)), state_spec],
        out_shape=[jax.ShapeDtypeStruct((t_rows, out_width), BF16),
                   jax.ShapeDtypeStruct((nb * nseg, n_heads, SSD_HEAD_DIM, SSD_STATE), F32)],
        scratch_shapes=scratch,
        input_output_aliases=aliases,
        compiler_params=_cparams(("arbitrary", "arbitrary", "arbitrary")),
        name="ssd_scan",
    )(*args)


def _ret_body(q_ref, k_ref, v_ref, g_ref, cos_ref, sin_ref, lg_ref, *rest,
              seg_len, n_chunks, first_valid, has_s0, has_prev):
    s0_ref = rest[0] if has_s0 else None
    y_ref, sf_ref, s_scr = rest[int(has_s0) + int(has_prev):]
    nseg = ROWS // seg_len
    half = RET_DIM // 2
    hb = q_ref.shape[1] // RET_DIM
    c = pl.program_id(2)

    @pl.when(c == 0)
    def _():
        for i in range(nseg):
            for hh in range(hb):
                if has_s0:
                    s_scr[i, hh] = s0_ref[i, hh]
                else:
                    s_scr[i, hh] = jnp.zeros((RET_DIM, RET_DIM), F32)

    nvalid = jnp.where(c == 0, first_valid, ROWS) if first_valid != ROWS else ROWS
    row_c = lax.broadcasted_iota(I32, (ROWS, 1), 0)
    row_r = lax.broadcasted_iota(I32, (1, ROWS), 1)
    valid = row_c < nvalid
    causal, _, _ = _seg_masks(seg_len)
    pos_c = jnp.minimum(jnp.bitwise_and(row_c, seg_len - 1) + 1, nvalid).astype(F32)
    pos_r = jnp.minimum(jnp.bitwise_and(row_r, seg_len - 1) + 1, nvalid).astype(F32)
    last = jnp.minimum(seg_len, nvalid).astype(F32) if first_valid != ROWS else float(seg_len)
    cos = cos_ref[...]
    sin = sin_ref[...]

    def rot(x):
        x1, x2 = x[:, :half], x[:, half:]
        return jnp.concatenate([x1 * cos - x2 * sin, x1 * sin + x2 * cos], axis=1)

    for hh in range(hb):
        cols = slice(hh * RET_DIM, (hh + 1) * RET_DIM)
        lg_c = lg_ref[hh][:, 0:1]
        cum = pos_c * lg_c
        cum_r = pos_r * lg_c
        cum_last = last * lg_c
        q = rot(jnp.where(valid, q_ref[:, cols], 0.0)).astype(BF16)
        k = (rot(jnp.where(valid, k_ref[:, cols], 0.0)) * (RET_DIM ** -0.5)).astype(BF16)
        v = jnp.where(valid, v_ref[:, cols], 0.0)
        gate = jnp.where(valid, g_ref[:, cols], 0.0)

        decay = jnp.exp(jnp.where(causal, cum - cum_r, -jnp.inf))
        scores = lax.dot_general(q, k, (((1,), (1,)), ((), ())), preferred_element_type=F32)
        y = jnp.dot((scores * decay).astype(BF16), v.astype(BF16), preferred_element_type=F32)
        kvw = (v * jnp.exp(cum_last - cum)).astype(BF16)
        ecum = jnp.exp(cum)
        e_last = jnp.exp(cum_last)

        y_inter = []
        for i in range(nseg):
            r0 = i * seg_len
            s_old = s_scr[i, hh]
            y_inter.append(jnp.dot(q[r0:r0 + seg_len], s_old.astype(BF16), preferred_element_type=F32))
            cs = lax.dot_general(k[r0:r0 + seg_len], kvw[r0:r0 + seg_len], (((0,), (0,)), ((), ())),
                                 preferred_element_type=F32)
            s_scr[i, hh] = e_last * s_old + cs
        y_inter = y_inter[0] if nseg == 1 else jnp.concatenate(y_inter, axis=0)
        y = y + y_inter * ecum
        y = y * lax.rsqrt(jnp.mean(y * y, axis=-1, keepdims=True) + EPS)
        y = y * _silu(gate)
        y_ref[:, cols] = y.astype(y_ref.dtype)

    @pl.when(c == n_chunks - 1)
    def _():
        for i in range(nseg):
            for hh in range(hb):
                sf_ref[i, hh] = s_scr[i, hh]


def _ret_scan(p3, cos, sin, lg, s0, y_prev, *, hb, out_width, out_col0, nb, n_chunks, seg_len, first_valid,
              row_block, cs_block):
    t_rows = p3.shape[0]
    nseg = ROWS // seg_len
    half = RET_DIM // 2
    wide = hb * RET_DIM
    hblk = RET_HEADS // hb
    out_blk0 = out_col0 // wide
    in_specs = [
        pl.BlockSpec((ROWS, wide), lambda b, h, c: (row_block(b, c), h)),
        pl.BlockSpec((ROWS, wide), lambda b, h, c: (row_block(b, c), hblk + h)),
        pl.BlockSpec((ROWS, wide), lambda b, h, c: (row_block(b, c), 2 * hblk + h)),
        pl.BlockSpec((ROWS, wide), lambda b, h, c: (row_block(b, c), 3 * hblk + h)),
        pl.BlockSpec((ROWS, half), lambda b, h, c: (cs_block(b, c), 0)),
        pl.BlockSpec((ROWS, half), lambda b, h, c: (cs_block(b, c), 0)),
        pl.BlockSpec((hb, 1, V7X_LANES), lambda b, h, c: (h, 0, 0)),
    ]
    args = [p3, p3, p3, p3, cos, sin, lg]
    state_spec = pl.BlockSpec((nseg, hb, RET_DIM, RET_DIM), lambda b, h, c: (b, h, 0, 0))
    if s0 is not None:
        in_specs.append(state_spec)
        args.append(s0)
    aliases = _alias_prev(in_specs, args, y_prev)
    return pl.pallas_call(
        functools.partial(_ret_body, seg_len=seg_len, n_chunks=n_chunks, first_valid=first_valid,
                          has_s0=s0 is not None, has_prev=y_prev is not None),
        grid=(nb, hblk, n_chunks),
        in_specs=in_specs,
        out_specs=[pl.BlockSpec((ROWS, wide), lambda b, h, c: (row_block(b, c), out_blk0 + h)),
                   state_spec],
        out_shape=[jax.ShapeDtypeStruct((t_rows, out_width), BF16),
                   jax.ShapeDtypeStruct((nb * nseg, RET_HEADS, RET_DIM, RET_DIM), F32)],
        scratch_shapes=[pltpu.VMEM((nseg, hb, RET_DIM, RET_DIM), F32)],
        input_output_aliases=aliases,
        compiler_params=_cparams(("arbitrary", "arbitrary", "arbitrary")),
        name="ret_scan",
    )(*args)


def _lru_body(x_ref, gate_ref, wa_ref, wx_ref, ba_ref, bx_ref, sp_ref, h0_ref, *rest,
              seg_len, n_chunks, first_valid, mark_pos0, has_prev, conv):
    rest = list(rest)
    cw = [rest.pop(0) for _ in range(2)] if conv else None
    if has_prev:
        rest.pop(0)
    y_ref, hl_ref, carry = rest[:3]
    c = pl.program_id(2)
    nvalid = jnp.where(c == 0, first_valid, ROWS) if first_valid != ROWS else ROWS
    row_c = lax.broadcasted_iota(I32, (ROWS, 1), 0)
    valid = row_c < nvalid

    if conv:
        x_in = _conv_block(x_ref[...], rest[3], cw[0][...], cw[1][...], c, first_valid)
    else:
        x_in = x_ref[...]
    x = jnp.where(valid, x_in, 0.0)
    xb = x.astype(BF16)
    r = _sigmoid(jnp.dot(xb, wa_ref[...], preferred_element_type=F32) + ba_ref[...])
    ig = _sigmoid(jnp.dot(xb, wx_ref[...], preferred_element_type=F32) + bx_ref[...])
    log_a = -LRU_C * r * sp_ref[...]
    a = jnp.exp(log_a)
    m2 = -jnp.tanh(log_a) * (a * a + 1.0)
    mult = m2 * lax.rsqrt(jnp.maximum(m2, F32_TINY))
    if mark_pos0:
        mult = jnp.where(jnp.logical_and(c == 0, row_c == 0), 1.0, mult)
    bterm = mult * ig * x
    a = jnp.where(valid, a, 1.0)
    bterm = jnp.where(valid, bterm, 0.0)

    sub = lax.broadcasted_iota(I32, (V7X_SUBLANES, PAIR_W), 0)
    if seg_len == ROWS:
        @pl.when(c == 0)
        def _():
            carry[...] = h0_ref[...]
        h_prev = carry[...]
    tiles = []
    for t in range(ROWS // V7X_SUBLANES):
        at = a[8 * t:8 * t + 8]
        bt = bterm[8 * t:8 * t + 8]
        for d in (1, 2, 4):
            a_sh = jnp.where(sub >= d, pltpu.roll(at, d, 0), 1.0)
            b_sh = jnp.where(sub >= d, pltpu.roll(bt, d, 0), 0.0)
            bt = at * b_sh + bt
            at = at * a_sh
        if seg_len != ROWS:
            h_prev = h0_ref[t:t + 1, :]
        ht = bt + at * h_prev
        h_prev = ht[7:8, :]
        if seg_len != ROWS:
            hl_ref[t:t + 1, :] = h_prev
        tiles.append(ht)
    h = jnp.concatenate(tiles, axis=0)
    if seg_len == ROWS:
        carry[...] = h_prev

        @pl.when(c == n_chunks - 1)
        def _():
            hl_ref[...] = h_prev

    g = gate_ref[...]
    gelu = 0.5 * g * (1.0 + jnp.tanh(math.sqrt(2.0 / math.pi) * (g + 0.044715 * (g * g * g))))
    y_ref[...] = (h * gelu).astype(y_ref.dtype)


def _lru_scan(x_src, x_blk0, data_block, p_lru, wa_p, wx_p, ba_p, bx_p, sp_p, conv_wb, h0, y_prev, *,
              nb, n_chunks, seg_len, first_valid, mark_pos0, row_block):
    t_rows = p_lru.shape[0]
    width = p_lru.shape[1] // 2
    n_pairs = width // PAIR_W
    nseg = ROWS // seg_len
    conv = conv_wb is not None
    hrows = 1 if seg_len == ROWS else nseg
    h_spec = pl.BlockSpec((None, hrows, PAIR_W), lambda b, p, c: (b, 0, p))
    vec_spec = pl.BlockSpec((None, 1, PAIR_W), lambda b, p, c: (p, 0, 0))
    in_specs = [
        pl.BlockSpec((ROWS, PAIR_W), lambda b, p, c: (data_block(b, c), x_blk0 + p)),
        pl.BlockSpec((ROWS, PAIR_W), lambda b, p, c: (row_block(b, c), p)),
        pl.BlockSpec((None, PAIR_W, PAIR_W), lambda b, p, c: (p, 0, 0)),
        pl.BlockSpec((None, PAIR_W, PAIR_W), lambda b, p, c: (p, 0, 0)),
        vec_spec, vec_spec, vec_spec, h_spec,
    ]
    args = [x_src, p_lru, wa_p, wx_p, ba_p, bx_p, sp_p, h0]
    scratch = [pltpu.VMEM((1, PAIR_W), F32)]
    if conv:
        cw, cb = conv_wb
        in_specs += [pl.BlockSpec((CONV_W, PAIR_W), lambda b, p, c: (0, p)),
                     pl.BlockSpec((1, PAIR_W), lambda b, p, c: (0, p))]
        args += [cw, cb]
        scratch.append(pltpu.VMEM((ROWS + V7X_SUBLANES, PAIR_W), F32))
    aliases = _alias_prev(in_specs, args, y_prev)
    return pl.pallas_call(
        functools.partial(_lru_body, seg_len=seg_len, n_chunks=n_chunks, first_valid=first_valid,
                          mark_pos0=mark_pos0, has_prev=y_prev is not None, conv=conv),
        grid=(nb, n_pairs, n_chunks),
        in_specs=in_specs,
        out_specs=[pl.BlockSpec((ROWS, PAIR_W), lambda b, p, c: (row_block(b, c), p)), h_spec],
        out_shape=[jax.ShapeDtypeStruct((t_rows, width), BF16),
                   jax.ShapeDtypeStruct((nb, hrows, width), F32)],
        scratch_shapes=scratch,
        input_output_aliases=aliases,
        compiler_params=_cparams(("arbitrary", "arbitrary", "arbitrary")),
        name="rglru",
    )(*args)


def _router_body(h_ref, g_ref, wr_ref, br_ref, hn_ref, route_ref, cnt_ref, carry, *, tm):
    i = pl.program_id(0)

    @pl.when(i == 0)
    def _():
        carry[...] = jnp.zeros_like(carry)

    x = h_ref[...]
    hn = x * lax.rsqrt(jnp.mean(x * x, axis=-1, keepdims=True) + EPS) * g_ref[...]
    hn_ref[...] = hn

    h_hi = hn.astype(BF16)
    h_lo = (hn - h_hi.astype(F32)).astype(BF16)
    w = wr_ref[...]
    w_hi = w.astype(BF16)
    w_lo = (w - w_hi.astype(F32)).astype(BF16)
    logits = (jnp.dot(h_hi, w_hi, preferred_element_type=F32)
              + jnp.dot(h_hi, w_lo, preferred_element_type=F32)
              + jnp.dot(h_lo, w_hi, preferred_element_type=F32)) + br_ref[...]

    lane_i = lax.broadcasted_iota(I32, (tm, V7X_LANES), 1)
    lane = lane_i.astype(F32)
    big = float(4 * V7X_LANES)
    neg = -jnp.inf
    is_g = jnp.logical_and(lane_i >= N_EXPERTS, lane_i < N_EXPERTS + MOE_GROUPS)
    glog = jnp.where(is_g, logits, neg)
    gmax = jnp.max(glog, axis=-1, keepdims=True)
    gsel = jnp.min(jnp.where(glog == gmax, lane, big), axis=-1, keepdims=True) - float(N_EXPERTS)
    gsum = jnp.sum(jnp.exp(glog - gmax), axis=-1, keepdims=True)
    pg = 1.0 / gsum
    lo = gsel * float(MOE_PER_GROUP)
    in_grp = jnp.logical_and(lane >= lo, lane < lo + MOE_PER_GROUP)
    elog = jnp.where(in_grp, logits, neg)
    emax = jnp.max(elog, axis=-1, keepdims=True)
    eexp = jnp.exp(elog - emax)
    ep = eexp / jnp.sum(eexp, axis=-1, keepdims=True)
    ep = jnp.where(in_grp, ep, -1.0)
    v1 = jnp.max(ep, axis=-1, keepdims=True)
    i1 = jnp.min(jnp.where(ep == v1, lane, big), axis=-1, keepdims=True)
    ep2 = jnp.where(lane == i1, -1.0, ep)
    v2 = jnp.max(ep2, axis=-1, keepdims=True)
    i2 = jnp.min(jnp.where(ep2 == v2, lane, big), axis=-1, keepdims=True)
    vs = v1 + v2
    w1 = v1 / vs * pg
    w2 = v2 / vs * pg

    oh1 = lane == i1
    oh2 = lane == i2
    oh = jnp.logical_or(oh1, oh2).astype(BF16)
    ti = lax.broadcasted_iota(I32, (tm, tm), 0)
    si = lax.broadcasted_iota(I32, (tm, tm), 1)
    before = (si < ti).astype(BF16)
    tot = jnp.dot(before, oh, preferred_element_type=F32) + carry[...]
    rank1 = jnp.sum(jnp.where(oh1, tot, 0.0), axis=-1, keepdims=True)
    rank2 = jnp.sum(jnp.where(oh2, tot, 0.0), axis=-1, keepdims=True)
    carry[...] = carry[...] + jnp.sum(oh.astype(F32), axis=0, keepdims=True)
    cnt_ref[...] = carry[...]

    out = jnp.where(lane_i == 0, i1, 0.0)
    out = jnp.where(lane_i == 1, i2, out)
    out = jnp.where(lane_i == 2, rank1, out)
    out = jnp.where(lane_i == 3, rank2, out)
    out = jnp.where(lane_i == 4, w1, out)
    out = jnp.where(lane_i == 5, w2, out)
    route_ref[...] = out


def _router(h, g, wr, br, tm):
    t_rows, d = h.shape
    return pl.pallas_call(
        functools.partial(_router_body, tm=tm),
        grid=(t_rows // tm,),
        in_specs=[pl.BlockSpec((tm, d), lambda i: (i, 0)),
                  pl.BlockSpec((1, d), lambda i: (0, 0)),
                  pl.BlockSpec((d, V7X_LANES), lambda i: (0, 0)),
                  pl.BlockSpec((1, V7X_LANES), lambda i: (0, 0))],
        out_specs=[pl.BlockSpec((tm, d), lambda i: (i, 0)),
                   pl.BlockSpec((tm, V7X_LANES), lambda i: (i, 0)),
                   pl.BlockSpec((1, V7X_LANES), lambda i: (0, 0))],
        out_shape=[jax.ShapeDtypeStruct((t_rows, d), F32),
                   jax.ShapeDtypeStruct((t_rows, V7X_LANES), F32),
                   jax.ShapeDtypeStruct((1, V7X_LANES), F32)],
        scratch_shapes=[pltpu.VMEM((1, V7X_LANES), F32)],
        compiler_params=_cparams(("arbitrary",)),
        name="moe_router",
    )(h, g.reshape(1, d), wr, br)


def _expert_body(te_ref, tfirst_ref, ngrp_ref, nt_ref, dst_cur, dst_nxt, hn_hbm, w1_hbm, w3_hbm, w2_hbm,
                 o_hbm, xbuf, ybuf, wb1, wb3, wb2, st13, st2, gsem, ssem, wsem13, wsem2,
                 *, tm, layer, nt_max, t_rows):
    i = pl.program_id(0)
    nt = nt_ref[0]
    slot = lax.rem(i, 2)

    def token_of(d):
        return jnp.where(d >= 2 * t_rows, 0, jnp.where(d >= t_rows, d - t_rows, d))

    def start_gather(dst_ref, s, ngrp):
        def body(r8, carry_):
            for u in range(ROW_UNROLL):
                r = r8 * ROW_UNROLL + u
                pltpu.make_async_copy(hn_hbm.at[pl.ds(token_of(dst_ref[0, 0, r]), 1)],
                                      xbuf.at[s, pl.ds(r, 1)], gsem.at[s]).start()
            return carry_
        lax.fori_loop(0, ngrp, body, 0)

    def wait_gather(s, ngrp):
        def body(r8, carry_):
            pltpu.make_async_copy(hn_hbm.at[pl.ds(0, ROW_UNROLL)], xbuf.at[s, pl.ds(0, ROW_UNROLL)],
                                  gsem.at[s]).wait()
            return carry_
        lax.fori_loop(0, ngrp, body, 0)

    def start_scatter(ngrp):
        def body(r8, carry_):
            for u in range(ROW_UNROLL):
                r = r8 * ROW_UNROLL + u
                pltpu.make_async_copy(ybuf.at[pl.ds(r, 1)], o_hbm.at[pl.ds(dst_cur[0, 0, r], 1)],
                                      ssem.at[0]).start(priority=1)
            return carry_
        lax.fori_loop(0, ngrp, body, 0)

    def wait_scatter(ngrp):
        def body(r8, carry_):
            pltpu.make_async_copy(ybuf.at[pl.ds(0, ROW_UNROLL)], o_hbm.at[pl.ds(0, ROW_UNROLL)],
                                  ssem.at[0]).wait()
            return carry_
        lax.fori_loop(0, ngrp, body, 0)

    @pl.when(i == 0)
    def _():
        xbuf[...] = jnp.zeros_like(xbuf)
        start_gather(dst_cur, 0, ngrp_ref[0])

    @pl.when(i + 1 < nt)
    def _():
        start_gather(dst_nxt, 1 - slot, ngrp_ref[jnp.minimum(i + 1, nt_max - 1)])

    @pl.when(jnp.logical_and(i < nt, tfirst_ref[i] == 1))
    def _():
        e = te_ref[i]
        n13 = 2 * (wb1.shape[0] // W13_ROWS)
        n2 = wb2.shape[0] // W2_ROWS

        def c13(k):
            src = w1_hbm if k % 2 == 0 else w3_hbm
            r0 = (k // 2) * W13_ROWS
            return pltpu.make_async_copy(src.at[layer, e, pl.ds(r0, W13_ROWS)], st13.at[k % W_RING],
                                         wsem13.at[k % W_RING])

        def c2(k):
            return pltpu.make_async_copy(w2_hbm.at[layer, e, pl.ds(k * W2_ROWS, W2_ROWS)],
                                         st2.at[k % W_RING], wsem2.at[k % W_RING])

        for k in range(W_RING):
            c13(k).start(priority=1)
        for k in range(W_RING):
            c2(k).start(priority=1)
        for k in range(n13):
            c13(k).wait()
            dstw = wb1 if k % 2 == 0 else wb3
            r0 = (k // 2) * W13_ROWS
            dstw[r0:r0 + W13_ROWS, :] = st13[k % W_RING].astype(BF16)
            if k + W_RING < n13:
                c13(k + W_RING).start(priority=1)
        for k in range(n2):
            c2(k).wait()
            wb2[k * W2_ROWS:(k + 1) * W2_ROWS, :] = st2[k % W_RING].astype(BF16)
            if k + W_RING < n2:
                c2(k + W_RING).start(priority=1)

    @pl.when(i < nt)
    def _():
        wait_gather(slot, ngrp_ref[i])
        x = xbuf[slot].astype(BF16)
        a = jnp.dot(x, wb1[...], preferred_element_type=F32)
        b = jnp.dot(x, wb3[...], preferred_element_type=F32)
        hdn = (_silu(a) * b).astype(BF16)

        @pl.when(i > 0)
        def _():
            wait_scatter(ngrp_ref[jnp.maximum(i - 1, 0)])

        ybuf[...] = jnp.dot(hdn, wb2[...], preferred_element_type=F32)
        start_scatter(ngrp_ref[i])

    @pl.when(i == nt)
    def _():
        wait_scatter(ngrp_ref[jnp.maximum(i - 1, 0)])

    @pl.when(jnp.logical_and(i == nt_max - 1, i < nt))
    def _():
        wait_scatter(ngrp_ref[i])


def _experts(hn, dst, tile_expert, tile_first, tile_ngrp, n_tiles, w1, w3, w2, *, layer, tm, nt_max):
    t_rows, d = hn.shape
    ff = w1.shape[3]
    smem_spec = lambda f: pl.BlockSpec((1, 1, tm), f, memory_space=pltpu.SMEM)
    grid_spec = pltpu.PrefetchScalarGridSpec(
        num_scalar_prefetch=4,
        grid=(nt_max,),
        in_specs=[
            smem_spec(lambda i, *_: (i, 0, 0)),
            smem_spec(lambda i, *_: (jnp.minimum(i + 1, nt_max - 1), 0, 0)),
            pl.BlockSpec(memory_space=pl.ANY),
            pl.BlockSpec(memory_space=pl.ANY),
            pl.BlockSpec(memory_space=pl.ANY),
            pl.BlockSpec(memory_space=pl.ANY),
        ],
        out_specs=pl.BlockSpec(memory_space=pl.ANY),
        scratch_shapes=[
            pltpu.VMEM((2, tm, d), F32), pltpu.VMEM((tm, d), F32),
            pltpu.VMEM((d, ff), BF16), pltpu.VMEM((d, ff), BF16), pltpu.VMEM((ff, d), BF16),
            pltpu.VMEM((W_RING, W13_ROWS, ff), F32), pltpu.VMEM((W_RING, W2_ROWS, d), F32),
            pltpu.SemaphoreType.DMA((2,)), pltpu.SemaphoreType.DMA((1,)),
            pltpu.SemaphoreType.DMA((W_RING,)), pltpu.SemaphoreType.DMA((W_RING,)),
        ],
    )
    return pl.pallas_call(
        functools.partial(_expert_body, tm=tm, layer=layer, nt_max=nt_max, t_rows=t_rows),
        grid_spec=grid_spec,
        out_shape=jax.ShapeDtypeStruct((2 * t_rows + tm, d), F32),
        compiler_params=_cparams(("arbitrary",)),
        name="moe_experts",
    )(tile_expert, tile_first, tile_ngrp, n_tiles, dst, dst, hn, w1, w3, w2)


def _combine_body(h_ref, y0_ref, y1_ref, route_ref, *rest, with_norm):
    route = route_ref[...]
    h_new = h_ref[...] + route[:, 4:5] * y0_ref[...] + route[:, 5:6] * y1_ref[...]
    if with_norm:
        g_ref, o_ref, n_ref = rest
        o_ref[...] = h_new
        y = h_new * lax.rsqrt(jnp.mean(h_new * h_new, axis=-1, keepdims=True) + EPS)
        n_ref[...] = (y * g_ref[...]).astype(n_ref.dtype)
    else:
        rest[0][...] = h_new


def _combine(h, y2, route, g_next, *, tm):
    t_rows, d = h.shape
    n = t_rows // tm
    row = pl.BlockSpec((tm, d), lambda i: (i, 0))
    in_specs = [row, row,
                pl.BlockSpec((tm, d), lambda i: (n + i, 0)),
                pl.BlockSpec((tm, V7X_LANES), lambda i: (i, 0))]
    args = [h, y2, y2, route]
    with_norm = g_next is not None
    if with_norm:
        in_specs.append(pl.BlockSpec((1, d), lambda i: (0, 0)))
        args.append(g_next.reshape(1, d))
        out_specs = [row, row]
        out_shape = [jax.ShapeDtypeStruct((t_rows, d), F32), jax.ShapeDtypeStruct((t_rows, d), BF16)]
    else:
        out_specs = row
        out_shape = jax.ShapeDtypeStruct((t_rows, d), F32)
    return pl.pallas_call(
        functools.partial(_combine_body, with_norm=with_norm),
        grid=(t_rows // tm,),
        in_specs=in_specs,
        out_specs=out_specs,
        out_shape=out_shape,
        compiler_params=_cparams(("arbitrary",)),
        name="moe_combine",
    )(*args)


def _dispatch_body(cnt_ref, rt_ref, dst_ref, te_ref, tfirst_ref, ngrp_ref, nt_ref, off_ref,
                   *, tm, nt_max, t_rows):
    tm_shift = int(math.log2(tm))
    grp_shift = int(math.log2(ROW_UNROLL))

    def init(p8, carry_):
        for u in range(ROW_UNROLL):
            p = p8 * ROW_UNROLL + u
            dst_ref[p] = 2 * t_rows + jnp.bitwise_and(p, tm - 1)
        return carry_
    lax.fori_loop(0, (nt_max * tm) // ROW_UNROLL, init, 0)

    def per_expert(e, tile0):
        c = cnt_ref[e]
        n = lax.shift_right_logical(c + (tm - 1), tm_shift)
        off_ref[e] = tile0 * tm

        def per_tile(k, carry_):
            i = tile0 + k
            te_ref[i] = e
            tfirst_ref[i] = jnp.where(k == 0, 1, 0)
            nval = jnp.minimum(c - k * tm, tm)
            ngrp_ref[i] = lax.shift_right_logical(nval + (ROW_UNROLL - 1), grp_shift)
            return carry_
        lax.fori_loop(0, n, per_tile, 0)
        return tile0 + n
    nt = lax.fori_loop(0, N_EXPERTS, per_expert, 0)
    nt_ref[0] = nt

    def dead(i, carry_):
        te_ref[i] = 0
        tfirst_ref[i] = 0
        ngrp_ref[i] = 0
        return carry_
    lax.fori_loop(nt, nt_max, dead, 0)

    def per_token(t, carry_):
        dst_ref[off_ref[rt_ref[t]] + rt_ref[2 * t_rows + t]] = t
        dst_ref[off_ref[rt_ref[t_rows + t]] + rt_ref[3 * t_rows + t]] = t_rows + t
        return carry_
    lax.fori_loop(0, t_rows, per_token, 0)


def _dispatch(cnt, rt, *, tm, nt_max, t_rows):
    smem = pl.BlockSpec(memory_space=pltpu.SMEM)
    return pl.pallas_call(
        functools.partial(_dispatch_body, tm=tm, nt_max=nt_max, t_rows=t_rows),
        in_specs=[smem, smem],
        out_specs=[smem] * 5,
        out_shape=[jax.ShapeDtypeStruct((nt_max * tm,), I32), jax.ShapeDtypeStruct((nt_max,), I32),
                   jax.ShapeDtypeStruct((nt_max,), I32), jax.ShapeDtypeStruct((nt_max,), I32),
                   jax.ShapeDtypeStruct((1,), I32)],
        scratch_shapes=[pltpu.SMEM((N_EXPERTS,), I32)],
        name="moe_dispatch",
    )(cnt, rt)


def _moe(h, layer, g, w_rg, b_rg, w_re, b_re, w1, w3, w2, g_next, *, tm_tok, tm_e):
    t_rows, d = h.shape
    pad = V7X_LANES - N_EXPERTS - MOE_GROUPS
    wr = jnp.concatenate([w_re[layer], w_rg[layer], jnp.zeros((d, pad), F32)], axis=1)
    br = jnp.concatenate([b_re[layer], b_rg[layer], jnp.zeros((pad,), F32)]).reshape(1, V7X_LANES)
    hn, route, counts = _router(h, g, wr, br, tm_tok)

    nt_max = (2 * t_rows) // tm_e + N_EXPERTS
    cnt = counts[0, :N_EXPERTS].astype(I32)
    rt = route[:, :4].astype(I32).T.reshape(4 * t_rows)
    dst, te, tfirst, ngrp, n_tiles = _dispatch(cnt, rt, tm=tm_e, nt_max=nt_max, t_rows=t_rows)

    y2 = _experts(hn, dst.reshape(nt_max, 1, tm_e), te, tfirst, ngrp, n_tiles, w1, w3, w2,
                  layer=layer, tm=tm_e, nt_max=nt_max)
    return _combine(h, y2, route, g_next, tm=_pick(t_rows, (256, 128)))


def _conv_sample(u, buf, w, b):
    full = jnp.concatenate([buf, u], axis=1)
    l = u.shape[1]
    y = b + w[0] * full[:, 0:l]
    for t in range(1, CONV_W):
        y = y + w[t] * full[:, t:t + l]
    return y, full[:, -(CONV_W - 1):]


def _pair_blocks(w):
    nblk, bw, _ = w.shape
    w = w.reshape(nblk // 2, 2, bw, bw)
    z = jnp.zeros((nblk // 2, bw, bw), w.dtype)
    top = jnp.concatenate([w[:, 0], z], axis=2)
    bot = jnp.concatenate([z, w[:, 1]], axis=2)
    return jnp.concatenate([top, bot], axis=1).astype(BF16)


def kernel(x_prompt, x_sample, state_ssd_conv, state_ssd, state_ret, state_lru_conv, state_lru, meta_tokens, norm_mix, norm_ffn, norm_final, w_in0, ssd_conv_w, ssd_conv_b, ssd_dt_bias, ssd_a_log, ssd_d, ssd_norm_g, w_out0, w_in1, lru_conv_w, lru_conv_b, lru_wa, lru_ba, lru_wx, lru_bx, lru_lambda, w_out1, moe_w_rg, moe_b_rg, moe_w_re, moe_b_re, moe_w1, moe_w3, moe_w2):
    nb, seq, d = x_prompt.shape
    nbs, ls, _ = x_sample.shape
    assert seq % ROWS == 0 and ROWS % ls == 0 and (nbs * ls) % ROWS == 0 and ls >= CONV_W - 1
    n_xblk = seq // ROWS
    blk_p = n_xblk + 1
    lp_pad = blk_p * ROWS
    tp = nb * lp_pad
    ts = nbs * ls
    t_all = tp + ts
    nbs_blk = ts // ROWS

    def rb_prompt(b, c):
        return b * blk_p + lax.rem(c + n_xblk, blk_p)

    def rb_sample(b, c):
        return nb * blk_p + b

    def rb_own(b, c):
        return b

    def _last_prompt_rows(p, c0, c1):
        return jnp.stack([p[b * lp_pad + seq - (CONV_W - 1):b * lp_pad + seq, c0:c1] for b in range(nb)])

    prompt_kw = dict(nb=nb, n_chunks=blk_p, seg_len=ROWS, first_valid=N_META, row_block=rb_prompt)
    sample_kw = dict(nb=nbs_blk, n_chunks=1, seg_len=ls, first_valid=ROWS, row_block=rb_sample)

    h, hn = _embed_norm(x_prompt, x_sample, meta_tokens, norm_mix[0])

    tm_tok = _pick(t_all, (512, 256, 128))
    tm_mm = _pick(t_all, (1216, 608, 512, 256, 128))

    d_inner = SSD_GROUPS * GROUP_W
    conv_dim = d_inner + 2 * SSD_GROUPS * SSD_STATE
    n_heads = SSD_GROUPS * SSD_HPG
    c_zx = d_inner + conv_dim
    c_qkvg = w_in0.shape[1] - c_zx - n_heads
    w_in0_t = w_in0.T
    p1 = _matmul_t(hn, w_in0_t, 0, c_zx, tm_mm, 512)
    pdt = _matmul_t(hn, w_in0_t, c_zx, n_heads, tm_mm, n_heads)
    p3 = _matmul_t(hn, w_in0_t, c_zx + n_heads, c_qkvg, tm_mm, 512)

    dt = jax.nn.softplus(pdt + ssd_dt_bias)
    la = dt * (-jnp.exp(ssd_a_log))
    dt3 = dt.reshape(t_all, SSD_GROUPS, SSD_HPG)
    la3 = la.reshape(t_all, SSD_GROUPS, SSD_HPG)
    dtc = dt3.transpose(1, 0, 2)[:, None]
    lac = la3.transpose(1, 0, 2)[:, None]
    lar = la3.transpose(1, 2, 0)[:, None]
    dx = jnp.repeat(ssd_d, SSD_HEAD_DIM).reshape(SSD_GROUPS, 1, GROUP_W)
    ng = ssd_norm_g.reshape(SSD_GROUPS, 1, GROUP_W)

    xbc_s, conv_s = _conv_sample(p1[tp:, d_inner:].reshape(nbs, ls, conv_dim), state_ssd_conv,
                                 ssd_conv_w, ssd_conv_b)
    xbc_s = jax.nn.silu(xbc_s).reshape(ts, conv_dim)
    conv_p = _last_prompt_rows(p1, d_inner, c_zx)

    mix_w = d_inner + RET_HEADS * RET_DIM
    ssd_common = (p1, dtc, lac, lar, dx, ng)
    mix, ssd_p = _ssd_scan(p1, d_inner // GROUP_W, rb_prompt, *ssd_common,
                           (ssd_conv_w, ssd_conv_b.reshape(1, conv_dim)), None, None,
                           out_width=mix_w, **prompt_kw)
    mix, ssd_s = _ssd_scan(xbc_s, 0, rb_own, *ssd_common, None, jnp.swapaxes(state_ssd, 2, 3), mix,
                           out_width=mix_w, **sample_kw)

    half = RET_DIM // 2
    inv = 1.0 / (ROPE_BASE ** (jnp.arange(half, dtype=F32) / half))
    pos_p = jnp.concatenate([N_META + jnp.arange(seq, dtype=I32), jnp.arange(N_META, dtype=I32),
                             jnp.zeros((lp_pad - seq - N_META,), I32)])
    pos_s = PAST_LEN + (jnp.arange(ROWS, dtype=I32) % ls)
    ang = jnp.concatenate([pos_p, pos_s]).astype(F32)[:, None] * inv[None, :]
    cos, sin = jnp.cos(ang), jnp.sin(ang)
    log_gamma = jnp.log1p(-jnp.exp2(-5.0 - jnp.arange(RET_HEADS, dtype=F32)))
    lg = jnp.broadcast_to(log_gamma[:, None, None], (RET_HEADS, 1, V7X_LANES))
    ret_kw = dict(out_width=mix_w, out_col0=d_inner)
    mix, ret_p = _ret_scan(p3, cos, sin, lg, None, mix, hb=RET_HB_PROMPT,
                           cs_block=lambda b, c: lax.rem(c + n_xblk, blk_p), **ret_kw, **prompt_kw)
    mix, ret_s = _ret_scan(p3, cos, sin, lg, state_ret, mix, hb=RET_HB_SAMPLE, cs_block=lambda b, c: blk_p,
                           **ret_kw, **sample_kw)
    tm_o = _pick(t_all, (608, 512, 256, 128))
    h = _matmul(mix, w_out0.astype(BF16), d, tm_o, 512, res=h)
    moe_kw = dict(tm_tok=tm_tok, tm_e=min(MOE_TM, tm_tok))
    moe_w = (moe_w_rg, moe_b_rg, moe_w_re, moe_b_re, moe_w1, moe_w3, moe_w2)
    h, hn = _moe(h, 0, norm_ffn[0], *moe_w, norm_mix[1], **moe_kw)

    width = lru_lambda.shape[0]
    p_lru = _matmul(hn, w_in1, 2 * width, tm_mm, 512)
    xc_s, lconv_s = _conv_sample(p_lru[tp:, width:].reshape(nbs, ls, width), state_lru_conv,
                                 lru_conv_w, lru_conv_b)
    xc_s = xc_s.reshape(ts, width)
    lconv_p = _last_prompt_rows(p_lru, width, 2 * width)
    n_pairs = width // PAIR_W
    lru_common = (p_lru, _pair_blocks(lru_wa), _pair_blocks(lru_wx), lru_ba.reshape(n_pairs, 1, PAIR_W),
                  lru_bx.reshape(n_pairs, 1, PAIR_W),
                  jax.nn.softplus(-lru_lambda).reshape(n_pairs, 1, PAIR_W))
    y_lru, lru_p = _lru_scan(p_lru, n_pairs, rb_prompt, *lru_common,
                             (lru_conv_w, lru_conv_b.reshape(1, width)), jnp.zeros((nb, 1, width), F32),
                             None, mark_pos0=True, **prompt_kw)
    y_lru, lru_s = _lru_scan(xc_s, 0, rb_own, *lru_common, None,
                             state_lru.reshape(nbs_blk, ROWS // ls, width), y_lru, mark_pos0=False,
                             **sample_kw)
    h = _matmul(y_lru, w_out1, d, tm_o, 512, res=h)
    h = _moe(h, 1, norm_ffn[1], *moe_w, None, **moe_kw)

    y_prompt = _rmsnorm(h, norm_final, F32, ROWS, n_out_blocks=nb * n_xblk,
                        in_block=lambda i: (i // n_xblk) * blk_p + lax.rem(i, n_xblk))
    y_sample = _rmsnorm(h, norm_final, F32, ROWS, n_out_blocks=nbs_blk, in_block=lambda i: nb * blk_p + i)
    return (y_prompt.reshape(nb, seq, d), y_sample.reshape(nbs, ls, d),
            conv_p, jnp.swapaxes(ssd_p, 2, 3), ret_p, lconv_p, lru_p.reshape(nb, width),
            conv_s, jnp.swapaxes(ssd_s, 2, 3), ret_s, lconv_s, lru_s.reshape(nbs, width))
```

```python
import functools
import math

import jax
import jax.numpy as jnp
from jax import lax
from jax.experimental import pallas as pl
from jax.experimental.pallas import tpu as pltpu

F32, BF16, I32 = jnp.float32, jnp.bfloat16, jnp.int32

N_META = 16
CONV_W = 4
EPS = 1e-6
F32_TINY = 1.1754944e-38
PAST_LEN = 16384
SSD_HEAD_DIM = 64
SSD_GROUPS = 8
SSD_HPG = 8
SSD_STATE = 128
RET_HEADS = 16
RET_DIM = 256
RET_HB_PROMPT = 4
RET_HB_SAMPLE = 2
ROPE_BASE = 10000.0
LRU_C = 8.0
MOE_GROUPS = 4
MOE_PER_GROUP = 8
N_EXPERTS = MOE_GROUPS * MOE_PER_GROUP

V7X_LANES = 128
V7X_SUBLANES = 8
V7X_VMEM_LIMIT = 56 * 1024 * 1024
ROWS = 128
GROUP_W = SSD_HPG * SSD_HEAD_DIM
SSD_GPS = 2
XBC_W = GROUP_W + 2 * SSD_STATE
PAIR_W = 2 * 320
LRU_PPS = 4
MOE_TM = 512
W_RING = 4
W13_ROWS = 512
W2_ROWS = 64
ROW_UNROLL = 8


def _cparams(sem):
    return pltpu.CompilerParams(dimension_semantics=sem, vmem_limit_bytes=V7X_VMEM_LIMIT)


def _pick(n, cands):
    for c in cands:
        if n % c == 0:
            return c
    raise ValueError(f"no tile for {n} in {cands}")


def _sigmoid(x):
    return 0.5 * jnp.tanh(0.5 * x) + 0.5


def _silu(x):
    return x * _sigmoid(x)


def _rmsnorm_body(x_ref, g_ref, o_ref):
    x = x_ref[...]
    y = x * lax.rsqrt(jnp.mean(x * x, axis=-1, keepdims=True) + EPS)
    o_ref[...] = (y * g_ref[...]).astype(o_ref.dtype)


def _rmsnorm(x, g, out_dtype, tm, n_out_blocks=None, in_block=None):
    m, d = x.shape
    n_blocks = m // tm if n_out_blocks is None else n_out_blocks
    in_map = (lambda i: (i, 0)) if in_block is None else (lambda i: (in_block(i), 0))
    return pl.pallas_call(
        _rmsnorm_body,
        grid=(n_blocks,),
        in_specs=[pl.BlockSpec((tm, d), in_map), pl.BlockSpec((1, d), lambda i: (0, 0))],
        out_specs=pl.BlockSpec((tm, d), lambda i: (i, 0)),
        out_shape=jax.ShapeDtypeStruct((n_blocks * tm, d), out_dtype),
        compiler_params=_cparams(("arbitrary",)),
        name="rmsnorm",
    )(x, g.reshape(1, d))


def _embed_body(xp_ref, xs_ref, meta_ref, g_ref, h_ref, hn_ref, *, nb, blk_p, n_xblk):
    i = pl.program_id(0)
    k = lax.rem(i, blk_p)
    is_prompt = i < nb * blk_p

    def emit(x):
        h_ref[...] = x
        y = x * lax.rsqrt(jnp.mean(x * x, axis=-1, keepdims=True) + EPS)
        hn_ref[...] = (y * g_ref[...]).astype(hn_ref.dtype)

    @pl.when(jnp.logical_and(is_prompt, k < n_xblk))
    def _():
        emit(xp_ref[...])

    @pl.when(jnp.logical_and(is_prompt, k == n_xblk))
    def _():
        meta = meta_ref[...]
        emit(jnp.concatenate([meta, jnp.zeros((ROWS - meta.shape[0], meta.shape[1]), F32)], axis=0))

    @pl.when(jnp.logical_not(is_prompt))
    def _():
        emit(xs_ref[...])


def _embed_norm(x_prompt, x_sample, meta, g):
    nb, seq, d = x_prompt.shape
    ts = x_sample.shape[0] * x_sample.shape[1]
    n_xblk = seq // ROWS
    blk_p = n_xblk + 1
    n_blocks = nb * blk_p + ts // ROWS
    row = pl.BlockSpec((ROWS, d), lambda i: (i, 0))
    return pl.pallas_call(
        functools.partial(_embed_body, nb=nb, blk_p=blk_p, n_xblk=n_xblk),
        grid=(n_blocks,),
        in_specs=[
            pl.BlockSpec((ROWS, d), lambda i: (jnp.minimum(i // blk_p, nb - 1) * n_xblk
                                               + jnp.minimum(lax.rem(i, blk_p), n_xblk - 1), 0)),
            pl.BlockSpec((ROWS, d), lambda i: (jnp.clip(i - nb * blk_p, 0, ts // ROWS - 1), 0)),
            pl.BlockSpec(meta.shape, lambda i: (0, 0)),
            pl.BlockSpec((1, d), lambda i: (0, 0)),
        ],
        out_specs=[row, row],
        out_shape=[jax.ShapeDtypeStruct((n_blocks * ROWS, d), F32),
                   jax.ShapeDtypeStruct((n_blocks * ROWS, d), BF16)],
        compiler_params=_cparams(("arbitrary",)),
        name="embed_norm",
    )(x_prompt.reshape(nb * seq, d), x_sample.reshape(ts, d), meta, g.reshape(1, d))


def _mm_body(*refs, has_res, cast_b):
    a_ref, b_ref = refs[0], refs[1]
    r_ref = refs[2] if has_res else None
    o_ref = refs[3] if has_res else refs[2]
    if cast_b:
        bs_ref = refs[-1]

        @pl.when(pl.program_id(1) == 0)
        def _():
            bs_ref[...] = b_ref[...].astype(BF16)

        b = bs_ref[...]
    else:
        b = b_ref[...]
    acc = jnp.dot(a_ref[...], b, preferred_element_type=F32)
    if has_res:
        acc = acc + r_ref[...]
    o_ref[...] = acc


def _matmul(a, b, n_cols, tm, tn, res=None):
    m, k = a.shape
    cast_b = b.dtype != BF16
    in_specs = [pl.BlockSpec((tm, k), lambda j, i: (i, 0)),
                pl.BlockSpec((k, tn), lambda j, i: (0, j))]
    args = [a, b]
    if res is not None:
        in_specs.append(pl.BlockSpec((tm, tn), lambda j, i: (i, j)))
        args.append(res)
    return pl.pallas_call(
        functools.partial(_mm_body, has_res=res is not None, cast_b=cast_b),
        grid=(n_cols // tn, m // tm),
        in_specs=in_specs,
        out_specs=pl.BlockSpec((tm, tn), lambda j, i: (i, j)),
        out_shape=jax.ShapeDtypeStruct((m, n_cols), F32),
        scratch_shapes=[pltpu.VMEM((k, tn), BF16)] if cast_b else [],
        compiler_params=_cparams(("arbitrary", "arbitrary")),
        name="proj",
    )(*args)


def _mm_t_body(a_ref, bt_ref, o_ref, bs_ref):
    @pl.when(pl.program_id(1) == 0)
    def _():
        bs_ref[...] = bt_ref[...].astype(BF16)

    o_ref[...] = lax.dot_general(a_ref[...], bs_ref[...], (((1,), (1,)), ((), ())),
                                 preferred_element_type=F32)


def _matmul_t(a, bt, row0, n_cols, tm, tn):
    m, k = a.shape
    return pl.pallas_call(
        _mm_t_body,
        grid=(n_cols // tn, m // tm),
        in_specs=[pl.BlockSpec((tm, k), lambda j, i: (i, 0)),
                  pl.BlockSpec((pl.Element(tn), pl.Element(k)),
                               lambda j, i: (pl.multiple_of(row0 + j * tn, math.gcd(row0, tn)), 0))],
        out_specs=pl.BlockSpec((tm, tn), lambda j, i: (i, j)),
        out_shape=jax.ShapeDtypeStruct((m, n_cols), F32),
        scratch_shapes=[pltpu.VMEM((tn, k), BF16)],
        compiler_params=_cparams(("arbitrary", "arbitrary")),
        name="proj_t",
    )(a, bt)


def _split3(a):
    hi = a.astype(BF16)
    r1 = a - hi.astype(F32)
    mid = r1.astype(BF16)
    lo = (r1 - mid.astype(F32)).astype(BF16)
    return hi, mid, lo


def _dot_a01(a, m01, terms=3):
    hi, mid, lo = _split3(a)
    out = jnp.dot(hi, m01, preferred_element_type=F32) + jnp.dot(mid, m01, preferred_element_type=F32)
    if terms == 3:
        out = out + jnp.dot(lo, m01, preferred_element_type=F32)
    return out


def _dot_01a(m01, a):
    hi, mid, lo = _split3(a)
    return (jnp.dot(m01, hi, preferred_element_type=F32)
            + jnp.dot(m01, mid, preferred_element_type=F32)
            + jnp.dot(m01, lo, preferred_element_type=F32))


def _seg_masks(seg_len):
    ti = lax.broadcasted_iota(I32, (ROWS, ROWS), 0)
    si = lax.broadcasted_iota(I32, (ROWS, ROWS), 1)
    if seg_len == ROWS:
        causal = si <= ti
        causal_t = ti <= si
        sel = si == ROWS - 1
    else:
        shift = int(math.log2(seg_len))
        tseg = lax.shift_right_logical(ti, shift)
        sseg = lax.shift_right_logical(si, shift)
        same = tseg == sseg
        causal = jnp.logical_and(si <= ti, same)
        causal_t = jnp.logical_and(ti <= si, same)
        sel = si == lax.shift_left(tseg, shift) + (seg_len - 1)
    return causal, causal_t, sel


def _conv_block(u, ext_scr, w, b, c, first_valid):
    halo = V7X_SUBLANES

    @pl.when(c == 0)
    def _():
        ext_scr[0:halo, :] = jnp.zeros((halo, u.shape[1]), F32)

    ext_scr[halo:halo + ROWS, :] = u
    y = b + w[0:1] * ext_scr[pl.ds(halo - 3, ROWS), :]
    y = y + w[1:2] * ext_scr[pl.ds(halo - 2, ROWS), :]
    y = y + w[2:3] * ext_scr[pl.ds(halo - 1, ROWS), :]
    y = y + w[3:4] * u

    @pl.when(c == 0)
    def _():
        ext_scr[0:halo, :] = ext_scr[first_valid:first_valid + halo, :]

    @pl.when(c != 0)
    def _():
        ext_scr[0:halo, :] = ext_scr[ROWS:ROWS + halo, :]

    return y


def _alias_prev(in_specs, args, y_prev):
    if y_prev is None:
        return {}
    in_specs.append(pl.BlockSpec(memory_space=pl.ANY))
    args.append(y_prev)
    return {len(args) - 1: 0}


def _ssd_body(xs_ref, bm_ref, cm_ref, z_ref, dtc_ref, lac_ref, lar_ref, dx_ref, ng_ref, *rest,
              seg_len, n_chunks, first_valid, has_s0, has_prev, conv):
    rest = list(rest)
    cw = [rest.pop(0) for _ in range(6)] if conv else None
    s0_ref = rest.pop(0) if has_s0 else None
    if has_prev:
        rest.pop(0)
    y_ref, sf_ref, s_scr = rest[:3]
    nseg = ROWS // seg_len
    ngr = xs_ref.shape[1] // GROUP_W
    c = pl.program_id(2)

    @pl.when(c == 0)
    def _():
        for i in range(nseg):
            for gi in range(ngr):
                if has_s0:
                    s_scr[i, gi] = jnp.concatenate([s0_ref[i, gi * SSD_HPG + j] for j in range(SSD_HPG)],
                                                   axis=0)
                else:
                    s_scr[i, gi] = jnp.zeros((GROUP_W, SSD_STATE), F32)

    if conv:
        u = jnp.concatenate([xs_ref[...], bm_ref[...], cm_ref[...]], axis=1)
        w = jnp.concatenate([cw[0][...], cw[1][...], cw[2][...]], axis=1)
        b = jnp.concatenate([cw[3][...], cw[4][...], cw[5][...]], axis=1)
        act = _silu(_conv_block(u, rest[3], w, b, c, first_valid))
        xs_all = act[:, :ngr * GROUP_W]
        bm_all = act[:, ngr * GROUP_W:ngr * (GROUP_W + SSD_STATE)]
        cm_all = act[:, ngr * (GROUP_W + SSD_STATE):]
    else:
        xs_all, bm_all, cm_all = xs_ref[...], bm_ref[...], cm_ref[...]

    nvalid = jnp.where(c == 0, first_valid, ROWS) if first_valid != ROWS else ROWS
    row_c = lax.broadcasted_iota(I32, (ROWS, 1), 0)
    row_r = lax.broadcasted_iota(I32, (1, ROWS), 1)
    valid = row_c < nvalid
    valid_r = row_r < nvalid
    causal, causal_t, sel = _seg_masks(seg_len)
    tril = causal.astype(BF16)
    triu = causal_t.astype(BF16)
    e_i = lax.broadcasted_iota(I32, (SSD_HPG, GROUP_W), 0)
    e_l = lax.shift_right_logical(lax.broadcasted_iota(I32, (SSD_HPG, GROUP_W), 1), 6)
    expand = (e_i == e_l).astype(BF16)
    lane_head = lax.shift_right_logical(lax.broadcasted_iota(I32, (1, GROUP_W), 1), 6)

    for gi in range(ngr):
        gcols = slice(gi * GROUP_W, (gi + 1) * GROUP_W)
        ncols = slice(gi * SSD_STATE, (gi + 1) * SSD_STATE)
        xs = jnp.where(valid, xs_all[:, gcols], 0.0)
        bm = jnp.where(valid, bm_all[:, ncols], 0.0).astype(BF16)
        cm = jnp.where(valid, cm_all[:, ncols], 0.0).astype(BF16)
        z = jnp.where(valid, z_ref[:, gcols], 0.0)
        dt = jnp.where(valid, dtc_ref[gi, 0], 0.0)
        la = jnp.where(valid, lac_ref[gi, 0], 0.0)
        la_r = jnp.where(valid_r, lar_ref[gi, 0], 0.0)

        cum = _dot_01a(tril, la)
        cum_r = _dot_a01(la_r, triu)
        if seg_len == ROWS:
            cum_last = jnp.broadcast_to(cum[ROWS - 1:ROWS, :], (ROWS, SSD_HPG))
        else:
            cum_last = _dot_01a(sel.astype(BF16), cum)
        ecum_x = _dot_a01(jnp.exp(cum), expand)
        v = xs * _dot_a01(dt, expand, terms=2)
        kvw = (v * _dot_a01(jnp.exp(cum_last - cum), expand, terms=2)).astype(BF16)

        scores = lax.dot_general(cm, bm, (((1,), (1,)), ((), ())), preferred_element_type=F32)
        y = jnp.zeros((ROWS, GROUP_W), F32)
        for j in range(SSD_HPG):
            dj = jnp.exp(jnp.where(causal, cum[:, j:j + 1] - cum_r[j:j + 1, :], -jnp.inf))
            pj = (scores * dj).astype(BF16)
            vj = jnp.where(lane_head == j, v, 0.0).astype(BF16)
            y = y + jnp.dot(pj, vj, preferred_element_type=F32)

        y_inter = []
        for i in range(nseg):
            r0 = i * seg_len
            r_last = r0 + seg_len - 1
            s_old = s_scr[i, gi]
            y_inter.append(lax.dot_general(cm[r0:r0 + seg_len], s_old.astype(BF16),
                                           (((1,), (1,)), ((), ())), preferred_element_type=F32))
            cs = lax.dot_general(kvw[r0:r0 + seg_len], bm[r0:r0 + seg_len], (((0,), (0,)), ((), ())),
                                 preferred_element_type=F32)
            e_last = jnp.exp(cum_r[:, r_last:r_last + 1])
            dec = jnp.concatenate([jnp.broadcast_to(e_last[j:j + 1, :], (SSD_HEAD_DIM, SSD_STATE))
                                   for j in range(SSD_HPG)], axis=0)
            s_scr[i, gi] = dec * s_old + cs
        y_inter = y_inter[0] if nseg == 1 else jnp.concatenate(y_inter, axis=0)
        y = y + y_inter * ecum_x
        y = y + dx_ref[gi] * xs
        y = y * _silu(z)
        y = y * lax.rsqrt(jnp.mean(y * y, axis=-1, keepdims=True) + EPS) * ng_ref[gi]
        y_ref[:, gcols] = y.astype(y_ref.dtype)

    @pl.when(c == n_chunks - 1)
    def _():
        for i in range(nseg):
            for gi in range(ngr):
                s_fin = s_scr[i, gi]
                for j in range(SSD_HPG):
                    sf_ref[i, gi * SSD_HPG + j] = s_fin[j * SSD_HEAD_DIM:(j + 1) * SSD_HEAD_DIM, :]


def _ssd_scan(xbc_src, xbc_col0, data_block, p1, dtc, lac, lar, dx, ng, conv_wb, s0, y_prev, *, out_width,
              nb, n_chunks, seg_len, first_valid, row_block):
    t_rows = p1.shape[0]
    nseg = ROWS // seg_len
    conv = conv_wb is not None
    d_inner = SSD_GROUPS * GROUP_W
    gw = SSD_GPS * GROUP_W
    sw = SSD_GPS * SSD_STATE
    x_blk0 = xbc_col0 // gw
    b_blk0 = (xbc_col0 + d_inner) // sw
    c_blk0 = b_blk0 + SSD_GROUPS // SSD_GPS
    aux = lambda shape, f: pl.BlockSpec((SSD_GPS,) + shape, f)
    in_specs = [
        pl.BlockSpec((ROWS, gw), lambda b, g, c: (data_block(b, c), x_blk0 + g)),
        pl.BlockSpec((ROWS, sw), lambda b, g, c: (data_block(b, c), b_blk0 + g)),
        pl.BlockSpec((ROWS, sw), lambda b, g, c: (data_block(b, c), c_blk0 + g)),
        pl.BlockSpec((ROWS, gw), lambda b, g, c: (row_block(b, c), g)),
        aux((1, ROWS, SSD_HPG), lambda b, g, c: (g, 0, row_block(b, c), 0)),
        aux((1, ROWS, SSD_HPG), lambda b, g, c: (g, 0, row_block(b, c), 0)),
        aux((1, SSD_HPG, ROWS), lambda b, g, c: (g, 0, 0, row_block(b, c))),
        aux((1, GROUP_W), lambda b, g, c: (g, 0, 0)),
        aux((1, GROUP_W), lambda b, g, c: (g, 0, 0)),
    ]
    args = [xbc_src, xbc_src, xbc_src, p1, dtc, lac, lar, dx, ng]
    scratch = [pltpu.VMEM((nseg, SSD_GPS, GROUP_W, SSD_STATE), F32)]
    if conv:
        cw, cb = conv_wb
        wb_blk0 = d_inner // sw
        for arr, rows in ((cw, CONV_W), (cb, 1)):
            in_specs += [pl.BlockSpec((rows, gw), lambda b, g, c: (0, g)),
                         pl.BlockSpec((rows, sw), lambda b, g, c: (0, wb_blk0 + g)),
                         pl.BlockSpec((rows, sw), lambda b, g, c: (0, wb_blk0 + SSD_GROUPS // SSD_GPS + g))]
            args += [arr, arr, arr]
        scratch.append(pltpu.VMEM((ROWS + V7X_SUBLANES, SSD_GPS * XBC_W), F32))
    state_spec = pl.BlockSpec((nseg, SSD_GPS * SSD_HPG, SSD_HEAD_DIM, SSD_STATE), lambda b, g, c: (b, g, 0, 0))
    if s0 is not None:
        in_specs.append(state_spec)
        args.append(s0)
    aliases = _alias_prev(in_specs, args, y_prev)
    n_heads = SSD_GROUPS * SSD_HPG
    return pl.pallas_call(
        functools.partial(_ssd_body, seg_len=seg_len, n_chunks=n_chunks, first_valid=first_valid,
                          has_s0=s0 is not None, has_prev=y_prev is not None, conv=conv),
        grid=(nb, SSD_GROUPS // SSD_GPS, n_chunks),
        in_specs=in_specs,
        out_specs=[pl.BlockSpec((ROWS, gw), lambda b, g, c: (row_block(b, c), g)), state_spec],
        out_shape=[jax.ShapeDtypeStruct((t_rows, out_width), BF16),
                   jax.ShapeDtypeStruct((nb * nseg, n_heads, SSD_HEAD_DIM, SSD_STATE), F32)],
        scratch_shapes=scratch,
        input_output_aliases=aliases,
        compiler_params=_cparams(("arbitrary", "arbitrary", "arbitrary")),
        name="ssd_scan",
    )(*args)


def _ret_body(q_ref, k_ref, v_ref, g_ref, cos_ref, sin_ref, lg_ref, *rest,
              seg_len, n_chunks, first_valid, has_s0, has_prev):
    s0_ref = rest[0] if has_s0 else None
    y_ref, sf_ref, s_scr = rest[int(has_s0) + int(has_prev):]
    nseg = ROWS // seg_len
    half = RET_DIM // 2
    hb = q_ref.shape[1] // RET_DIM
    c = pl.program_id(2)

    @pl.when(c == 0)
    def _():
        for i in range(nseg):
            for hh in range(hb):
                if has_s0:
                    s_scr[i, hh] = s0_ref[i, hh]
                else:
                    s_scr[i, hh] = jnp.zeros((RET_DIM, RET_DIM), F32)

    nvalid = jnp.where(c == 0, first_valid, ROWS) if first_valid != ROWS else ROWS
    row_c = lax.broadcasted_iota(I32, (ROWS, 1), 0)
    row_r = lax.broadcasted_iota(I32, (1, ROWS), 1)
    valid = row_c < nvalid
    causal, _, _ = _seg_masks(seg_len)
    pos_c = jnp.minimum(jnp.bitwise_and(row_c, seg_len - 1) + 1, nvalid).astype(F32)
    pos_r = jnp.minimum(jnp.bitwise_and(row_r, seg_len - 1) + 1, nvalid).astype(F32)
    last = jnp.minimum(seg_len, nvalid).astype(F32) if first_valid != ROWS else float(seg_len)
    cos = cos_ref[...]
    sin = sin_ref[...]

    def rot(x):
        x1, x2 = x[:, :half], x[:, half:]
        return jnp.concatenate([x1 * cos - x2 * sin, x1 * sin + x2 * cos], axis=1)

    for hh in range(hb):
        cols = slice(hh * RET_DIM, (hh + 1) * RET_DIM)
        lg_c = lg_ref[hh][:, 0:1]
        cum = pos_c * lg_c
        cum_r = pos_r * lg_c
        cum_last = last * lg_c
        q = rot(jnp.where(valid, q_ref[:, cols], 0.0)).astype(BF16)
        k = (rot(jnp.where(valid, k_ref[:, cols], 0.0)) * (RET_DIM ** -0.5)).astype(BF16)
        v = jnp.where(valid, v_ref[:, cols], 0.0)
        gate = jnp.where(valid, g_ref[:, cols], 0.0)

        decay = jnp.exp(jnp.where(causal, cum - cum_r, -jnp.inf))
        scores = lax.dot_general(q, k, (((1,), (1,)), ((), ())), preferred_element_type=F32)
        y = jnp.dot((scores * decay).astype(BF16), v.astype(BF16), preferred_element_type=F32)
        kvw = (v * jnp.exp(cum_last - cum)).astype(BF16)
        ecum = jnp.exp(cum)
        e_last = jnp.exp(cum_last)

        y_inter = []
        for i in range(nseg):
            r0 = i * seg_len
            s_old = s_scr[i, hh]
            y_inter.append(jnp.dot(q[r0:r0 + seg_len], s_old.astype(BF16), preferred_element_type=F32))
            cs = lax.dot_general(k[r0:r0 + seg_len], kvw[r0:r0 + seg_len], (((0,), (0,)), ((), ())),
                                 preferred_element_type=F32)
            s_scr[i, hh] = e_last * s_old + cs
        y_inter = y_inter[0] if nseg == 1 else jnp.concatenate(y_inter, axis=0)
        y = y + y_inter * ecum
        y = y * lax.rsqrt(jnp.mean(y * y, axis=-1, keepdims=True) + EPS)
        y = y * _silu(gate)
        y_ref[:, cols] = y.astype(y_ref.dtype)

    @pl.when(c == n_chunks - 1)
    def _():
        for i in range(nseg):
            for hh in range(hb):
                sf_ref[i, hh] = s_scr[i, hh]


def _ret_scan(p3, cos, sin, lg, s0, y_prev, *, hb, out_width, out_col0, nb, n_chunks, seg_len, first_valid,
              row_block, cs_block):
    t_rows = p3.shape[0]
    nseg = ROWS // seg_len
    half = RET_DIM // 2
    wide = hb * RET_DIM
    hblk = RET_HEADS // hb
    out_blk0 = out_col0 // wide
    in_specs = [
        pl.BlockSpec((ROWS, wide), lambda b, h, c: (row_block(b, c), h)),
        pl.BlockSpec((ROWS, wide), lambda b, h, c: (row_block(b, c), hblk + h)),
        pl.BlockSpec((ROWS, wide), lambda b, h, c: (row_block(b, c), 2 * hblk + h)),
        pl.BlockSpec((ROWS, wide), lambda b, h, c: (row_block(b, c), 3 * hblk + h)),
        pl.BlockSpec((ROWS, half), lambda b, h, c: (cs_block(b, c), 0)),
        pl.BlockSpec((ROWS, half), lambda b, h, c: (cs_block(b, c), 0)),
        pl.BlockSpec((hb, 1, V7X_LANES), lambda b, h, c: (h, 0, 0)),
    ]
    args = [p3, p3, p3, p3, cos, sin, lg]
    state_spec = pl.BlockSpec((nseg, hb, RET_DIM, RET_DIM), lambda b, h, c: (b, h, 0, 0))
    if s0 is not None:
        in_specs.append(state_spec)
        args.append(s0)
    aliases = _alias_prev(in_specs, args, y_prev)
    return pl.pallas_call(
        functools.partial(_ret_body, seg_len=seg_len, n_chunks=n_chunks, first_valid=first_valid,
                          has_s0=s0 is not None, has_prev=y_prev is not None),
        grid=(nb, hblk, n_chunks),
        in_specs=in_specs,
        out_specs=[pl.BlockSpec((ROWS, wide), lambda b, h, c: (row_block(b, c), out_blk0 + h)),
                   state_spec],
        out_shape=[jax.ShapeDtypeStruct((t_rows, out_width), BF16),
                   jax.ShapeDtypeStruct((nb * nseg, RET_HEADS, RET_DIM, RET_DIM), F32)],
        scratch_shapes=[pltpu.VMEM((nseg, hb, RET_DIM, RET_DIM), F32)],
        input_output_aliases=aliases,
        compiler_params=_cparams(("arbitrary", "arbitrary", "arbitrary")),
        name="ret_scan",
    )(*args)


def _lru_body(x_ref, gate_ref, wa_ref, wx_ref, ba_ref, bx_ref, sp_ref, h0_ref, *rest,
              seg_len, n_chunks, first_valid, mark_pos0, has_prev, conv):
    rest = list(rest)
    cw = [rest.pop(0) for _ in range(2)] if conv else None
    if has_prev:
        rest.pop(0)
    y_ref, hl_ref, carry = rest[:3]
    c = pl.program_id(2)
    nvalid = jnp.where(c == 0, first_valid, ROWS) if first_valid != ROWS else ROWS
    row_c = lax.broadcasted_iota(I32, (ROWS, 1), 0)
    valid = row_c < nvalid

    if conv:
        x_in = _conv_block(x_ref[...], rest[3], cw[0][...], cw[1][...], c, first_valid)
    else:
        x_in = x_ref[...]
    x = jnp.where(valid, x_in, 0.0)
    xb = x.astype(BF16)
    npp = x.shape[1] // PAIR_W
    ra, ri = [], []
    for pp in range(npp):
        xbp = xb[:, pp * PAIR_W:(pp + 1) * PAIR_W]
        ra.append(jnp.dot(xbp, wa_ref[pp], preferred_element_type=F32) + ba_ref[pp])
        ri.append(jnp.dot(xbp, wx_ref[pp], preferred_element_type=F32) + bx_ref[pp])
    r = _sigmoid(jnp.concatenate(ra, axis=1))
    ig = _sigmoid(jnp.concatenate(ri, axis=1))
    log_a = -LRU_C * r * jnp.concatenate([sp_ref[pp] for pp in range(npp)], axis=1)
    a = jnp.exp(log_a)
    m2 = -jnp.tanh(log_a) * (a * a + 1.0)
    mult = m2 * lax.rsqrt(jnp.maximum(m2, F32_TINY))
    if mark_pos0:
        mult = jnp.where(jnp.logical_and(c == 0, row_c == 0), 1.0, mult)
    bterm = mult * ig * x
    a = jnp.where(valid, a, 1.0)
    bterm = jnp.where(valid, bterm, 0.0)

    sub = lax.broadcasted_iota(I32, (V7X_SUBLANES, x.shape[1]), 0)
    if seg_len == ROWS:
        @pl.when(c == 0)
        def _():
            carry[...] = h0_ref[...]
        h_prev = carry[...]
    tiles = []
    for t in range(ROWS // V7X_SUBLANES):
        at = a[8 * t:8 * t + 8]
        bt = bterm[8 * t:8 * t + 8]
        for d in (1, 2, 4):
            a_sh = jnp.where(sub >= d, pltpu.roll(at, d, 0), 1.0)
            b_sh = jnp.where(sub >= d, pltpu.roll(bt, d, 0), 0.0)
            bt = at * b_sh + bt
            at = at * a_sh
        if seg_len != ROWS:
            h_prev = h0_ref[t:t + 1, :]
        ht = bt + at * h_prev
        h_prev = ht[7:8, :]
        if seg_len != ROWS:
            hl_ref[t:t + 1, :] = h_prev
        tiles.append(ht)
    h = jnp.concatenate(tiles, axis=0)
    if seg_len == ROWS:
        carry[...] = h_prev

        @pl.when(c == n_chunks - 1)
        def _():
            hl_ref[...] = h_prev

    g = gate_ref[...]
    gelu = 0.5 * g * (1.0 + jnp.tanh(math.sqrt(2.0 / math.pi) * (g + 0.044715 * (g * g * g))))
    y_ref[...] = (h * gelu).astype(y_ref.dtype)


def _lru_scan(x_src, x_col0, data_block, p_lru, wa_p, wx_p, ba_p, bx_p, sp_p, conv_wb, h0, y_prev, *,
              nb, n_chunks, seg_len, first_valid, mark_pos0, row_block):
    t_rows = p_lru.shape[0]
    width = p_lru.shape[1] // 2
    n_pairs = width // PAIR_W
    nseg = ROWS // seg_len
    conv = conv_wb is not None
    hrows = 1 if seg_len == ROWS else nseg
    sw = LRU_PPS * PAIR_W
    x_blk0 = x_col0 // sw
    h_spec = pl.BlockSpec((None, hrows, sw), lambda b, p, c: (b, 0, p))
    vec_spec = pl.BlockSpec((LRU_PPS, 1, PAIR_W), lambda b, p, c: (p, 0, 0))
    in_specs = [
        pl.BlockSpec((ROWS, sw), lambda b, p, c: (data_block(b, c), x_blk0 + p)),
        pl.BlockSpec((ROWS, sw), lambda b, p, c: (row_block(b, c), p)),
        pl.BlockSpec((LRU_PPS, PAIR_W, PAIR_W), lambda b, p, c: (p, 0, 0)),
        pl.BlockSpec((LRU_PPS, PAIR_W, PAIR_W), lambda b, p, c: (p, 0, 0)),
        vec_spec, vec_spec, vec_spec, h_spec,
    ]
    args = [x_src, p_lru, wa_p, wx_p, ba_p, bx_p, sp_p, h0]
    scratch = [pltpu.VMEM((1, sw), F32)]
    if conv:
        cw, cb = conv_wb
        in_specs += [pl.BlockSpec((CONV_W, sw), lambda b, p, c: (0, p)),
                     pl.BlockSpec((1, sw), lambda b, p, c: (0, p))]
        args += [cw, cb]
        scratch.append(pltpu.VMEM((ROWS + V7X_SUBLANES, sw), F32))
    aliases = _alias_prev(in_specs, args, y_prev)
    return pl.pallas_call(
        functools.partial(_lru_body, seg_len=seg_len, n_chunks=n_chunks, first_valid=first_valid,
                          mark_pos0=mark_pos0, has_prev=y_prev is not None, conv=conv),
        grid=(nb, n_pairs // LRU_PPS, n_chunks),
        in_specs=in_specs,
        out_specs=[pl.BlockSpec((ROWS, sw), lambda b, p, c: (row_block(b, c), p)), h_spec],
        out_shape=[jax.ShapeDtypeStruct((t_rows, width), BF16),
                   jax.ShapeDtypeStruct((nb, hrows, width), F32)],
        scratch_shapes=scratch,
        input_output_aliases=aliases,
        compiler_params=_cparams(("arbitrary", "arbitrary", "arbitrary")),
        name="rglru",
    )(*args)


def _router_body(h_ref, g_ref, wr_ref, br_ref, hn_ref, route_ref, cnt_ref, carry, *, tm):
    i = pl.program_id(0)

    @pl.when(i == 0)
    def _():
        carry[...] = jnp.zeros_like(carry)

    x = h_ref[...]
    hn = x * lax.rsqrt(jnp.mean(x * x, axis=-1, keepdims=True) + EPS) * g_ref[...]
    hn_ref[...] = hn

    h_hi = hn.astype(BF16)
    h_lo = (hn - h_hi.astype(F32)).astype(BF16)
    w = wr_ref[...]
    w_hi = w.astype(BF16)
    w_lo = (w - w_hi.astype(F32)).astype(BF16)
    logits = (jnp.dot(h_hi, w_hi, preferred_element_type=F32)
              + jnp.dot(h_hi, w_lo, preferred_element_type=F32)
              + jnp.dot(h_lo, w_hi, preferred_element_type=F32)) + br_ref[...]

    lane_i = lax.broadcasted_iota(I32, (tm, V7X_LANES), 1)
    lane = lane_i.astype(F32)
    big = float(4 * V7X_LANES)
    neg = -jnp.inf
    is_g = jnp.logical_and(lane_i >= N_EXPERTS, lane_i < N_EXPERTS + MOE_GROUPS)
    glog = jnp.where(is_g, logits, neg)
    gmax = jnp.max(glog, axis=-1, keepdims=True)
    gsel = jnp.min(jnp.where(glog == gmax, lane, big), axis=-1, keepdims=True) - float(N_EXPERTS)
    gsum = jnp.sum(jnp.exp(glog - gmax), axis=-1, keepdims=True)
    pg = 1.0 / gsum
    lo = gsel * float(MOE_PER_GROUP)
    in_grp = jnp.logical_and(lane >= lo, lane < lo + MOE_PER_GROUP)
    elog = jnp.where(in_grp, logits, neg)
    emax = jnp.max(elog, axis=-1, keepdims=True)
    eexp = jnp.exp(elog - emax)
    ep = eexp / jnp.sum(eexp, axis=-1, keepdims=True)
    ep = jnp.where(in_grp, ep, -1.0)
    v1 = jnp.max(ep, axis=-1, keepdims=True)
    i1 = jnp.min(jnp.where(ep == v1, lane, big), axis=-1, keepdims=True)
    ep2 = jnp.where(lane == i1, -1.0, ep)
    v2 = jnp.max(ep2, axis=-1, keepdims=True)
    i2 = jnp.min(jnp.where(ep2 == v2, lane, big), axis=-1, keepdims=True)
    vs = v1 + v2
    w1 = v1 / vs * pg
    w2 = v2 / vs * pg

    oh1 = lane == i1
    oh2 = lane == i2
    oh = jnp.logical_or(oh1, oh2).astype(BF16)
    ti = lax.broadcasted_iota(I32, (tm, tm), 0)
    si = lax.broadcasted_iota(I32, (tm, tm), 1)
    before = (si < ti).astype(BF16)
    tot = jnp.dot(before, oh, preferred_element_type=F32) + carry[...]
    rank1 = jnp.sum(jnp.where(oh1, tot, 0.0), axis=-1, keepdims=True)
    rank2 = jnp.sum(jnp.where(oh2, tot, 0.0), axis=-1, keepdims=True)
    carry[...] = carry[...] + jnp.sum(oh.astype(F32), axis=0, keepdims=True)
    cnt_ref[...] = carry[...]

    out = jnp.where(lane_i == 0, i1, 0.0)
    out = jnp.where(lane_i == 1, i2, out)
    out = jnp.where(lane_i == 2, rank1, out)
    out = jnp.where(lane_i == 3, rank2, out)
    out = jnp.where(lane_i == 4, w1, out)
    out = jnp.where(lane_i == 5, w2, out)
    route_ref[...] = out


def _router(h, g, wr, br, tm):
    t_rows, d = h.shape
    return pl.pallas_call(
        functools.partial(_router_body, tm=tm),
        grid=(t_rows // tm,),
        in_specs=[pl.BlockSpec((tm, d), lambda i: (i, 0)),
                  pl.BlockSpec((1, d), lambda i: (0, 0)),
                  pl.BlockSpec((d, V7X_LANES), lambda i: (0, 0)),
                  pl.BlockSpec((1, V7X_LANES), lambda i: (0, 0))],
        out_specs=[pl.BlockSpec((tm, d), lambda i: (i, 0)),
                   pl.BlockSpec((tm, V7X_LANES), lambda i: (i, 0)),
                   pl.BlockSpec((1, V7X_LANES), lambda i: (0, 0))],
        out_shape=[jax.ShapeDtypeStruct((t_rows, d), F32),
                   jax.ShapeDtypeStruct((t_rows, V7X_LANES), F32),
                   jax.ShapeDtypeStruct((1, V7X_LANES), F32)],
        scratch_shapes=[pltpu.VMEM((1, V7X_LANES), F32)],
        compiler_params=_cparams(("arbitrary",)),
        name="moe_router",
    )(h, g.reshape(1, d), wr, br)


def _expert_body(te_ref, tfirst_ref, ngrp_ref, nt_ref, dst_cur, dst_nxt, hn_hbm, w1_hbm, w3_hbm, w2_hbm,
                 o_hbm, xbuf, ybuf, wb1, wb3, wb2, st13, st2, gsem, ssem, wsem13, wsem2,
                 *, tm, layer, nt_max, t_rows):
    i = pl.program_id(0)
    nt = nt_ref[0]
    slot = lax.rem(i, 2)

    def token_of(d):
        return jnp.where(d >= 2 * t_rows, 0, jnp.where(d >= t_rows, d - t_rows, d))

    def start_gather(dst_ref, s, ngrp):
        def body(r8, carry_):
            for u in range(ROW_UNROLL):
                r = r8 * ROW_UNROLL + u
                pltpu.make_async_copy(hn_hbm.at[pl.ds(token_of(dst_ref[0, 0, r]), 1)],
                                      xbuf.at[s, pl.ds(r, 1)], gsem.at[s]).start()
            return carry_
        lax.fori_loop(0, ngrp, body, 0)

    def wait_gather(s, ngrp):
        def body(r8, carry_):
            pltpu.make_async_copy(hn_hbm.at[pl.ds(0, ROW_UNROLL)], xbuf.at[s, pl.ds(0, ROW_UNROLL)],
                                  gsem.at[s]).wait()
            return carry_
        lax.fori_loop(0, ngrp, body, 0)

    def start_scatter(ngrp):
        def body(r8, carry_):
            for u in range(ROW_UNROLL):
                r = r8 * ROW_UNROLL + u
                pltpu.make_async_copy(ybuf.at[pl.ds(r, 1)], o_hbm.at[pl.ds(dst_cur[0, 0, r], 1)],
                                      ssem.at[0]).start(priority=1)
            return carry_
        lax.fori_loop(0, ngrp, body, 0)

    def wait_scatter(ngrp):
        def body(r8, carry_):
            pltpu.make_async_copy(ybuf.at[pl.ds(0, ROW_UNROLL)], o_hbm.at[pl.ds(0, ROW_UNROLL)],
                                  ssem.at[0]).wait()
            return carry_
        lax.fori_loop(0, ngrp, body, 0)

    @pl.when(i == 0)
    def _():
        xbuf[...] = jnp.zeros_like(xbuf)
        start_gather(dst_cur, 0, ngrp_ref[0])

    @pl.when(i + 1 < nt)
    def _():
        start_gather(dst_nxt, 1 - slot, ngrp_ref[jnp.minimum(i + 1, nt_max - 1)])

    @pl.when(jnp.logical_and(i < nt, tfirst_ref[i] == 1))
    def _():
        e = te_ref[i]
        n13 = 2 * (wb1.shape[0] // W13_ROWS)
        n2 = wb2.shape[0] // W2_ROWS

        def c13(k):
            src = w1_hbm if k % 2 == 0 else w3_hbm
            r0 = (k // 2) * W13_ROWS
            return pltpu.make_async_copy(src.at[layer, e, pl.ds(r0, W13_ROWS)], st13.at[k % W_RING],
                                         wsem13.at[k % W_RING])

        def c2(k):
            return pltpu.make_async_copy(w2_hbm.at[layer, e, pl.ds(k * W2_ROWS, W2_ROWS)],
                                         st2.at[k % W_RING], wsem2.at[k % W_RING])

        for k in range(W_RING):
            c13(k).start(priority=1)
        for k in range(W_RING):
            c2(k).start(priority=1)
        for k in range(n13):
            c13(k).wait()
            dstw = wb1 if k % 2 == 0 else wb3
            r0 = (k // 2) * W13_ROWS
            dstw[r0:r0 + W13_ROWS, :] = st13[k % W_RING].astype(BF16)
            if k + W_RING < n13:
                c13(k + W_RING).start(priority=1)
        for k in range(n2):
            c2(k).wait()
            wb2[k * W2_ROWS:(k + 1) * W2_ROWS, :] = st2[k % W_RING].astype(BF16)
            if k + W_RING < n2:
                c2(k + W_RING).start(priority=1)

    @pl.when(i < nt)
    def _():
        wait_gather(slot, ngrp_ref[i])
        x = xbuf[slot].astype(BF16)
        a = jnp.dot(x, wb1[...], preferred_element_type=F32)
        b = jnp.dot(x, wb3[...], preferred_element_type=F32)
        hdn = (_silu(a) * b).astype(BF16)

        @pl.when(i > 0)
        def _():
            wait_scatter(ngrp_ref[jnp.maximum(i - 1, 0)])

        ybuf[...] = jnp.dot(hdn, wb2[...], preferred_element_type=F32)
        start_scatter(ngrp_ref[i])

    @pl.when(i == nt)
    def _():
        wait_scatter(ngrp_ref[jnp.maximum(i - 1, 0)])

    @pl.when(jnp.logical_and(i == nt_max - 1, i < nt))
    def _():
        wait_scatter(ngrp_ref[i])


def _experts(hn, dst, tile_expert, tile_first, tile_ngrp, n_tiles, w1, w3, w2, *, layer, tm, nt_max):
    t_rows, d = hn.shape
    ff = w1.shape[3]
    smem_spec = lambda f: pl.BlockSpec((1, 1, tm), f, memory_space=pltpu.SMEM)
    grid_spec = pltpu.PrefetchScalarGridSpec(
        num_scalar_prefetch=4,
        grid=(nt_max,),
        in_specs=[
            smem_spec(lambda i, *_: (i, 0, 0)),
            smem_spec(lambda i, *_: (jnp.minimum(i + 1, nt_max - 1), 0, 0)),
            pl.BlockSpec(memory_space=pl.ANY),
            pl.BlockSpec(memory_space=pl.ANY),
            pl.BlockSpec(memory_space=pl.ANY),
            pl.BlockSpec(memory_space=pl.ANY),
        ],
        out_specs=pl.BlockSpec(memory_space=pl.ANY),
        scratch_shapes=[
            pltpu.VMEM((2, tm, d), F32), pltpu.VMEM((tm, d), F32),
            pltpu.VMEM((d, ff), BF16), pltpu.VMEM((d, ff), BF16), pltpu.VMEM((ff, d), BF16),
            pltpu.VMEM((W_RING, W13_ROWS, ff), F32), pltpu.VMEM((W_RING, W2_ROWS, d), F32),
            pltpu.SemaphoreType.DMA((2,)), pltpu.SemaphoreType.DMA((1,)),
            pltpu.SemaphoreType.DMA((W_RING,)), pltpu.SemaphoreType.DMA((W_RING,)),
        ],
    )
    return pl.pallas_call(
        functools.partial(_expert_body, tm=tm, layer=layer, nt_max=nt_max, t_rows=t_rows),
        grid_spec=grid_spec,
        out_shape=jax.ShapeDtypeStruct((2 * t_rows + tm, d), F32),
        compiler_params=_cparams(("arbitrary",)),
        name="moe_experts",
    )(tile_expert, tile_first, tile_ngrp, n_tiles, dst, dst, hn, w1, w3, w2)


def _combine_body(h_ref, y0_ref, y1_ref, route_ref, *rest, with_norm):
    route = route_ref[...]
    h_new = h_ref[...] + route[:, 4:5] * y0_ref[...] + route[:, 5:6] * y1_ref[...]
    if with_norm:
        g_ref, o_ref, n_ref = rest
        o_ref[...] = h_new
        y = h_new * lax.rsqrt(jnp.mean(h_new * h_new, axis=-1, keepdims=True) + EPS)
        n_ref[...] = (y * g_ref[...]).astype(n_ref.dtype)
    else:
        rest[0][...] = h_new


def _combine(h, y2, route, g_next, *, tm):
    t_rows, d = h.shape
    n = t_rows // tm
    row = pl.BlockSpec((tm, d), lambda i: (i, 0))
    in_specs = [row, row,
                pl.BlockSpec((tm, d), lambda i: (n + i, 0)),
                pl.BlockSpec((tm, V7X_LANES), lambda i: (i, 0))]
    args = [h, y2, y2, route]
    with_norm = g_next is not None
    if with_norm:
        in_specs.append(pl.BlockSpec((1, d), lambda i: (0, 0)))
        args.append(g_next.reshape(1, d))
        out_specs = [row, row]
        out_shape = [jax.ShapeDtypeStruct((t_rows, d), F32), jax.ShapeDtypeStruct((t_rows, d), BF16)]
    else:
        out_specs = row
        out_shape = jax.ShapeDtypeStruct((t_rows, d), F32)
    return pl.pallas_call(
        functools.partial(_combine_body, with_norm=with_norm),
        grid=(t_rows // tm,),
        in_specs=in_specs,
        out_specs=out_specs,
        out_shape=out_shape,
        compiler_params=_cparams(("arbitrary",)),
        name="moe_combine",
    )(*args)


def _dispatch_body(cnt_ref, rt_ref, dst_ref, te_ref, tfirst_ref, ngrp_ref, nt_ref, off_ref,
                   *, tm, nt_max, t_rows):
    tm_shift = int(math.log2(tm))
    grp_shift = int(math.log2(ROW_UNROLL))

    def per_expert(e, tile0):
        c = cnt_ref[e]
        n = lax.shift_right_logical(c + (tm - 1), tm_shift)
        off_ref[e] = tile0 * tm

        def per_tile(k, carry_):
            i = tile0 + k
            te_ref[i] = e
            tfirst_ref[i] = jnp.where(k == 0, 1, 0)
            nval = jnp.minimum(c - k * tm, tm)
            ngrp_ref[i] = lax.shift_right_logical(nval + (ROW_UNROLL - 1), grp_shift)
            return carry_
        lax.fori_loop(0, n, per_tile, 0)
        for u in range(ROW_UNROLL - 1):
            r = c + u
            dst_ref[tile0 * tm + r] = 2 * t_rows + jnp.bitwise_and(r, tm - 1)
        return tile0 + n
    nt = lax.fori_loop(0, N_EXPERTS, per_expert, 0)
    nt_ref[0] = nt

    def dead(i, carry_):
        te_ref[i] = 0
        tfirst_ref[i] = 0
        ngrp_ref[i] = 0
        return carry_
    lax.fori_loop(nt, nt_max, dead, 0)

    def per_token(t, carry_):
        dst_ref[off_ref[rt_ref[t]] + rt_ref[2 * t_rows + t]] = t
        dst_ref[off_ref[rt_ref[t_rows + t]] + rt_ref[3 * t_rows + t]] = t_rows + t
        return carry_
    lax.fori_loop(0, t_rows, per_token, 0, unroll=ROW_UNROLL)


def _dispatch(cnt, rt, *, tm, nt_max, t_rows):
    smem = pl.BlockSpec(memory_space=pltpu.SMEM)
    return pl.pallas_call(
        functools.partial(_dispatch_body, tm=tm, nt_max=nt_max, t_rows=t_rows),
        in_specs=[smem, smem],
        out_specs=[smem] * 5,
        out_shape=[jax.ShapeDtypeStruct((nt_max * tm,), I32), jax.ShapeDtypeStruct((nt_max,), I32),
                   jax.ShapeDtypeStruct((nt_max,), I32), jax.ShapeDtypeStruct((nt_max,), I32),
                   jax.ShapeDtypeStruct((1,), I32)],
        scratch_shapes=[pltpu.SMEM((N_EXPERTS,), I32)],
        name="moe_dispatch",
    )(cnt, rt)


def _moe(h, layer, g, w_rg, b_rg, w_re, b_re, w1, w3, w2, g_next, *, tm_tok, tm_e):
    t_rows, d = h.shape
    pad = V7X_LANES - N_EXPERTS - MOE_GROUPS
    wr = jnp.concatenate([w_re[layer], w_rg[layer], jnp.zeros((d, pad), F32)], axis=1)
    br = jnp.concatenate([b_re[layer], b_rg[layer], jnp.zeros((pad,), F32)]).reshape(1, V7X_LANES)
    hn, route, counts = _router(h, g, wr, br, tm_tok)

    nt_max = (2 * t_rows) // tm_e + N_EXPERTS
    cnt = counts[0, :N_EXPERTS].astype(I32)
    rt = route[:, :4].astype(I32).T.reshape(4 * t_rows)
    dst, te, tfirst, ngrp, n_tiles = _dispatch(cnt, rt, tm=tm_e, nt_max=nt_max, t_rows=t_rows)

    y2 = _experts(hn, dst.reshape(nt_max, 1, tm_e), te, tfirst, ngrp, n_tiles, w1, w3, w2,
                  layer=layer, tm=tm_e, nt_max=nt_max)
    return _combine(h, y2, route, g_next, tm=_pick(t_rows, (256, 128)))


def _conv_sample(u, buf, w, b):
    full = jnp.concatenate([buf, u], axis=1)
    l = u.shape[1]
    y = b + w[0] * full[:, 0:l]
    for t in range(1, CONV_W):
        y = y + w[t] * full[:, t:t + l]
    return y, full[:, -(CONV_W - 1):]


def _pair_blocks(w):
    nblk, bw, _ = w.shape
    w = w.reshape(nblk // 2, 2, bw, bw)
    z = jnp.zeros((nblk // 2, bw, bw), w.dtype)
    top = jnp.concatenate([w[:, 0], z], axis=2)
    bot = jnp.concatenate([z, w[:, 1]], axis=2)
    return jnp.concatenate([top, bot], axis=1).astype(BF16)


def kernel(x_prompt, x_sample, state_ssd_conv, state_ssd, state_ret, state_lru_conv, state_lru, meta_tokens, norm_mix, norm_ffn, norm_final, w_in0, ssd_conv_w, ssd_conv_b, ssd_dt_bias, ssd_a_log, ssd_d, ssd_norm_g, w_out0, w_in1, lru_conv_w, lru_conv_b, lru_wa, lru_ba, lru_wx, lru_bx, lru_lambda, w_out1, moe_w_rg, moe_b_rg, moe_w_re, moe_b_re, moe_w1, moe_w3, moe_w2):
    nb, seq, d = x_prompt.shape
    nbs, ls, _ = x_sample.shape
    assert seq % ROWS == 0 and ROWS % ls == 0 and (nbs * ls) % ROWS == 0 and ls >= CONV_W - 1
    n_xblk = seq // ROWS
    blk_p = n_xblk + 1
    lp_pad = blk_p * ROWS
    tp = nb * lp_pad
    ts = nbs * ls
    t_all = tp + ts
    nbs_blk = ts // ROWS

    def rb_prompt(b, c):
        return b * blk_p + lax.rem(c + n_xblk, blk_p)

    def rb_sample(b, c):
        return nb * blk_p + b

    def rb_own(b, c):
        return b

    def _last_prompt_rows(p, c0, c1):
        return jnp.stack([p[b * lp_pad + seq - (CONV_W - 1):b * lp_pad + seq, c0:c1] for b in range(nb)])

    prompt_kw = dict(nb=nb, n_chunks=blk_p, seg_len=ROWS, first_valid=N_META, row_block=rb_prompt)
    sample_kw = dict(nb=nbs_blk, n_chunks=1, seg_len=ls, first_valid=ROWS, row_block=rb_sample)

    h, hn = _embed_norm(x_prompt, x_sample, meta_tokens, norm_mix[0])

    tm_tok = _pick(t_all, (512, 256, 128))
    tm_mm = _pick(t_all, (1216, 608, 512, 256, 128))

    d_inner = SSD_GROUPS * GROUP_W
    conv_dim = d_inner + 2 * SSD_GROUPS * SSD_STATE
    n_heads = SSD_GROUPS * SSD_HPG
    c_zx = d_inner + conv_dim
    c_qkvg = w_in0.shape[1] - c_zx - n_heads
    w_in0_t = w_in0.T
    p1 = _matmul_t(hn, w_in0_t, 0, c_zx, tm_mm, 512)
    pdt = _matmul_t(hn, w_in0_t, c_zx, n_heads, tm_mm, n_heads)
    p3 = _matmul_t(hn, w_in0_t, c_zx + n_heads, c_qkvg, tm_mm, 512)

    dt = jax.nn.softplus(pdt + ssd_dt_bias)
    la = dt * (-jnp.exp(ssd_a_log))
    dt3 = dt.reshape(t_all, SSD_GROUPS, SSD_HPG)
    la3 = la.reshape(t_all, SSD_GROUPS, SSD_HPG)
    dtc = dt3.transpose(1, 0, 2)[:, None]
    lac = la3.transpose(1, 0, 2)[:, None]
    lar = la3.transpose(1, 2, 0)[:, None]
    dx = jnp.repeat(ssd_d, SSD_HEAD_DIM).reshape(SSD_GROUPS, 1, GROUP_W)
    ng = ssd_norm_g.reshape(SSD_GROUPS, 1, GROUP_W)

    xbc_s, conv_s = _conv_sample(p1[tp:, d_inner:].reshape(nbs, ls, conv_dim), state_ssd_conv,
                                 ssd_conv_w, ssd_conv_b)
    xbc_s = jax.nn.silu(xbc_s).reshape(ts, conv_dim)
    conv_p = _last_prompt_rows(p1, d_inner, c_zx)

    mix_w = d_inner + RET_HEADS * RET_DIM
    ssd_common = (p1, dtc, lac, lar, dx, ng)
    mix, ssd_p = _ssd_scan(p1, d_inner, rb_prompt, *ssd_common,
                           (ssd_conv_w, ssd_conv_b.reshape(1, conv_dim)), None, None,
                           out_width=mix_w, **prompt_kw)
    mix, ssd_s = _ssd_scan(xbc_s, 0, rb_own, *ssd_common, None, jnp.swapaxes(state_ssd, 2, 3), mix,
                           out_width=mix_w, **sample_kw)

    half = RET_DIM // 2
    inv = 1.0 / (ROPE_BASE ** (jnp.arange(half, dtype=F32) / half))
    pos_p = jnp.concatenate([N_META + jnp.arange(seq, dtype=I32), jnp.arange(N_META, dtype=I32),
                             jnp.zeros((lp_pad - seq - N_META,), I32)])
    pos_s = PAST_LEN + (jnp.arange(ROWS, dtype=I32) % ls)
    ang = jnp.concatenate([pos_p, pos_s]).astype(F32)[:, None] * inv[None, :]
    cos, sin = jnp.cos(ang), jnp.sin(ang)
    log_gamma = jnp.log1p(-jnp.exp2(-5.0 - jnp.arange(RET_HEADS, dtype=F32)))
    lg = jnp.broadcast_to(log_gamma[:, None, None], (RET_HEADS, 1, V7X_LANES))
    ret_kw = dict(out_width=mix_w, out_col0=d_inner)
    mix, ret_p = _ret_scan(p3, cos, sin, lg, None, mix, hb=RET_HB_PROMPT,
                           cs_block=lambda b, c: lax.rem(c + n_xblk, blk_p), **ret_kw, **prompt_kw)
    mix, ret_s = _ret_scan(p3, cos, sin, lg, state_ret, mix, hb=RET_HB_SAMPLE, cs_block=lambda b, c: blk_p,
                           **ret_kw, **sample_kw)
    tm_o = _pick(t_all, (608, 512, 256, 128))
    h = _matmul(mix, w_out0.astype(BF16), d, tm_o, 512, res=h)
    moe_kw = dict(tm_tok=tm_tok, tm_e=min(MOE_TM, tm_tok))
    moe_w = (moe_w_rg, moe_b_rg, moe_w_re, moe_b_re, moe_w1, moe_w3, moe_w2)
    h, hn = _moe(h, 0, norm_ffn[0], *moe_w, norm_mix[1], **moe_kw)

    width = lru_lambda.shape[0]
    p_lru = _matmul(hn, w_in1, 2 * width, tm_mm, 512)
    xc_s, lconv_s = _conv_sample(p_lru[tp:, width:].reshape(nbs, ls, width), state_lru_conv,
                                 lru_conv_w, lru_conv_b)
    xc_s = xc_s.reshape(ts, width)
    lconv_p = _last_prompt_rows(p_lru, width, 2 * width)
    n_pairs = width // PAIR_W
    lru_common = (p_lru, _pair_blocks(lru_wa), _pair_blocks(lru_wx), lru_ba.reshape(n_pairs, 1, PAIR_W),
                  lru_bx.reshape(n_pairs, 1, PAIR_W),
                  jax.nn.softplus(-lru_lambda).reshape(n_pairs, 1, PAIR_W))
    y_lru, lru_p = _lru_scan(p_lru, width, rb_prompt, *lru_common,
                             (lru_conv_w, lru_conv_b.reshape(1, width)), jnp.zeros((nb, 1, width), F32),
                             None, mark_pos0=True, **prompt_kw)
    y_lru, lru_s = _lru_scan(xc_s, 0, rb_own, *lru_common, None,
                             state_lru.reshape(nbs_blk, ROWS // ls, width), y_lru, mark_pos0=False,
                             **sample_kw)
    h = _matmul(y_lru, w_out1, d, tm_o, 512, res=h)
    h = _moe(h, 1, norm_ffn[1], *moe_w, None, **moe_kw)

    y_prompt = _rmsnorm(h, norm_final, F32, ROWS, n_out_blocks=nb * n_xblk,
                        in_block=lambda i: (i // n_xblk) * blk_p + lax.rem(i, n_xblk))
    y_sample = _rmsnorm(h, norm_final, F32, ROWS, n_out_blocks=nbs_blk, in_block=lambda i: nb * blk_p + i)
    return (y_prompt.reshape(nb, seq, d), y_sample.reshape(nbs, ls, d),
            conv_p, jnp.swapaxes(ssd_p, 2, 3), ret_p, lconv_p, lru_p.reshape(nb, width),
            conv_s, jnp.swapaxes(ssd_s, 2, 3), ret_s, lconv_s, lru_s.reshape(nbs, width))
```

```python
import functools
import math

import jax
import jax.numpy as jnp
from jax import lax
from jax.experimental import pallas as pl
from jax.experimental.pallas import tpu as pltpu

F32, BF16, I32 = jnp.float32, jnp.bfloat16, jnp.int32

N_META = 16
CONV_W = 4
EPS = 1e-6
F32_TINY = 1.1754944e-38
PAST_LEN = 16384
SSD_HEAD_DIM = 64
SSD_GROUPS = 8
SSD_HPG = 8
SSD_STATE = 128
RET_HEADS = 16
RET_DIM = 256
RET_HB_PROMPT = 8
RET_HB_SAMPLE = 2
ROPE_BASE = 10000.0
LRU_C = 8.0
MOE_GROUPS = 4
MOE_PER_GROUP = 8
N_EXPERTS = MOE_GROUPS * MOE_PER_GROUP

V7X_LANES = 128
V7X_SUBLANES = 8
V7X_VMEM_LIMIT = 56 * 1024 * 1024
ROWS = 128
GROUP_W = SSD_HPG * SSD_HEAD_DIM
SSD_GPS_PROMPT = 4
SSD_GPS_SAMPLE = 2
XBC_W = GROUP_W + 2 * SSD_STATE
PAIR_W = 2 * 320
LRU_PPS = 4
MOE_TM = 512
W_RING = 4
W13_ROWS = 512
W2_ROWS = 64
ROW_UNROLL = 8


def _cparams(sem):
    return pltpu.CompilerParams(dimension_semantics=sem, vmem_limit_bytes=V7X_VMEM_LIMIT)


def _pick(n, cands):
    for c in cands:
        if n % c == 0:
            return c
    raise ValueError(f"no tile for {n} in {cands}")


def _sigmoid(x):
    return 0.5 * jnp.tanh(0.5 * x) + 0.5


def _silu(x):
    return x * _sigmoid(x)


def _rmsnorm_body(x_ref, g_ref, o_ref):
    x = x_ref[...]
    y = x * lax.rsqrt(jnp.mean(x * x, axis=-1, keepdims=True) + EPS)
    o_ref[...] = (y * g_ref[...]).astype(o_ref.dtype)


def _rmsnorm(x, g, out_dtype, tm, n_out_blocks=None, in_block=None):
    m, d = x.shape
    n_blocks = m // tm if n_out_blocks is None else n_out_blocks
    in_map = (lambda i: (i, 0)) if in_block is None else (lambda i: (in_block(i), 0))
    return pl.pallas_call(
        _rmsnorm_body,
        grid=(n_blocks,),
        in_specs=[pl.BlockSpec((tm, d), in_map), pl.BlockSpec((1, d), lambda i: (0, 0))],
        out_specs=pl.BlockSpec((tm, d), lambda i: (i, 0)),
        out_shape=jax.ShapeDtypeStruct((n_blocks * tm, d), out_dtype),
        compiler_params=_cparams(("arbitrary",)),
        name="rmsnorm",
    )(x, g.reshape(1, d))


def _embed_body(xp_ref, xs_ref, meta_ref, g_ref, h_ref, hn_ref, *, nb, blk_p, n_xblk):
    i = pl.program_id(0)
    k = lax.rem(i, blk_p)
    is_prompt = i < nb * blk_p

    def emit(x):
        h_ref[...] = x
        y = x * lax.rsqrt(jnp.mean(x * x, axis=-1, keepdims=True) + EPS)
        hn_ref[...] = (y * g_ref[...]).astype(hn_ref.dtype)

    @pl.when(jnp.logical_and(is_prompt, k < n_xblk))
    def _():
        emit(xp_ref[...])

    @pl.when(jnp.logical_and(is_prompt, k == n_xblk))
    def _():
        meta = meta_ref[...]
        emit(jnp.concatenate([meta, jnp.zeros((ROWS - meta.shape[0], meta.shape[1]), F32)], axis=0))

    @pl.when(jnp.logical_not(is_prompt))
    def _():
        emit(xs_ref[...])


def _embed_norm(x_prompt, x_sample, meta, g):
    nb, seq, d = x_prompt.shape
    ts = x_sample.shape[0] * x_sample.shape[1]
    n_xblk = seq // ROWS
    blk_p = n_xblk + 1
    n_blocks = nb * blk_p + ts // ROWS
    row = pl.BlockSpec((ROWS, d), lambda i: (i, 0))
    return pl.pallas_call(
        functools.partial(_embed_body, nb=nb, blk_p=blk_p, n_xblk=n_xblk),
        grid=(n_blocks,),
        in_specs=[
            pl.BlockSpec((ROWS, d), lambda i: (jnp.minimum(i // blk_p, nb - 1) * n_xblk
                                               + jnp.minimum(lax.rem(i, blk_p), n_xblk - 1), 0)),
            pl.BlockSpec((ROWS, d), lambda i: (jnp.clip(i - nb * blk_p, 0, ts // ROWS - 1), 0)),
            pl.BlockSpec(meta.shape, lambda i: (0, 0)),
            pl.BlockSpec((1, d), lambda i: (0, 0)),
        ],
        out_specs=[row, row],
        out_shape=[jax.ShapeDtypeStruct((n_blocks * ROWS, d), F32),
                   jax.ShapeDtypeStruct((n_blocks * ROWS, d), BF16)],
        compiler_params=_cparams(("arbitrary",)),
        name="embed_norm",
    )(x_prompt.reshape(nb * seq, d), x_sample.reshape(ts, d), meta, g.reshape(1, d))


def _mm_body(*refs, has_res, cast_b):
    a_ref, b_ref = refs[0], refs[1]
    r_ref = refs[2] if has_res else None
    o_ref = refs[3] if has_res else refs[2]
    if cast_b:
        bs_ref = refs[-1]

        @pl.when(pl.program_id(1) == 0)
        def _():
            bs_ref[...] = b_ref[...].astype(BF16)

        b = bs_ref[...]
    else:
        b = b_ref[...]
    acc = jnp.dot(a_ref[...], b, preferred_element_type=F32)
    if has_res:
        acc = acc + r_ref[...]
    o_ref[...] = acc


def _matmul(a, b, n_cols, tm, tn, res=None):
    m, k = a.shape
    cast_b = b.dtype != BF16
    in_specs = [pl.BlockSpec((tm, k), lambda j, i: (i, 0)),
                pl.BlockSpec((k, tn), lambda j, i: (0, j))]
    args = [a, b]
    if res is not None:
        in_specs.append(pl.BlockSpec((tm, tn), lambda j, i: (i, j)))
        args.append(res)
    return pl.pallas_call(
        functools.partial(_mm_body, has_res=res is not None, cast_b=cast_b),
        grid=(n_cols // tn, m // tm),
        in_specs=in_specs,
        out_specs=pl.BlockSpec((tm, tn), lambda j, i: (i, j)),
        out_shape=jax.ShapeDtypeStruct((m, n_cols), F32),
        scratch_shapes=[pltpu.VMEM((k, tn), BF16)] if cast_b else [],
        compiler_params=_cparams(("arbitrary", "arbitrary")),
        name="proj",
    )(*args)


def _mm_t_body(a_ref, bt_ref, o_ref, bs_ref):
    @pl.when(pl.program_id(1) == 0)
    def _():
        bs_ref[...] = bt_ref[...].astype(BF16)

    o_ref[...] = lax.dot_general(a_ref[...], bs_ref[...], (((1,), (1,)), ((), ())),
                                 preferred_element_type=F32)


def _matmul_t(a, bt, row0, n_cols, tm, tn):
    m, k = a.shape
    return pl.pallas_call(
        _mm_t_body,
        grid=(n_cols // tn, m // tm),
        in_specs=[pl.BlockSpec((tm, k), lambda j, i: (i, 0)),
                  pl.BlockSpec((pl.Element(tn), pl.Element(k)),
                               lambda j, i: (pl.multiple_of(row0 + j * tn, math.gcd(row0, tn)), 0))],
        out_specs=pl.BlockSpec((tm, tn), lambda j, i: (i, j)),
        out_shape=jax.ShapeDtypeStruct((m, n_cols), F32),
        scratch_shapes=[pltpu.VMEM((tn, k), BF16)],
        compiler_params=_cparams(("arbitrary", "arbitrary")),
        name="proj_t",
    )(a, bt)


def _split3(a):
    hi = a.astype(BF16)
    r1 = a - hi.astype(F32)
    mid = r1.astype(BF16)
    lo = (r1 - mid.astype(F32)).astype(BF16)
    return hi, mid, lo


def _dot_a01(a, m01, terms=3):
    hi, mid, lo = _split3(a)
    out = jnp.dot(hi, m01, preferred_element_type=F32) + jnp.dot(mid, m01, preferred_element_type=F32)
    if terms == 3:
        out = out + jnp.dot(lo, m01, preferred_element_type=F32)
    return out


def _dot_01a(m01, a):
    hi, mid, lo = _split3(a)
    return (jnp.dot(m01, hi, preferred_element_type=F32)
            + jnp.dot(m01, mid, preferred_element_type=F32)
            + jnp.dot(m01, lo, preferred_element_type=F32))


def _seg_masks(seg_len):
    ti = lax.broadcasted_iota(I32, (ROWS, ROWS), 0)
    si = lax.broadcasted_iota(I32, (ROWS, ROWS), 1)
    if seg_len == ROWS:
        causal = si <= ti
        causal_t = ti <= si
        sel = si == ROWS - 1
    else:
        shift = int(math.log2(seg_len))
        tseg = lax.shift_right_logical(ti, shift)
        sseg = lax.shift_right_logical(si, shift)
        same = tseg == sseg
        causal = jnp.logical_and(si <= ti, same)
        causal_t = jnp.logical_and(ti <= si, same)
        sel = si == lax.shift_left(tseg, shift) + (seg_len - 1)
    return causal, causal_t, sel


def _conv_block(u, ext_scr, w, b, c, first_valid):
    halo = V7X_SUBLANES

    @pl.when(c == 0)
    def _():
        ext_scr[0:halo, :] = jnp.zeros((halo, u.shape[1]), F32)

    ext_scr[halo:halo + ROWS, :] = u
    y = b + w[0:1] * ext_scr[pl.ds(halo - 3, ROWS), :]
    y = y + w[1:2] * ext_scr[pl.ds(halo - 2, ROWS), :]
    y = y + w[2:3] * ext_scr[pl.ds(halo - 1, ROWS), :]
    y = y + w[3:4] * u

    @pl.when(c == 0)
    def _():
        ext_scr[0:halo, :] = ext_scr[first_valid:first_valid + halo, :]

    @pl.when(c != 0)
    def _():
        ext_scr[0:halo, :] = ext_scr[ROWS:ROWS + halo, :]

    return y


def _alias_prev(in_specs, args, y_prev):
    if y_prev is None:
        return {}
    in_specs.append(pl.BlockSpec(memory_space=pl.ANY))
    args.append(y_prev)
    return {len(args) - 1: 0}


def _ssd_body(xs_ref, bm_ref, cm_ref, z_ref, dtc_ref, lac_ref, lar_ref, dx_ref, ng_ref, *rest,
              seg_len, n_chunks, first_valid, has_s0, has_prev, conv):
    rest = list(rest)
    cw = [rest.pop(0) for _ in range(6)] if conv else None
    s0_ref = rest.pop(0) if has_s0 else None
    if has_prev:
        rest.pop(0)
    y_ref, sf_ref, s_scr = rest[:3]
    nseg = ROWS // seg_len
    ngr = xs_ref.shape[1] // GROUP_W
    c = pl.program_id(2)

    @pl.when(c == 0)
    def _():
        for i in range(nseg):
            for gi in range(ngr):
                if has_s0:
                    s_scr[i, gi] = jnp.concatenate([s0_ref[i, gi * SSD_HPG + j] for j in range(SSD_HPG)],
                                                   axis=0)
                else:
                    s_scr[i, gi] = jnp.zeros((GROUP_W, SSD_STATE), F32)

    if conv:
        u = jnp.concatenate([xs_ref[...], bm_ref[...], cm_ref[...]], axis=1)
        w = jnp.concatenate([cw[0][...], cw[1][...], cw[2][...]], axis=1)
        b = jnp.concatenate([cw[3][...], cw[4][...], cw[5][...]], axis=1)
        act = _silu(_conv_block(u, rest[3], w, b, c, first_valid))
        xs_all = act[:, :ngr * GROUP_W]
        bm_all = act[:, ngr * GROUP_W:ngr * (GROUP_W + SSD_STATE)]
        cm_all = act[:, ngr * (GROUP_W + SSD_STATE):]
    else:
        xs_all, bm_all, cm_all = xs_ref[...], bm_ref[...], cm_ref[...]

    nvalid = jnp.where(c == 0, first_valid, ROWS) if first_valid != ROWS else ROWS
    row_c = lax.broadcasted_iota(I32, (ROWS, 1), 0)
    row_r = lax.broadcasted_iota(I32, (1, ROWS), 1)
    valid = row_c < nvalid
    valid_r = row_r < nvalid
    causal, causal_t, sel = _seg_masks(seg_len)
    tril = causal.astype(BF16)
    triu = causal_t.astype(BF16)
    e_i = lax.broadcasted_iota(I32, (SSD_HPG, GROUP_W), 0)
    e_l = lax.shift_right_logical(lax.broadcasted_iota(I32, (SSD_HPG, GROUP_W), 1), 6)
    expand = (e_i == e_l).astype(BF16)
    lane_head = lax.shift_right_logical(lax.broadcasted_iota(I32, (1, GROUP_W), 1), 6)

    for gi in range(ngr):
        gcols = slice(gi * GROUP_W, (gi + 1) * GROUP_W)
        ncols = slice(gi * SSD_STATE, (gi + 1) * SSD_STATE)
        xs = jnp.where(valid, xs_all[:, gcols], 0.0)
        bm = jnp.where(valid, bm_all[:, ncols], 0.0).astype(BF16)
        cm = jnp.where(valid, cm_all[:, ncols], 0.0).astype(BF16)
        z = jnp.where(valid, z_ref[:, gcols], 0.0)
        dt = jnp.where(valid, dtc_ref[gi, 0], 0.0)
        la = jnp.where(valid, lac_ref[gi, 0], 0.0)
        la_r = jnp.where(valid_r, lar_ref[gi, 0], 0.0)

        cum = _dot_01a(tril, la)
        cum_r = _dot_a01(la_r, triu)
        if seg_len == ROWS:
            cum_last = jnp.broadcast_to(cum[ROWS - 1:ROWS, :], (ROWS, SSD_HPG))
        else:
            cum_last = _dot_01a(sel.astype(BF16), cum)
        ecum_x = _dot_a01(jnp.exp(cum), expand)
        v = xs * _dot_a01(dt, expand, terms=2)
        kvw = (v * _dot_a01(jnp.exp(cum_last - cum), expand, terms=2)).astype(BF16)

        scores = lax.dot_general(cm, bm, (((1,), (1,)), ((), ())), preferred_element_type=F32)
        y = jnp.zeros((ROWS, GROUP_W), F32)
        for j in range(SSD_HPG):
            dj = jnp.exp(jnp.where(causal, cum[:, j:j + 1] - cum_r[j:j + 1, :], -jnp.inf))
            pj = (scores * dj).astype(BF16)
            vj = jnp.where(lane_head == j, v, 0.0).astype(BF16)
            y = y + jnp.dot(pj, vj, preferred_element_type=F32)

        y_inter = []
        for i in range(nseg):
            r0 = i * seg_len
            r_last = r0 + seg_len - 1
            s_old = s_scr[i, gi]
            y_inter.append(lax.dot_general(cm[r0:r0 + seg_len], s_old.astype(BF16),
                                           (((1,), (1,)), ((), ())), preferred_element_type=F32))
            cs = lax.dot_general(kvw[r0:r0 + seg_len], bm[r0:r0 + seg_len], (((0,), (0,)), ((), ())),
                                 preferred_element_type=F32)
            e_last = jnp.exp(cum_r[:, r_last:r_last + 1])
            dec = jnp.concatenate([jnp.broadcast_to(e_last[j:j + 1, :], (SSD_HEAD_DIM, SSD_STATE))
                                   for j in range(SSD_HPG)], axis=0)
            s_scr[i, gi] = dec * s_old + cs
        y_inter = y_inter[0] if nseg == 1 else jnp.concatenate(y_inter, axis=0)
        y = y + y_inter * ecum_x
        y = y + dx_ref[gi] * xs
        y = y * _silu(z)
        y = y * lax.rsqrt(jnp.mean(y * y, axis=-1, keepdims=True) + EPS) * ng_ref[gi]
        y_ref[:, gcols] = y.astype(y_ref.dtype)

    @pl.when(c == n_chunks - 1)
    def _():
        for i in range(nseg):
            for gi in range(ngr):
                s_fin = s_scr[i, gi]
                for j in range(SSD_HPG):
                    sf_ref[i, gi * SSD_HPG + j] = s_fin[j * SSD_HEAD_DIM:(j + 1) * SSD_HEAD_DIM, :]


def _ssd_scan(xbc_src, xbc_col0, data_block, p1, dtc, lac, lar, dx, ng, conv_wb, s0, y_prev, *, gps,
              out_width, nb, n_chunks, seg_len, first_valid, row_block):
    t_rows = p1.shape[0]
    nseg = ROWS // seg_len
    conv = conv_wb is not None
    d_inner = SSD_GROUPS * GROUP_W
    gw = gps * GROUP_W
    sw = gps * SSD_STATE
    x_blk0 = xbc_col0 // gw
    b_blk0 = (xbc_col0 + d_inner) // sw
    c_blk0 = b_blk0 + SSD_GROUPS // gps
    aux = lambda shape, f: pl.BlockSpec((gps,) + shape, f)
    in_specs = [
        pl.BlockSpec((ROWS, gw), lambda b, g, c: (data_block(b, c), x_blk0 + g)),
        pl.BlockSpec((ROWS, sw), lambda b, g, c: (data_block(b, c), b_blk0 + g)),
        pl.BlockSpec((ROWS, sw), lambda b, g, c: (data_block(b, c), c_blk0 + g)),
        pl.BlockSpec((ROWS, gw), lambda b, g, c: (row_block(b, c), g)),
        aux((1, ROWS, SSD_HPG), lambda b, g, c: (g, 0, row_block(b, c), 0)),
        aux((1, ROWS, SSD_HPG), lambda b, g, c: (g, 0, row_block(b, c), 0)),
        aux((1, SSD_HPG, ROWS), lambda b, g, c: (g, 0, 0, row_block(b, c))),
        aux((1, GROUP_W), lambda b, g, c: (g, 0, 0)),
        aux((1, GROUP_W), lambda b, g, c: (g, 0, 0)),
    ]
    args = [xbc_src, xbc_src, xbc_src, p1, dtc, lac, lar, dx, ng]
    scratch = [pltpu.VMEM((nseg, gps, GROUP_W, SSD_STATE), F32)]
    if conv:
        cw, cb = conv_wb
        wb_blk0 = d_inner // sw
        for arr, rows in ((cw, CONV_W), (cb, 1)):
            in_specs += [pl.BlockSpec((rows, gw), lambda b, g, c: (0, g)),
                         pl.BlockSpec((rows, sw), lambda b, g, c: (0, wb_blk0 + g)),
                         pl.BlockSpec((rows, sw), lambda b, g, c: (0, wb_blk0 + SSD_GROUPS // gps + g))]
            args += [arr, arr, arr]
        scratch.append(pltpu.VMEM((ROWS + V7X_SUBLANES, gps * XBC_W), F32))
    state_spec = pl.BlockSpec((nseg, gps * SSD_HPG, SSD_HEAD_DIM, SSD_STATE), lambda b, g, c: (b, g, 0, 0))
    if s0 is not None:
        in_specs.append(state_spec)
        args.append(s0)
    aliases = _alias_prev(in_specs, args, y_prev)
    n_heads = SSD_GROUPS * SSD_HPG
    return pl.pallas_call(
        functools.partial(_ssd_body, seg_len=seg_len, n_chunks=n_chunks, first_valid=first_valid,
                          has_s0=s0 is not None, has_prev=y_prev is not None, conv=conv),
        grid=(nb, SSD_GROUPS // gps, n_chunks),
        in_specs=in_specs,
        out_specs=[pl.BlockSpec((ROWS, gw), lambda b, g, c: (row_block(b, c), g)), state_spec],
        out_shape=[jax.ShapeDtypeStruct((t_rows, out_width), BF16),
                   jax.ShapeDtypeStruct((nb * nseg, n_heads, SSD_HEAD_DIM, SSD_STATE), F32)],
        scratch_shapes=scratch,
        input_output_aliases=aliases,
        compiler_params=_cparams(("arbitrary", "arbitrary", "arbitrary")),
        name="ssd_scan",
    )(*args)


def _ret_body(q_ref, k_ref, v_ref, g_ref, cos_ref, sin_ref, lg_ref, *rest,
              seg_len, n_chunks, first_valid, has_s0, has_prev):
    s0_ref = rest[0] if has_s0 else None
    y_ref, sf_ref, s_scr = rest[int(has_s0) + int(has_prev):]
    nseg = ROWS // seg_len
    half = RET_DIM // 2
    hb = q_ref.shape[1] // RET_DIM
    c = pl.program_id(2)

    @pl.when(c == 0)
    def _():
        for i in range(nseg):
            for hh in range(hb):
                if has_s0:
                    s_scr[i, hh] = s0_ref[i, hh]
                else:
                    s_scr[i, hh] = jnp.zeros((RET_DIM, RET_DIM), F32)

    nvalid = jnp.where(c == 0, first_valid, ROWS) if first_valid != ROWS else ROWS
    row_c = lax.broadcasted_iota(I32, (ROWS, 1), 0)
    row_r = lax.broadcasted_iota(I32, (1, ROWS), 1)
    valid = row_c < nvalid
    causal, _, _ = _seg_masks(seg_len)
    pos_c = jnp.minimum(jnp.bitwise_and(row_c, seg_len - 1) + 1, nvalid).astype(F32)
    pos_r = jnp.minimum(jnp.bitwise_and(row_r, seg_len - 1) + 1, nvalid).astype(F32)
    last = jnp.minimum(seg_len, nvalid).astype(F32) if first_valid != ROWS else float(seg_len)
    cos = cos_ref[...]
    sin = sin_ref[...]

    def rot(x):
        x1, x2 = x[:, :half], x[:, half:]
        return jnp.concatenate([x1 * cos - x2 * sin, x1 * sin + x2 * cos], axis=1)

    for hh in range(hb):
        cols = slice(hh * RET_DIM, (hh + 1) * RET_DIM)
        lg_c = lg_ref[hh][:, 0:1]
        cum = pos_c * lg_c
        cum_r = pos_r * lg_c
        cum_last = last * lg_c
        q = rot(jnp.where(valid, q_ref[:, cols], 0.0)).astype(BF16)
        k = (rot(jnp.where(valid, k_ref[:, cols], 0.0)) * (RET_DIM ** -0.5)).astype(BF16)
        v = jnp.where(valid, v_ref[:, cols], 0.0)
        gate = jnp.where(valid, g_ref[:, cols], 0.0)

        decay = jnp.exp(jnp.where(causal, cum - cum_r, -jnp.inf))
        scores = lax.dot_general(q, k, (((1,), (1,)), ((), ())), preferred_element_type=F32)
        y = jnp.dot((scores * decay).astype(BF16), v.astype(BF16), preferred_element_type=F32)
        kvw = (v * jnp.exp(cum_last - cum)).astype(BF16)
        ecum = jnp.exp(cum)
        e_last = jnp.exp(cum_last)

        y_inter = []
        for i in range(nseg):
            r0 = i * seg_len
            s_old = s_scr[i, hh]
            y_inter.append(jnp.dot(q[r0:r0 + seg_len], s_old.astype(BF16), preferred_element_type=F32))
            cs = lax.dot_general(k[r0:r0 + seg_len], kvw[r0:r0 + seg_len], (((0,), (0,)), ((), ())),
                                 preferred_element_type=F32)
            s_scr[i, hh] = e_last * s_old + cs
        y_inter = y_inter[0] if nseg == 1 else jnp.concatenate(y_inter, axis=0)
        y = y + y_inter * ecum
        y = y * lax.rsqrt(jnp.mean(y * y, axis=-1, keepdims=True) + EPS)
        y = y * _silu(gate)
        y_ref[:, cols] = y.astype(y_ref.dtype)

    @pl.when(c == n_chunks - 1)
    def _():
        for i in range(nseg):
            for hh in range(hb):
                sf_ref[i, hh] = s_scr[i, hh]


def _ret_scan(p3, cos, sin, lg, s0, y_prev, *, hb, out_width, out_col0, nb, n_chunks, seg_len, first_valid,
              row_block, cs_block):
    t_rows = p3.shape[0]
    nseg = ROWS // seg_len
    half = RET_DIM // 2
    wide = hb * RET_DIM
    hblk = RET_HEADS // hb
    out_blk0 = out_col0 // wide
    in_specs = [
        pl.BlockSpec((ROWS, wide), lambda b, h, c: (row_block(b, c), h)),
        pl.BlockSpec((ROWS, wide), lambda b, h, c: (row_block(b, c), hblk + h)),
        pl.BlockSpec((ROWS, wide), lambda b, h, c: (row_block(b, c), 2 * hblk + h)),
        pl.BlockSpec((ROWS, wide), lambda b, h, c: (row_block(b, c), 3 * hblk + h)),
        pl.BlockSpec((ROWS, half), lambda b, h, c: (cs_block(b, c), 0)),
        pl.BlockSpec((ROWS, half), lambda b, h, c: (cs_block(b, c), 0)),
        pl.BlockSpec((hb, 1, V7X_LANES), lambda b, h, c: (h, 0, 0)),
    ]
    args = [p3, p3, p3, p3, cos, sin, lg]
    state_spec = pl.BlockSpec((nseg, hb, RET_DIM, RET_DIM), lambda b, h, c: (b, h, 0, 0))
    if s0 is not None:
        in_specs.append(state_spec)
        args.append(s0)
    aliases = _alias_prev(in_specs, args, y_prev)
    return pl.pallas_call(
        functools.partial(_ret_body, seg_len=seg_len, n_chunks=n_chunks, first_valid=first_valid,
                          has_s0=s0 is not None, has_prev=y_prev is not None),
        grid=(nb, hblk, n_chunks),
        in_specs=in_specs,
        out_specs=[pl.BlockSpec((ROWS, wide), lambda b, h, c: (row_block(b, c), out_blk0 + h)),
                   state_spec],
        out_shape=[jax.ShapeDtypeStruct((t_rows, out_width), BF16),
                   jax.ShapeDtypeStruct((nb * nseg, RET_HEADS, RET_DIM, RET_DIM), F32)],
        scratch_shapes=[pltpu.VMEM((nseg, hb, RET_DIM, RET_DIM), F32)],
        input_output_aliases=aliases,
        compiler_params=_cparams(("arbitrary", "arbitrary", "arbitrary")),
        name="ret_scan",
    )(*args)


def _lru_body(x_ref, gate_ref, wa_ref, wx_ref, ba_ref, bx_ref, sp_ref, h0_ref, *rest,
              seg_len, n_chunks, first_valid, mark_pos0, has_prev, conv):
    rest = list(rest)
    cw = [rest.pop(0) for _ in range(2)] if conv else None
    if has_prev:
        rest.pop(0)
    y_ref, hl_ref, carry = rest[:3]
    c = pl.program_id(2)
    nvalid = jnp.where(c == 0, first_valid, ROWS) if first_valid != ROWS else ROWS
    row_c = lax.broadcasted_iota(I32, (ROWS, 1), 0)
    valid = row_c < nvalid

    if conv:
        x_in = _conv_block(x_ref[...], rest[3], cw[0][...], cw[1][...], c, first_valid)
    else:
        x_in = x_ref[...]
    x = jnp.where(valid, x_in, 0.0)
    xb = x.astype(BF16)
    npp = x.shape[1] // PAIR_W
    ra, ri = [], []
    for pp in range(npp):
        xbp = xb[:, pp * PAIR_W:(pp + 1) * PAIR_W]
        ra.append(jnp.dot(xbp, wa_ref[pp], preferred_element_type=F32) + ba_ref[pp])
        ri.append(jnp.dot(xbp, wx_ref[pp], preferred_element_type=F32) + bx_ref[pp])
    r = _sigmoid(jnp.concatenate(ra, axis=1))
    ig = _sigmoid(jnp.concatenate(ri, axis=1))
    log_a = -LRU_C * r * jnp.concatenate([sp_ref[pp] for pp in range(npp)], axis=1)
    a = jnp.exp(log_a)
    m2 = -jnp.tanh(log_a) * (a * a + 1.0)
    mult = m2 * lax.rsqrt(jnp.maximum(m2, F32_TINY))
    if mark_pos0:
        mult = jnp.where(jnp.logical_and(c == 0, row_c == 0), 1.0, mult)
    bterm = mult * ig * x
    a = jnp.where(valid, a, 1.0)
    bterm = jnp.where(valid, bterm, 0.0)

    sub = lax.broadcasted_iota(I32, (V7X_SUBLANES, x.shape[1]), 0)
    if seg_len == ROWS:
        @pl.when(c == 0)
        def _():
            carry[...] = h0_ref[...]
        h_prev = carry[...]
    tiles = []
    for t in range(ROWS // V7X_SUBLANES):
        at = a[8 * t:8 * t + 8]
        bt = bterm[8 * t:8 * t + 8]
        for d in (1, 2, 4):
            a_sh = jnp.where(sub >= d, pltpu.roll(at, d, 0), 1.0)
            b_sh = jnp.where(sub >= d, pltpu.roll(bt, d, 0), 0.0)
            bt = at * b_sh + bt
            at = at * a_sh
        if seg_len != ROWS:
            h_prev = h0_ref[t:t + 1, :]
        ht = bt + at * h_prev
        h_prev = ht[7:8, :]
        if seg_len != ROWS:
            hl_ref[t:t + 1, :] = h_prev
        tiles.append(ht)
    h = jnp.concatenate(tiles, axis=0)
    if seg_len == ROWS:
        carry[...] = h_prev

        @pl.when(c == n_chunks - 1)
        def _():
            hl_ref[...] = h_prev

    g = gate_ref[...]
    gelu = 0.5 * g * (1.0 + jnp.tanh(math.sqrt(2.0 / math.pi) * (g + 0.044715 * (g * g * g))))
    y_ref[...] = (h * gelu).astype(y_ref.dtype)


def _lru_scan(x_src, x_col0, data_block, p_lru, wa_p, wx_p, ba_p, bx_p, sp_p, conv_wb, h0, y_prev, *,
              nb, n_chunks, seg_len, first_valid, mark_pos0, row_block):
    t_rows = p_lru.shape[0]
    width = p_lru.shape[1] // 2
    n_pairs = width // PAIR_W
    nseg = ROWS // seg_len
    conv = conv_wb is not None
    hrows = 1 if seg_len == ROWS else nseg
    sw = LRU_PPS * PAIR_W
    x_blk0 = x_col0 // sw
    h_spec = pl.BlockSpec((None, hrows, sw), lambda b, p, c: (b, 0, p))
    vec_spec = pl.BlockSpec((LRU_PPS, 1, PAIR_W), lambda b, p, c: (p, 0, 0))
    in_specs = [
        pl.BlockSpec((ROWS, sw), lambda b, p, c: (data_block(b, c), x_blk0 + p)),
        pl.BlockSpec((ROWS, sw), lambda b, p, c: (row_block(b, c), p)),
        pl.BlockSpec((LRU_PPS, PAIR_W, PAIR_W), lambda b, p, c: (p, 0, 0)),
        pl.BlockSpec((LRU_PPS, PAIR_W, PAIR_W), lambda b, p, c: (p, 0, 0)),
        vec_spec, vec_spec, vec_spec, h_spec,
    ]
    args = [x_src, p_lru, wa_p, wx_p, ba_p, bx_p, sp_p, h0]
    scratch = [pltpu.VMEM((1, sw), F32)]
    if conv:
        cw, cb = conv_wb
        in_specs += [pl.BlockSpec((CONV_W, sw), lambda b, p, c: (0, p)),
                     pl.BlockSpec((1, sw), lambda b, p, c: (0, p))]
        args += [cw, cb]
        scratch.append(pltpu.VMEM((ROWS + V7X_SUBLANES, sw), F32))
    aliases = _alias_prev(in_specs, args, y_prev)
    return pl.pallas_call(
        functools.partial(_lru_body, seg_len=seg_len, n_chunks=n_chunks, first_valid=first_valid,
                          mark_pos0=mark_pos0, has_prev=y_prev is not None, conv=conv),
        grid=(nb, n_pairs // LRU_PPS, n_chunks),
        in_specs=in_specs,
        out_specs=[pl.BlockSpec((ROWS, sw), lambda b, p, c: (row_block(b, c), p)), h_spec],
        out_shape=[jax.ShapeDtypeStruct((t_rows, width), BF16),
                   jax.ShapeDtypeStruct((nb, hrows, width), F32)],
        scratch_shapes=scratch,
        input_output_aliases=aliases,
        compiler_params=_cparams(("arbitrary", "arbitrary", "arbitrary")),
        name="rglru",
    )(*args)


def _router_body(h_ref, g_ref, wr_ref, br_ref, hn_ref, route_ref, cnt_ref, carry, *, tm):
    i = pl.program_id(0)

    @pl.when(i == 0)
    def _():
        carry[...] = jnp.zeros_like(carry)

    x = h_ref[...]
    hn = x * lax.rsqrt(jnp.mean(x * x, axis=-1, keepdims=True) + EPS) * g_ref[...]
    hn_ref[...] = hn

    h_hi = hn.astype(BF16)
    h_lo = (hn - h_hi.astype(F32)).astype(BF16)
    w = wr_ref[...]
    w_hi = w.astype(BF16)
    w_lo = (w - w_hi.astype(F32)).astype(BF16)
    logits = (jnp.dot(h_hi, w_hi, preferred_element_type=F32)
              + jnp.dot(h_hi, w_lo, preferred_element_type=F32)
              + jnp.dot(h_lo, w_hi, preferred_element_type=F32)) + br_ref[...]

    lane_i = lax.broadcasted_iota(I32, (tm, V7X_LANES), 1)
    lane = lane_i.astype(F32)
    big = float(4 * V7X_LANES)
    neg = -jnp.inf
    is_g = jnp.logical_and(lane_i >= N_EXPERTS, lane_i < N_EXPERTS + MOE_GROUPS)
    glog = jnp.where(is_g, logits, neg)
    gmax = jnp.max(glog, axis=-1, keepdims=True)
    gsel = jnp.min(jnp.where(glog == gmax, lane, big), axis=-1, keepdims=True) - float(N_EXPERTS)
    gsum = jnp.sum(jnp.exp(glog - gmax), axis=-1, keepdims=True)
    pg = 1.0 / gsum
    lo = gsel * float(MOE_PER_GROUP)
    in_grp = jnp.logical_and(lane >= lo, lane < lo + MOE_PER_GROUP)
    elog = jnp.where(in_grp, logits, neg)
    emax = jnp.max(elog, axis=-1, keepdims=True)
    eexp = jnp.exp(elog - emax)
    ep = eexp / jnp.sum(eexp, axis=-1, keepdims=True)
    ep = jnp.where(in_grp, ep, -1.0)
    v1 = jnp.max(ep, axis=-1, keepdims=True)
    i1 = jnp.min(jnp.where(ep == v1, lane, big), axis=-1, keepdims=True)
    ep2 = jnp.where(lane == i1, -1.0, ep)
    v2 = jnp.max(ep2, axis=-1, keepdims=True)
    i2 = jnp.min(jnp.where(ep2 == v2, lane, big), axis=-1, keepdims=True)
    vs = v1 + v2
    w1 = v1 / vs * pg
    w2 = v2 / vs * pg

    oh1 = lane == i1
    oh2 = lane == i2
    oh = jnp.logical_or(oh1, oh2).astype(BF16)
    ti = lax.broadcasted_iota(I32, (tm, tm), 0)
    si = lax.broadcasted_iota(I32, (tm, tm), 1)
    before = (si < ti).astype(BF16)
    tot = jnp.dot(before, oh, preferred_element_type=F32) + carry[...]
    rank1 = jnp.sum(jnp.where(oh1, tot, 0.0), axis=-1, keepdims=True)
    rank2 = jnp.sum(jnp.where(oh2, tot, 0.0), axis=-1, keepdims=True)
    carry[...] = carry[...] + jnp.sum(oh.astype(F32), axis=0, keepdims=True)
    cnt_ref[...] = carry[...]

    out = jnp.where(lane_i == 0, i1, 0.0)
    out = jnp.where(lane_i == 1, i2, out)
    out = jnp.where(lane_i == 2, rank1, out)
    out = jnp.where(lane_i == 3, rank2, out)
    out = jnp.where(lane_i == 4, w1, out)
    out = jnp.where(lane_i == 5, w2, out)
    route_ref[...] = out


def _router(h, g, wr, br, tm):
    t_rows, d = h.shape
    return pl.pallas_call(
        functools.partial(_router_body, tm=tm),
        grid=(t_rows // tm,),
        in_specs=[pl.BlockSpec((tm, d), lambda i: (i, 0)),
                  pl.BlockSpec((1, d), lambda i: (0, 0)),
                  pl.BlockSpec((d, V7X_LANES), lambda i: (0, 0)),
                  pl.BlockSpec((1, V7X_LANES), lambda i: (0, 0))],
        out_specs=[pl.BlockSpec((tm, d), lambda i: (i, 0)),
                   pl.BlockSpec((tm, V7X_LANES), lambda i: (i, 0)),
                   pl.BlockSpec((1, V7X_LANES), lambda i: (0, 0))],
        out_shape=[jax.ShapeDtypeStruct((t_rows, d), F32),
                   jax.ShapeDtypeStruct((t_rows, V7X_LANES), F32),
                   jax.ShapeDtypeStruct((1, V7X_LANES), F32)],
        scratch_shapes=[pltpu.VMEM((1, V7X_LANES), F32)],
        compiler_params=_cparams(("arbitrary",)),
        name="moe_router",
    )(h, g.reshape(1, d), wr, br)


def _expert_body(te_ref, tfirst_ref, ngrp_ref, nt_ref, dst_cur, dst_nxt, hn_hbm, w1_hbm, w3_hbm, w2_hbm,
                 o_hbm, xbuf, ybuf, wb1, wb3, wb2, st13, st2, gsem, ssem, wsem13, wsem2,
                 *, tm, layer, nt_max, t_rows):
    i = pl.program_id(0)
    nt = nt_ref[0]
    slot = lax.rem(i, 2)

    def token_of(d):
        return jnp.where(d >= 2 * t_rows, 0, jnp.where(d >= t_rows, d - t_rows, d))

    def start_gather(dst_ref, s, ngrp):
        def body(r8, carry_):
            for u in range(ROW_UNROLL):
                r = r8 * ROW_UNROLL + u
                pltpu.make_async_copy(hn_hbm.at[pl.ds(token_of(dst_ref[0, 0, r]), 1)],
                                      xbuf.at[s, pl.ds(r, 1)], gsem.at[s]).start()
            return carry_
        lax.fori_loop(0, ngrp, body, 0)

    def wait_gather(s, ngrp):
        def body(r8, carry_):
            pltpu.make_async_copy(hn_hbm.at[pl.ds(0, ROW_UNROLL)], xbuf.at[s, pl.ds(0, ROW_UNROLL)],
                                  gsem.at[s]).wait()
            return carry_
        lax.fori_loop(0, ngrp, body, 0)

    def start_scatter(ngrp):
        def body(r8, carry_):
            for u in range(ROW_UNROLL):
                r = r8 * ROW_UNROLL + u
                pltpu.make_async_copy(ybuf.at[pl.ds(r, 1)], o_hbm.at[pl.ds(dst_cur[0, 0, r], 1)],
                                      ssem.at[0]).start(priority=1)
            return carry_
        lax.fori_loop(0, ngrp, body, 0)

    def wait_scatter(ngrp):
        def body(r8, carry_):
            pltpu.make_async_copy(ybuf.at[pl.ds(0, ROW_UNROLL)], o_hbm.at[pl.ds(0, ROW_UNROLL)],
                                  ssem.at[0]).wait()
            return carry_
        lax.fori_loop(0, ngrp, body, 0)

    @pl.when(i == 0)
    def _():
        xbuf[...] = jnp.zeros_like(xbuf)
        start_gather(dst_cur, 0, ngrp_ref[0])

    @pl.when(i + 1 < nt)
    def _():
        start_gather(dst_nxt, 1 - slot, ngrp_ref[jnp.minimum(i + 1, nt_max - 1)])

    @pl.when(jnp.logical_and(i < nt, tfirst_ref[i] == 1))
    def _():
        e = te_ref[i]
        n13 = 2 * (wb1.shape[0] // W13_ROWS)
        n2 = wb2.shape[0] // W2_ROWS

        def c13(k):
            src = w1_hbm if k % 2 == 0 else w3_hbm
            r0 = (k // 2) * W13_ROWS
            return pltpu.make_async_copy(src.at[layer, e, pl.ds(r0, W13_ROWS)], st13.at[k % W_RING],
                                         wsem13.at[k % W_RING])

        def c2(k):
            return pltpu.make_async_copy(w2_hbm.at[layer, e, pl.ds(k * W2_ROWS, W2_ROWS)],
                                         st2.at[k % W_RING], wsem2.at[k % W_RING])

        for k in range(W_RING):
            c13(k).start(priority=1)
        for k in range(W_RING):
            c2(k).start(priority=1)
        for k in range(n13):
            c13(k).wait()
            dstw = wb1 if k % 2 == 0 else wb3
            r0 = (k // 2) * W13_ROWS
            dstw[r0:r0 + W13_ROWS, :] = st13[k % W_RING].astype(BF16)
            if k + W_RING < n13:
                c13(k + W_RING).start(priority=1)
        for k in range(n2):
            c2(k).wait()
            wb2[k * W2_ROWS:(k + 1) * W2_ROWS, :] = st2[k % W_RING].astype(BF16)
            if k + W_RING < n2:
                c2(k + W_RING).start(priority=1)

    @pl.when(i < nt)
    def _():
        wait_gather(slot, ngrp_ref[i])
        x = xbuf[slot].astype(BF16)
        a = jnp.dot(x, wb1[...], preferred_element_type=F32)
        b = jnp.dot(x, wb3[...], preferred_element_type=F32)
        hdn = (_silu(a) * b).astype(BF16)

        @pl.when(i > 0)
        def _():
            wait_scatter(ngrp_ref[jnp.maximum(i - 1, 0)])

        ybuf[...] = jnp.dot(hdn, wb2[...], preferred_element_type=F32)
        start_scatter(ngrp_ref[i])

    @pl.when(i == nt)
    def _():
        wait_scatter(ngrp_ref[jnp.maximum(i - 1, 0)])

    @pl.when(jnp.logical_and(i == nt_max - 1, i < nt))
    def _():
        wait_scatter(ngrp_ref[i])


def _experts(hn, dst, tile_expert, tile_first, tile_ngrp, n_tiles, w1, w3, w2, *, layer, tm, nt_max):
    t_rows, d = hn.shape
    ff = w1.shape[3]
    smem_spec = lambda f: pl.BlockSpec((1, 1, tm), f, memory_space=pltpu.SMEM)
    grid_spec = pltpu.PrefetchScalarGridSpec(
        num_scalar_prefetch=4,
        grid=(nt_max,),
        in_specs=[
            smem_spec(lambda i, *_: (i, 0, 0)),
            smem_spec(lambda i, *_: (jnp.minimum(i + 1, nt_max - 1), 0, 0)),
            pl.BlockSpec(memory_space=pl.ANY),
            pl.BlockSpec(memory_space=pl.ANY),
            pl.BlockSpec(memory_space=pl.ANY),
            pl.BlockSpec(memory_space=pl.ANY),
        ],
        out_specs=pl.BlockSpec(memory_space=pl.ANY),
        scratch_shapes=[
            pltpu.VMEM((2, tm, d), F32), pltpu.VMEM((tm, d), F32),
            pltpu.VMEM((d, ff), BF16), pltpu.VMEM((d, ff), BF16), pltpu.VMEM((ff, d), BF16),
            pltpu.VMEM((W_RING, W13_ROWS, ff), F32), pltpu.VMEM((W_RING, W2_ROWS, d), F32),
            pltpu.SemaphoreType.DMA((2,)), pltpu.SemaphoreType.DMA((1,)),
            pltpu.SemaphoreType.DMA((W_RING,)), pltpu.SemaphoreType.DMA((W_RING,)),
        ],
    )
    return pl.pallas_call(
        functools.partial(_expert_body, tm=tm, layer=layer, nt_max=nt_max, t_rows=t_rows),
        grid_spec=grid_spec,
        out_shape=jax.ShapeDtypeStruct((2 * t_rows + tm, d), F32),
        compiler_params=_cparams(("arbitrary",)),
        name="moe_experts",
    )(tile_expert, tile_first, tile_ngrp, n_tiles, dst, dst, hn, w1, w3, w2)


def _combine_body(h_ref, y0_ref, y1_ref, route_ref, *rest, with_norm):
    route = route_ref[...]
    h_new = h_ref[...] + route[:, 4:5] * y0_ref[...] + route[:, 5:6] * y1_ref[...]
    if with_norm:
        g_ref, o_ref, n_ref = rest
        o_ref[...] = h_new
        y = h_new * lax.rsqrt(jnp.mean(h_new * h_new, axis=-1, keepdims=True) + EPS)
        n_ref[...] = (y * g_ref[...]).astype(n_ref.dtype)
    else:
        rest[0][...] = h_new


def _combine(h, y2, route, g_next, *, tm):
    t_rows, d = h.shape
    n = t_rows // tm
    row = pl.BlockSpec((tm, d), lambda i: (i, 0))
    in_specs = [row, row,
                pl.BlockSpec((tm, d), lambda i: (n + i, 0)),
                pl.BlockSpec((tm, V7X_LANES), lambda i: (i, 0))]
    args = [h, y2, y2, route]
    with_norm = g_next is not None
    if with_norm:
        in_specs.append(pl.BlockSpec((1, d), lambda i: (0, 0)))
        args.append(g_next.reshape(1, d))
        out_specs = [row, row]
        out_shape = [jax.ShapeDtypeStruct((t_rows, d), F32), jax.ShapeDtypeStruct((t_rows, d), BF16)]
    else:
        out_specs = row
        out_shape = jax.ShapeDtypeStruct((t_rows, d), F32)
    return pl.pallas_call(
        functools.partial(_combine_body, with_norm=with_norm),
        grid=(t_rows // tm,),
        in_specs=in_specs,
        out_specs=out_specs,
        out_shape=out_shape,
        compiler_params=_cparams(("arbitrary",)),
        name="moe_combine",
    )(*args)


def _dispatch_body(cnt_ref, rt_ref, dst_ref, te_ref, tfirst_ref, ngrp_ref, nt_ref, off_ref,
                   *, tm, nt_max, t_rows):
    tm_shift = int(math.log2(tm))
    grp_shift = int(math.log2(ROW_UNROLL))

    def per_expert(e, tile0):
        c = cnt_ref[e]
        n = lax.shift_right_logical(c + (tm - 1), tm_shift)
        off_ref[e] = tile0 * tm

        def per_tile(k, carry_):
            i = tile0 + k
            te_ref[i] = e
            tfirst_ref[i] = jnp.where(k == 0, 1, 0)
            nval = jnp.minimum(c - k * tm, tm)
            ngrp_ref[i] = lax.shift_right_logical(nval + (ROW_UNROLL - 1), grp_shift)
            return carry_
        lax.fori_loop(0, n, per_tile, 0)
        for u in range(ROW_UNROLL - 1):
            r = c + u
            dst_ref[tile0 * tm + r] = 2 * t_rows + jnp.bitwise_and(r, tm - 1)
        return tile0 + n
    nt = lax.fori_loop(0, N_EXPERTS, per_expert, 0)
    nt_ref[0] = nt

    def dead(i, carry_):
        te_ref[i] = 0
        tfirst_ref[i] = 0
        ngrp_ref[i] = 0
        return carry_
    lax.fori_loop(nt, nt_max, dead, 0)

    def per_token(t, carry_):
        dst_ref[off_ref[rt_ref[t]] + rt_ref[2 * t_rows + t]] = t
        dst_ref[off_ref[rt_ref[t_rows + t]] + rt_ref[3 * t_rows + t]] = t_rows + t
        return carry_
    lax.fori_loop(0, t_rows, per_token, 0, unroll=ROW_UNROLL)


def _dispatch(cnt, rt, *, tm, nt_max, t_rows):
    smem = pl.BlockSpec(memory_space=pltpu.SMEM)
    return pl.pallas_call(
        functools.partial(_dispatch_body, tm=tm, nt_max=nt_max, t_rows=t_rows),
        in_specs=[smem, smem],
        out_specs=[smem] * 5,
        out_shape=[jax.ShapeDtypeStruct((nt_max * tm,), I32), jax.ShapeDtypeStruct((nt_max,), I32),
                   jax.ShapeDtypeStruct((nt_max,), I32), jax.ShapeDtypeStruct((nt_max,), I32),
                   jax.ShapeDtypeStruct((1,), I32)],
        scratch_shapes=[pltpu.SMEM((N_EXPERTS,), I32)],
        name="moe_dispatch",
    )(cnt, rt)


def _moe(h, layer, g, w_rg, b_rg, w_re, b_re, w1, w3, w2, g_next, *, tm_tok, tm_e):
    t_rows, d = h.shape
    pad = V7X_LANES - N_EXPERTS - MOE_GROUPS
    wr = jnp.concatenate([w_re[layer], w_rg[layer], jnp.zeros((d, pad), F32)], axis=1)
    br = jnp.concatenate([b_re[layer], b_rg[layer], jnp.zeros((pad,), F32)]).reshape(1, V7X_LANES)
    hn, route, counts = _router(h, g, wr, br, tm_tok)

    nt_max = (2 * t_rows) // tm_e + N_EXPERTS
    cnt = counts[0, :N_EXPERTS].astype(I32)
    rt = route[:, :4].astype(I32).T.reshape(4 * t_rows)
    dst, te, tfirst, ngrp, n_tiles = _dispatch(cnt, rt, tm=tm_e, nt_max=nt_max, t_rows=t_rows)

    y2 = _experts(hn, dst.reshape(nt_max, 1, tm_e), te, tfirst, ngrp, n_tiles, w1, w3, w2,
                  layer=layer, tm=tm_e, nt_max=nt_max)
    return _combine(h, y2, route, g_next, tm=_pick(t_rows, (256, 128)))


def _conv_sample(u, buf, w, b):
    full = jnp.concatenate([buf, u], axis=1)
    l = u.shape[1]
    y = b + w[0] * full[:, 0:l]
    for t in range(1, CONV_W):
        y = y + w[t] * full[:, t:t + l]
    return y, full[:, -(CONV_W - 1):]


def _pair_blocks(w):
    nblk, bw, _ = w.shape
    w = w.reshape(nblk // 2, 2, bw, bw)
    z = jnp.zeros((nblk // 2, bw, bw), w.dtype)
    top = jnp.concatenate([w[:, 0], z], axis=2)
    bot = jnp.concatenate([z, w[:, 1]], axis=2)
    return jnp.concatenate([top, bot], axis=1).astype(BF16)


def kernel(x_prompt, x_sample, state_ssd_conv, state_ssd, state_ret, state_lru_conv, state_lru, meta_tokens, norm_mix, norm_ffn, norm_final, w_in0, ssd_conv_w, ssd_conv_b, ssd_dt_bias, ssd_a_log, ssd_d, ssd_norm_g, w_out0, w_in1, lru_conv_w, lru_conv_b, lru_wa, lru_ba, lru_wx, lru_bx, lru_lambda, w_out1, moe_w_rg, moe_b_rg, moe_w_re, moe_b_re, moe_w1, moe_w3, moe_w2):
    nb, seq, d = x_prompt.shape
    nbs, ls, _ = x_sample.shape
    assert seq % ROWS == 0 and ROWS % ls == 0 and (nbs * ls) % ROWS == 0 and ls >= CONV_W - 1
    n_xblk = seq // ROWS
    blk_p = n_xblk + 1
    lp_pad = blk_p * ROWS
    tp = nb * lp_pad
    ts = nbs * ls
    t_all = tp + ts
    nbs_blk = ts // ROWS

    def rb_prompt(b, c):
        return b * blk_p + lax.rem(c + n_xblk, blk_p)

    def rb_sample(b, c):
        return nb * blk_p + b

    def rb_own(b, c):
        return b

    def _last_prompt_rows(p, c0, c1):
        return jnp.stack([p[b * lp_pad + seq - (CONV_W - 1):b * lp_pad + seq, c0:c1] for b in range(nb)])

    prompt_kw = dict(nb=nb, n_chunks=blk_p, seg_len=ROWS, first_valid=N_META, row_block=rb_prompt)
    sample_kw = dict(nb=nbs_blk, n_chunks=1, seg_len=ls, first_valid=ROWS, row_block=rb_sample)

    h, hn = _embed_norm(x_prompt, x_sample, meta_tokens, norm_mix[0])

    tm_tok = _pick(t_all, (512, 256, 128))
    tm_mm = _pick(t_all, (1216, 608, 512, 256, 128))

    d_inner = SSD_GROUPS * GROUP_W
    conv_dim = d_inner + 2 * SSD_GROUPS * SSD_STATE
    n_heads = SSD_GROUPS * SSD_HPG
    c_zx = d_inner + conv_dim
    c_qkvg = w_in0.shape[1] - c_zx - n_heads
    w_in0_t = w_in0.T
    p1 = _matmul_t(hn, w_in0_t, 0, c_zx, tm_mm, 512)
    pdt = _matmul_t(hn, w_in0_t, c_zx, n_heads, tm_mm, n_heads)
    p3 = _matmul_t(hn, w_in0_t, c_zx + n_heads, c_qkvg, tm_mm, 512)

    dt = jax.nn.softplus(pdt + ssd_dt_bias)
    la = dt * (-jnp.exp(ssd_a_log))
    dt3 = dt.reshape(t_all, SSD_GROUPS, SSD_HPG)
    la3 = la.reshape(t_all, SSD_GROUPS, SSD_HPG)
    dtc = dt3.transpose(1, 0, 2)[:, None]
    lac = la3.transpose(1, 0, 2)[:, None]
    lar = la3.transpose(1, 2, 0)[:, None]
    dx = jnp.repeat(ssd_d, SSD_HEAD_DIM).reshape(SSD_GROUPS, 1, GROUP_W)
    ng = ssd_norm_g.reshape(SSD_GROUPS, 1, GROUP_W)

    xbc_s, conv_s = _conv_sample(p1[tp:, d_inner:].reshape(nbs, ls, conv_dim), state_ssd_conv,
                                 ssd_conv_w, ssd_conv_b)
    xbc_s = jax.nn.silu(xbc_s).reshape(ts, conv_dim)
    conv_p = _last_prompt_rows(p1, d_inner, c_zx)

    mix_w = d_inner + RET_HEADS * RET_DIM
    ssd_common = (p1, dtc, lac, lar, dx, ng)
    mix, ssd_p = _ssd_scan(p1, d_inner, rb_prompt, *ssd_common,
                           (ssd_conv_w, ssd_conv_b.reshape(1, conv_dim)), None, None,
                           gps=SSD_GPS_PROMPT, out_width=mix_w, **prompt_kw)
    mix, ssd_s = _ssd_scan(xbc_s, 0, rb_own, *ssd_common, None, jnp.swapaxes(state_ssd, 2, 3), mix,
                           gps=SSD_GPS_SAMPLE, out_width=mix_w, **sample_kw)

    half = RET_DIM // 2
    inv = 1.0 / (ROPE_BASE ** (jnp.arange(half, dtype=F32) / half))
    pos_p = jnp.concatenate([N_META + jnp.arange(seq, dtype=I32), jnp.arange(N_META, dtype=I32),
                             jnp.zeros((lp_pad - seq - N_META,), I32)])
    pos_s = PAST_LEN + (jnp.arange(ROWS, dtype=I32) % ls)
    ang = jnp.concatenate([pos_p, pos_s]).astype(F32)[:, None] * inv[None, :]
    cos, sin = jnp.cos(ang), jnp.sin(ang)
    log_gamma = jnp.log1p(-jnp.exp2(-5.0 - jnp.arange(RET_HEADS, dtype=F32)))
    lg = jnp.broadcast_to(log_gamma[:, None, None], (RET_HEADS, 1, V7X_LANES))
    ret_kw = dict(out_width=mix_w, out_col0=d_inner)
    mix, ret_p = _ret_scan(p3, cos, sin, lg, None, mix, hb=RET_HB_PROMPT,
                           cs_block=lambda b, c: lax.rem(c + n_xblk, blk_p), **ret_kw, **prompt_kw)
    mix, ret_s = _ret_scan(p3, cos, sin, lg, state_ret, mix, hb=RET_HB_SAMPLE, cs_block=lambda b, c: blk_p,
                           **ret_kw, **sample_kw)
    tm_o = _pick(t_all, (608, 512, 256, 128))
    h = _matmul(mix, w_out0.astype(BF16), d, tm_o, 512, res=h)
    moe_kw = dict(tm_tok=tm_tok, tm_e=min(MOE_TM, tm_tok))
    moe_w = (moe_w_rg, moe_b_rg, moe_w_re, moe_b_re, moe_w1, moe_w3, moe_w2)
    h, hn = _moe(h, 0, norm_ffn[0], *moe_w, norm_mix[1], **moe_kw)

    width = lru_lambda.shape[0]
    p_lru = _matmul(hn, w_in1, 2 * width, tm_mm, 512)
    xc_s, lconv_s = _conv_sample(p_lru[tp:, width:].reshape(nbs, ls, width), state_lru_conv,
                                 lru_conv_w, lru_conv_b)
    xc_s = xc_s.reshape(ts, width)
    lconv_p = _last_prompt_rows(p_lru, width, 2 * width)
    n_pairs = width // PAIR_W
    lru_common = (p_lru, _pair_blocks(lru_wa), _pair_blocks(lru_wx), lru_ba.reshape(n_pairs, 1, PAIR_W),
                  lru_bx.reshape(n_pairs, 1, PAIR_W),
                  jax.nn.softplus(-lru_lambda).reshape(n_pairs, 1, PAIR_W))
    y_lru, lru_p = _lru_scan(p_lru, width, rb_prompt, *lru_common,
                             (lru_conv_w, lru_conv_b.reshape(1, width)), jnp.zeros((nb, 1, width), F32),
                             None, mark_pos0=True, **prompt_kw)
    y_lru, lru_s = _lru_scan(xc_s, 0, rb_own, *lru_common, None,
                             state_lru.reshape(nbs_blk, ROWS // ls, width), y_lru, mark_pos0=False,
                             **sample_kw)
    h = _matmul(y_lru, w_out1, d, tm_o, 512, res=h)
    h = _moe(h, 1, norm_ffn[1], *moe_w, None, **moe_kw)

    y_prompt = _rmsnorm(h, norm_final, F32, ROWS, n_out_blocks=nb * n_xblk,
                        in_block=lambda i: (i // n_xblk) * blk_p + lax.rem(i, n_xblk))
    y_sample = _rmsnorm(h, norm_final, F32, ROWS, n_out_blocks=nbs_blk, in_block=lambda i: nb * blk_p + i)
    return (y_prompt.reshape(nb, seq, d), y_sample.reshape(nbs, ls, d),
            conv_p, jnp.swapaxes(ssd_p, 2, 3), ret_p, lconv_p, lru_p.reshape(nb, width),
            conv_s, jnp.swapaxes(ssd_s, 2, 3), ret_s, lconv_s, lru_s.reshape(nbs, width))
```

```python
import functools
import math

import jax
import jax.numpy as jnp
from jax import lax
from jax.experimental import pallas as pl
from jax.experimental.pallas import tpu as pltpu

F32, BF16, I32 = jnp.float32, jnp.bfloat16, jnp.int32

N_META = 16
CONV_W = 4
EPS = 1e-6
F32_TINY = 1.1754944e-38
PAST_LEN = 16384
SSD_HEAD_DIM = 64
SSD_GROUPS = 8
SSD_HPG = 8
SSD_STATE = 128
RET_HEADS = 16
RET_DIM = 256
RET_HB_PROMPT = 8
RET_HB_SAMPLE = 2
ROPE_BASE = 10000.0
LRU_C = 8.0
MOE_GROUPS = 4
MOE_PER_GROUP = 8
N_EXPERTS = MOE_GROUPS * MOE_PER_GROUP

V7X_LANES = 128
V7X_SUBLANES = 8
V7X_VMEM_LIMIT = 56 * 1024 * 1024
ROWS = 128
GROUP_W = SSD_HPG * SSD_HEAD_DIM
SSD_GPS_PROMPT = 8
SSD_GPS_SAMPLE = 2
XBC_W = GROUP_W + 2 * SSD_STATE
PAIR_W = 2 * 320
LRU_PPS = 8
MOE_TM = 512
W_RING = 4
W13_ROWS = 512
W2_ROWS = 64
ROW_UNROLL = 8


def _cparams(sem):
    return pltpu.CompilerParams(dimension_semantics=sem, vmem_limit_bytes=V7X_VMEM_LIMIT)


def _pick(n, cands):
    for c in cands:
        if n % c == 0:
            return c
    raise ValueError(f"no tile for {n} in {cands}")


def _sigmoid(x):
    return 0.5 * jnp.tanh(0.5 * x) + 0.5


def _silu(x):
    return x * _sigmoid(x)


def _rmsnorm_body(x_ref, g_ref, o_ref):
    x = x_ref[...]
    y = x * lax.rsqrt(jnp.mean(x * x, axis=-1, keepdims=True) + EPS)
    o_ref[...] = (y * g_ref[...]).astype(o_ref.dtype)


def _rmsnorm(x, g, out_dtype, tm, n_out_blocks=None, in_block=None):
    m, d = x.shape
    n_blocks = m // tm if n_out_blocks is None else n_out_blocks
    in_map = (lambda i: (i, 0)) if in_block is None else (lambda i: (in_block(i), 0))
    return pl.pallas_call(
        _rmsnorm_body,
        grid=(n_blocks,),
        in_specs=[pl.BlockSpec((tm, d), in_map), pl.BlockSpec((1, d), lambda i: (0, 0))],
        out_specs=pl.BlockSpec((tm, d), lambda i: (i, 0)),
        out_shape=jax.ShapeDtypeStruct((n_blocks * tm, d), out_dtype),
        compiler_params=_cparams(("arbitrary",)),
        name="rmsnorm",
    )(x, g.reshape(1, d))


def _embed_body(xp_ref, xs_ref, meta_ref, g_ref, h_ref, hn_ref, *, nb, blk_p, n_xblk):
    i = pl.program_id(0)
    k = lax.rem(i, blk_p)
    is_prompt = i < nb * blk_p

    def emit(x):
        h_ref[...] = x
        y = x * lax.rsqrt(jnp.mean(x * x, axis=-1, keepdims=True) + EPS)
        hn_ref[...] = (y * g_ref[...]).astype(hn_ref.dtype)

    @pl.when(jnp.logical_and(is_prompt, k < n_xblk))
    def _():
        emit(xp_ref[...])

    @pl.when(jnp.logical_and(is_prompt, k == n_xblk))
    def _():
        meta = meta_ref[...]
        emit(jnp.concatenate([meta, jnp.zeros((ROWS - meta.shape[0], meta.shape[1]), F32)], axis=0))

    @pl.when(jnp.logical_not(is_prompt))
    def _():
        emit(xs_ref[...])


def _embed_norm(x_prompt, x_sample, meta, g):
    nb, seq, d = x_prompt.shape
    ts = x_sample.shape[0] * x_sample.shape[1]
    n_xblk = seq // ROWS
    blk_p = n_xblk + 1
    n_blocks = nb * blk_p + ts // ROWS
    row = pl.BlockSpec((ROWS, d), lambda i: (i, 0))
    return pl.pallas_call(
        functools.partial(_embed_body, nb=nb, blk_p=blk_p, n_xblk=n_xblk),
        grid=(n_blocks,),
        in_specs=[
            pl.BlockSpec((ROWS, d), lambda i: (jnp.minimum(i // blk_p, nb - 1) * n_xblk
                                               + jnp.minimum(lax.rem(i, blk_p), n_xblk - 1), 0)),
            pl.BlockSpec((ROWS, d), lambda i: (jnp.clip(i - nb * blk_p, 0, ts // ROWS - 1), 0)),
            pl.BlockSpec(meta.shape, lambda i: (0, 0)),
            pl.BlockSpec((1, d), lambda i: (0, 0)),
        ],
        out_specs=[row, row],
        out_shape=[jax.ShapeDtypeStruct((n_blocks * ROWS, d), F32),
                   jax.ShapeDtypeStruct((n_blocks * ROWS, d), BF16)],
        compiler_params=_cparams(("arbitrary",)),
        name="embed_norm",
    )(x_prompt.reshape(nb * seq, d), x_sample.reshape(ts, d), meta, g.reshape(1, d))


def _mm_body(*refs, has_res, cast_b):
    a_ref, b_ref = refs[0], refs[1]
    r_ref = refs[2] if has_res else None
    o_ref = refs[3] if has_res else refs[2]
    if cast_b:
        bs_ref = refs[-1]

        @pl.when(pl.program_id(1) == 0)
        def _():
            bs_ref[...] = b_ref[...].astype(BF16)

        b = bs_ref[...]
    else:
        b = b_ref[...]
    acc = jnp.dot(a_ref[...], b, preferred_element_type=F32)
    if has_res:
        acc = acc + r_ref[...]
    o_ref[...] = acc


def _matmul(a, b, n_cols, tm, tn, res=None):
    m, k = a.shape
    cast_b = b.dtype != BF16
    in_specs = [pl.BlockSpec((tm, k), lambda j, i: (i, 0)),
                pl.BlockSpec((k, tn), lambda j, i: (0, j))]
    args = [a, b]
    if res is not None:
        in_specs.append(pl.BlockSpec((tm, tn), lambda j, i: (i, j)))
        args.append(res)
    return pl.pallas_call(
        functools.partial(_mm_body, has_res=res is not None, cast_b=cast_b),
        grid=(n_cols // tn, m // tm),
        in_specs=in_specs,
        out_specs=pl.BlockSpec((tm, tn), lambda j, i: (i, j)),
        out_shape=jax.ShapeDtypeStruct((m, n_cols), F32),
        scratch_shapes=[pltpu.VMEM((k, tn), BF16)] if cast_b else [],
        compiler_params=_cparams(("arbitrary", "arbitrary")),
        name="proj",
    )(*args)


def _mm_t_body(a_ref, bt_ref, o_ref, bs_ref):
    @pl.when(pl.program_id(1) == 0)
    def _():
        bs_ref[...] = bt_ref[...].astype(BF16)

    o_ref[...] = lax.dot_general(a_ref[...], bs_ref[...], (((1,), (1,)), ((), ())),
                                 preferred_element_type=F32)


def _matmul_t(a, bt, row0, n_cols, tm, tn):
    m, k = a.shape
    return pl.pallas_call(
        _mm_t_body,
        grid=(n_cols // tn, m // tm),
        in_specs=[pl.BlockSpec((tm, k), lambda j, i: (i, 0)),
                  pl.BlockSpec((pl.Element(tn), pl.Element(k)),
                               lambda j, i: (pl.multiple_of(row0 + j * tn, math.gcd(row0, tn)), 0))],
        out_specs=pl.BlockSpec((tm, tn), lambda j, i: (i, j)),
        out_shape=jax.ShapeDtypeStruct((m, n_cols), F32),
        scratch_shapes=[pltpu.VMEM((tn, k), BF16)],
        compiler_params=_cparams(("arbitrary", "arbitrary")),
        name="proj_t",
    )(a, bt)


def _split3(a):
    hi = a.astype(BF16)
    r1 = a - hi.astype(F32)
    mid = r1.astype(BF16)
    lo = (r1 - mid.astype(F32)).astype(BF16)
    return hi, mid, lo


def _dot_a01(a, m01, terms=3):
    hi, mid, lo = _split3(a)
    out = jnp.dot(hi, m01, preferred_element_type=F32) + jnp.dot(mid, m01, preferred_element_type=F32)
    if terms == 3:
        out = out + jnp.dot(lo, m01, preferred_element_type=F32)
    return out


def _dot_01a(m01, a):
    hi, mid, lo = _split3(a)
    return (jnp.dot(m01, hi, preferred_element_type=F32)
            + jnp.dot(m01, mid, preferred_element_type=F32)
            + jnp.dot(m01, lo, preferred_element_type=F32))


def _seg_masks(seg_len):
    ti = lax.broadcasted_iota(I32, (ROWS, ROWS), 0)
    si = lax.broadcasted_iota(I32, (ROWS, ROWS), 1)
    if seg_len == ROWS:
        causal = si <= ti
        causal_t = ti <= si
        sel = si == ROWS - 1
    else:
        shift = int(math.log2(seg_len))
        tseg = lax.shift_right_logical(ti, shift)
        sseg = lax.shift_right_logical(si, shift)
        same = tseg == sseg
        causal = jnp.logical_and(si <= ti, same)
        causal_t = jnp.logical_and(ti <= si, same)
        sel = si == lax.shift_left(tseg, shift) + (seg_len - 1)
    return causal, causal_t, sel


def _conv_block(u, ext_scr, w, b, c, first_valid):
    halo = V7X_SUBLANES

    @pl.when(c == 0)
    def _():
        ext_scr[0:halo, :] = jnp.zeros((halo, u.shape[1]), F32)

    ext_scr[halo:halo + ROWS, :] = u
    y = b + w[0:1] * ext_scr[pl.ds(halo - 3, ROWS), :]
    y = y + w[1:2] * ext_scr[pl.ds(halo - 2, ROWS), :]
    y = y + w[2:3] * ext_scr[pl.ds(halo - 1, ROWS), :]
    y = y + w[3:4] * u

    @pl.when(c == 0)
    def _():
        ext_scr[0:halo, :] = ext_scr[first_valid:first_valid + halo, :]

    @pl.when(c != 0)
    def _():
        ext_scr[0:halo, :] = ext_scr[ROWS:ROWS + halo, :]

    return y


def _alias_prev(in_specs, args, y_prev):
    if y_prev is None:
        return {}
    in_specs.append(pl.BlockSpec(memory_space=pl.ANY))
    args.append(y_prev)
    return {len(args) - 1: 0}


def _ssd_body(xs_ref, bm_ref, cm_ref, z_ref, dtc_ref, lac_ref, lar_ref, dx_ref, ng_ref, *rest,
              seg_len, n_chunks, first_valid, has_s0, has_prev, conv):
    rest = list(rest)
    cw = [rest.pop(0) for _ in range(6)] if conv else None
    s0_ref = rest.pop(0) if has_s0 else None
    if has_prev:
        rest.pop(0)
    y_ref, sf_ref, s_scr = rest[:3]
    nseg = ROWS // seg_len
    ngr = xs_ref.shape[1] // GROUP_W
    c = pl.program_id(2)

    @pl.when(c == 0)
    def _():
        for i in range(nseg):
            for gi in range(ngr):
                if has_s0:
                    s_scr[i, gi] = jnp.concatenate([s0_ref[i, gi * SSD_HPG + j] for j in range(SSD_HPG)],
                                                   axis=0)
                else:
                    s_scr[i, gi] = jnp.zeros((GROUP_W, SSD_STATE), F32)

    if conv:
        u = jnp.concatenate([xs_ref[...], bm_ref[...], cm_ref[...]], axis=1)
        w = jnp.concatenate([cw[0][...], cw[1][...], cw[2][...]], axis=1)
        b = jnp.concatenate([cw[3][...], cw[4][...], cw[5][...]], axis=1)
        act = _silu(_conv_block(u, rest[3], w, b, c, first_valid))
        xs_all = act[:, :ngr * GROUP_W]
        bm_all = act[:, ngr * GROUP_W:ngr * (GROUP_W + SSD_STATE)]
        cm_all = act[:, ngr * (GROUP_W + SSD_STATE):]
    else:
        xs_all, bm_all, cm_all = xs_ref[...], bm_ref[...], cm_ref[...]

    nvalid = jnp.where(c == 0, first_valid, ROWS) if first_valid != ROWS else ROWS
    row_c = lax.broadcasted_iota(I32, (ROWS, 1), 0)
    row_r = lax.broadcasted_iota(I32, (1, ROWS), 1)
    valid = row_c < nvalid
    valid_r = row_r < nvalid
    causal, causal_t, sel = _seg_masks(seg_len)
    tril = causal.astype(BF16)
    triu = causal_t.astype(BF16)
    e_i = lax.broadcasted_iota(I32, (SSD_HPG, GROUP_W), 0)
    e_l = lax.shift_right_logical(lax.broadcasted_iota(I32, (SSD_HPG, GROUP_W), 1), 6)
    expand = (e_i == e_l).astype(BF16)
    lane_head = lax.shift_right_logical(lax.broadcasted_iota(I32, (1, GROUP_W), 1), 6)

    for gi in range(ngr):
        gcols = slice(gi * GROUP_W, (gi + 1) * GROUP_W)
        ncols = slice(gi * SSD_STATE, (gi + 1) * SSD_STATE)
        xs = jnp.where(valid, xs_all[:, gcols], 0.0)
        bm = jnp.where(valid, bm_all[:, ncols], 0.0).astype(BF16)
        cm = jnp.where(valid, cm_all[:, ncols], 0.0).astype(BF16)
        z = jnp.where(valid, z_ref[:, gcols], 0.0)
        dt = jnp.where(valid, dtc_ref[gi, 0], 0.0)
        la = jnp.where(valid, lac_ref[gi, 0], 0.0)
        la_r = jnp.where(valid_r, lar_ref[gi, 0], 0.0)

        cum = _dot_01a(tril, la)
        cum_r = _dot_a01(la_r, triu)
        if seg_len == ROWS:
            cum_last = jnp.broadcast_to(cum[ROWS - 1:ROWS, :], (ROWS, SSD_HPG))
        else:
            cum_last = _dot_01a(sel.astype(BF16), cum)
        ecum_x = _dot_a01(jnp.exp(cum), expand)
        v = xs * _dot_a01(dt, expand, terms=2)
        kvw = (v * _dot_a01(jnp.exp(cum_last - cum), expand, terms=2)).astype(BF16)

        scores = lax.dot_general(cm, bm, (((1,), (1,)), ((), ())), preferred_element_type=F32)
        y = jnp.zeros((ROWS, GROUP_W), F32)
        for j in range(SSD_HPG):
            dj = jnp.exp(jnp.where(causal, cum[:, j:j + 1] - cum_r[j:j + 1, :], -jnp.inf))
            pj = (scores * dj).astype(BF16)
            vj = jnp.where(lane_head == j, v, 0.0).astype(BF16)
            y = y + jnp.dot(pj, vj, preferred_element_type=F32)

        y_inter = []
        for i in range(nseg):
            r0 = i * seg_len
            r_last = r0 + seg_len - 1
            s_old = s_scr[i, gi]
            y_inter.append(lax.dot_general(cm[r0:r0 + seg_len], s_old.astype(BF16),
                                           (((1,), (1,)), ((), ())), preferred_element_type=F32))
            cs = lax.dot_general(kvw[r0:r0 + seg_len], bm[r0:r0 + seg_len], (((0,), (0,)), ((), ())),
                                 preferred_element_type=F32)
            e_last = jnp.exp(cum_r[:, r_last:r_last + 1])
            dec = jnp.concatenate([jnp.broadcast_to(e_last[j:j + 1, :], (SSD_HEAD_DIM, SSD_STATE))
                                   for j in range(SSD_HPG)], axis=0)
            s_scr[i, gi] = dec * s_old + cs
        y_inter = y_inter[0] if nseg == 1 else jnp.concatenate(y_inter, axis=0)
        y = y + y_inter * ecum_x
        y = y + dx_ref[gi] * xs
        y = y * _silu(z)
        y = y * lax.rsqrt(jnp.mean(y * y, axis=-1, keepdims=True) + EPS) * ng_ref[gi]
        y_ref[:, gcols] = y.astype(y_ref.dtype)

    @pl.when(c == n_chunks - 1)
    def _():
        for i in range(nseg):
            for gi in range(ngr):
                s_fin = s_scr[i, gi]
                for j in range(SSD_HPG):
                    sf_ref[i, gi * SSD_HPG + j] = s_fin[j * SSD_HEAD_DIM:(j + 1) * SSD_HEAD_DIM, :]


def _ssd_scan(xbc_src, xbc_col0, data_block, p1, dtc, lac, lar, dx, ng, conv_wb, s0, y_prev, *, gps,
              out_width, nb, n_chunks, seg_len, first_valid, row_block):
    t_rows = p1.shape[0]
    nseg = ROWS // seg_len
    conv = conv_wb is not None
    d_inner = SSD_GROUPS * GROUP_W
    gw = gps * GROUP_W
    sw = gps * SSD_STATE
    x_blk0 = xbc_col0 // gw
    b_blk0 = (xbc_col0 + d_inner) // sw
    c_blk0 = b_blk0 + SSD_GROUPS // gps
    aux = lambda shape, f: pl.BlockSpec((gps,) + shape, f)
    in_specs = [
        pl.BlockSpec((ROWS, gw), lambda b, g, c: (data_block(b, c), x_blk0 + g)),
        pl.BlockSpec((ROWS, sw), lambda b, g, c: (data_block(b, c), b_blk0 + g)),
        pl.BlockSpec((ROWS, sw), lambda b, g, c: (data_block(b, c), c_blk0 + g)),
        pl.BlockSpec((ROWS, gw), lambda b, g, c: (row_block(b, c), g)),
        aux((1, ROWS, SSD_HPG), lambda b, g, c: (g, 0, row_block(b, c), 0)),
        aux((1, ROWS, SSD_HPG), lambda b, g, c: (g, 0, row_block(b, c), 0)),
        aux((1, SSD_HPG, ROWS), lambda b, g, c: (g, 0, 0, row_block(b, c))),
        aux((1, GROUP_W), lambda b, g, c: (g, 0, 0)),
        aux((1, GROUP_W), lambda b, g, c: (g, 0, 0)),
    ]
    args = [xbc_src, xbc_src, xbc_src, p1, dtc, lac, lar, dx, ng]
    scratch = [pltpu.VMEM((nseg, gps, GROUP_W, SSD_STATE), F32)]
    if conv:
        cw, cb = conv_wb
        wb_blk0 = d_inner // sw
        for arr, rows in ((cw, CONV_W), (cb, 1)):
            in_specs += [pl.BlockSpec((rows, gw), lambda b, g, c: (0, g)),
                         pl.BlockSpec((rows, sw), lambda b, g, c: (0, wb_blk0 + g)),
                         pl.BlockSpec((rows, sw), lambda b, g, c: (0, wb_blk0 + SSD_GROUPS // gps + g))]
            args += [arr, arr, arr]
        scratch.append(pltpu.VMEM((ROWS + V7X_SUBLANES, gps * XBC_W), F32))
    state_spec = pl.BlockSpec((nseg, gps * SSD_HPG, SSD_HEAD_DIM, SSD_STATE), lambda b, g, c: (b, g, 0, 0))
    if s0 is not None:
        in_specs.append(state_spec)
        args.append(s0)
    aliases = _alias_prev(in_specs, args, y_prev)
    n_heads = SSD_GROUPS * SSD_HPG
    return pl.pallas_call(
        functools.partial(_ssd_body, seg_len=seg_len, n_chunks=n_chunks, first_valid=first_valid,
                          has_s0=s0 is not None, has_prev=y_prev is not None, conv=conv),
        grid=(nb, SSD_GROUPS // gps, n_chunks),
        in_specs=in_specs,
        out_specs=[pl.BlockSpec((ROWS, gw), lambda b, g, c: (row_block(b, c), g)), state_spec],
        out_shape=[jax.ShapeDtypeStruct((t_rows, out_width), BF16),
                   jax.ShapeDtypeStruct((nb * nseg, n_heads, SSD_HEAD_DIM, SSD_STATE), F32)],
        scratch_shapes=scratch,
        input_output_aliases=aliases,
        compiler_params=_cparams(("arbitrary", "arbitrary", "arbitrary")),
        name="ssd_scan",
    )(*args)


def _ret_body(q_ref, k_ref, v_ref, g_ref, cos_ref, sin_ref, lg_ref, *rest,
              seg_len, n_chunks, first_valid, has_s0, has_prev):
    s0_ref = rest[0] if has_s0 else None
    y_ref, sf_ref, s_scr = rest[int(has_s0) + int(has_prev):]
    nseg = ROWS // seg_len
    half = RET_DIM // 2
    hb = q_ref.shape[1] // RET_DIM
    c = pl.program_id(2)

    @pl.when(c == 0)
    def _():
        for i in range(nseg):
            for hh in range(hb):
                if has_s0:
                    s_scr[i, hh] = s0_ref[i, hh]
                else:
                    s_scr[i, hh] = jnp.zeros((RET_DIM, RET_DIM), F32)

    nvalid = jnp.where(c == 0, first_valid, ROWS) if first_valid != ROWS else ROWS
    row_c = lax.broadcasted_iota(I32, (ROWS, 1), 0)
    row_r = lax.broadcasted_iota(I32, (1, ROWS), 1)
    valid = row_c < nvalid
    causal, _, _ = _seg_masks(seg_len)
    pos_c = jnp.minimum(jnp.bitwise_and(row_c, seg_len - 1) + 1, nvalid).astype(F32)
    pos_r = jnp.minimum(jnp.bitwise_and(row_r, seg_len - 1) + 1, nvalid).astype(F32)
    last = jnp.minimum(seg_len, nvalid).astype(F32) if first_valid != ROWS else float(seg_len)
    cos = cos_ref[...]
    sin = sin_ref[...]

    def rot(x):
        x1, x2 = x[:, :half], x[:, half:]
        return jnp.concatenate([x1 * cos - x2 * sin, x1 * sin + x2 * cos], axis=1)

    for hh in range(hb):
        cols = slice(hh * RET_DIM, (hh + 1) * RET_DIM)
        lg_c = lg_ref[hh][:, 0:1]
        cum = pos_c * lg_c
        cum_r = pos_r * lg_c
        cum_last = last * lg_c
        q = rot(jnp.where(valid, q_ref[:, cols], 0.0)).astype(BF16)
        k = (rot(jnp.where(valid, k_ref[:, cols], 0.0)) * (RET_DIM ** -0.5)).astype(BF16)
        v = jnp.where(valid, v_ref[:, cols], 0.0)
        gate = jnp.where(valid, g_ref[:, cols], 0.0)

        decay = jnp.exp(jnp.where(causal, cum - cum_r, -jnp.inf))
        scores = lax.dot_general(q, k, (((1,), (1,)), ((), ())), preferred_element_type=F32)
        y = jnp.dot((scores * decay).astype(BF16), v.astype(BF16), preferred_element_type=F32)
        kvw = (v * jnp.exp(cum_last - cum)).astype(BF16)
        ecum = jnp.exp(cum)
        e_last = jnp.exp(cum_last)

        y_inter = []
        for i in range(nseg):
            r0 = i * seg_len
            s_old = s_scr[i, hh]
            y_inter.append(jnp.dot(q[r0:r0 + seg_len], s_old.astype(BF16), preferred_element_type=F32))
            cs = lax.dot_general(k[r0:r0 + seg_len], kvw[r0:r0 + seg_len], (((0,), (0,)), ((), ())),
                                 preferred_element_type=F32)
            s_scr[i, hh] = e_last * s_old + cs
        y_inter = y_inter[0] if nseg == 1 else jnp.concatenate(y_inter, axis=0)
        y = y + y_inter * ecum
        y = y * lax.rsqrt(jnp.mean(y * y, axis=-1, keepdims=True) + EPS)
        y = y * _silu(gate)
        y_ref[:, cols] = y.astype(y_ref.dtype)

    @pl.when(c == n_chunks - 1)
    def _():
        for i in range(nseg):
            for hh in range(hb):
                sf_ref[i, hh] = s_scr[i, hh]


def _ret_scan(p3, cos, sin, lg, s0, y_prev, *, hb, out_width, out_col0, nb, n_chunks, seg_len, first_valid,
              row_block, cs_block):
    t_rows = p3.shape[0]
    nseg = ROWS // seg_len
    half = RET_DIM // 2
    wide = hb * RET_DIM
    hblk = RET_HEADS // hb
    out_blk0 = out_col0 // wide
    in_specs = [
        pl.BlockSpec((ROWS, wide), lambda b, h, c: (row_block(b, c), h)),
        pl.BlockSpec((ROWS, wide), lambda b, h, c: (row_block(b, c), hblk + h)),
        pl.BlockSpec((ROWS, wide), lambda b, h, c: (row_block(b, c), 2 * hblk + h)),
        pl.BlockSpec((ROWS, wide), lambda b, h, c: (row_block(b, c), 3 * hblk + h)),
        pl.BlockSpec((ROWS, half), lambda b, h, c: (cs_block(b, c), 0)),
        pl.BlockSpec((ROWS, half), lambda b, h, c: (cs_block(b, c), 0)),
        pl.BlockSpec((hb, 1, V7X_LANES), lambda b, h, c: (h, 0, 0)),
    ]
    args = [p3, p3, p3, p3, cos, sin, lg]
    state_spec = pl.BlockSpec((nseg, hb, RET_DIM, RET_DIM), lambda b, h, c: (b, h, 0, 0))
    if s0 is not None:
        in_specs.append(state_spec)
        args.append(s0)
    aliases = _alias_prev(in_specs, args, y_prev)
    return pl.pallas_call(
        functools.partial(_ret_body, seg_len=seg_len, n_chunks=n_chunks, first_valid=first_valid,
                          has_s0=s0 is not None, has_prev=y_prev is not None),
        grid=(nb, hblk, n_chunks),
        in_specs=in_specs,
        out_specs=[pl.BlockSpec((ROWS, wide), lambda b, h, c: (row_block(b, c), out_blk0 + h)),
                   state_spec],
        out_shape=[jax.ShapeDtypeStruct((t_rows, out_width), BF16),
                   jax.ShapeDtypeStruct((nb * nseg, RET_HEADS, RET_DIM, RET_DIM), F32)],
        scratch_shapes=[pltpu.VMEM((nseg, hb, RET_DIM, RET_DIM), F32)],
        input_output_aliases=aliases,
        compiler_params=_cparams(("arbitrary", "arbitrary", "arbitrary")),
        name="ret_scan",
    )(*args)


def _lru_body(x_ref, gate_ref, wa_ref, wx_ref, ba_ref, bx_ref, sp_ref, h0_ref, *rest,
              seg_len, n_chunks, first_valid, mark_pos0, has_prev, conv):
    rest = list(rest)
    cw = [rest.pop(0) for _ in range(2)] if conv else None
    if has_prev:
        rest.pop(0)
    y_ref, hl_ref, carry = rest[:3]
    c = pl.program_id(2)
    nvalid = jnp.where(c == 0, first_valid, ROWS) if first_valid != ROWS else ROWS
    row_c = lax.broadcasted_iota(I32, (ROWS, 1), 0)
    valid = row_c < nvalid

    if conv:
        x_in = _conv_block(x_ref[...], rest[3], cw[0][...], cw[1][...], c, first_valid)
    else:
        x_in = x_ref[...]
    x = jnp.where(valid, x_in, 0.0)
    xb = x.astype(BF16)
    npp = x.shape[1] // PAIR_W
    ra, ri = [], []
    for pp in range(npp):
        xbp = xb[:, pp * PAIR_W:(pp + 1) * PAIR_W]
        ra.append(jnp.dot(xbp, wa_ref[pp], preferred_element_type=F32) + ba_ref[pp])
        ri.append(jnp.dot(xbp, wx_ref[pp], preferred_element_type=F32) + bx_ref[pp])
    r = _sigmoid(jnp.concatenate(ra, axis=1))
    ig = _sigmoid(jnp.concatenate(ri, axis=1))
    log_a = -LRU_C * r * jnp.concatenate([sp_ref[pp] for pp in range(npp)], axis=1)
    a = jnp.exp(log_a)
    m2 = -jnp.tanh(log_a) * (a * a + 1.0)
    mult = m2 * lax.rsqrt(jnp.maximum(m2, F32_TINY))
    if mark_pos0:
        mult = jnp.where(jnp.logical_and(c == 0, row_c == 0), 1.0, mult)
    bterm = mult * ig * x
    a = jnp.where(valid, a, 1.0)
    bterm = jnp.where(valid, bterm, 0.0)

    sub = lax.broadcasted_iota(I32, (V7X_SUBLANES, x.shape[1]), 0)
    if seg_len == ROWS:
        @pl.when(c == 0)
        def _():
            carry[...] = h0_ref[...]
        h_prev = carry[...]
    tiles = []
    for t in range(ROWS // V7X_SUBLANES):
        at = a[8 * t:8 * t + 8]
        bt = bterm[8 * t:8 * t + 8]
        for d in (1, 2, 4):
            a_sh = jnp.where(sub >= d, pltpu.roll(at, d, 0), 1.0)
            b_sh = jnp.where(sub >= d, pltpu.roll(bt, d, 0), 0.0)
            bt = at * b_sh + bt
            at = at * a_sh
        if seg_len != ROWS:
            h_prev = h0_ref[t:t + 1, :]
        ht = bt + at * h_prev
        h_prev = ht[7:8, :]
        if seg_len != ROWS:
            hl_ref[t:t + 1, :] = h_prev
        tiles.append(ht)
    h = jnp.concatenate(tiles, axis=0)
    if seg_len == ROWS:
        carry[...] = h_prev

        @pl.when(c == n_chunks - 1)
        def _():
            hl_ref[...] = h_prev

    g = gate_ref[...]
    gelu = 0.5 * g * (1.0 + jnp.tanh(math.sqrt(2.0 / math.pi) * (g + 0.044715 * (g * g * g))))
    y_ref[...] = (h * gelu).astype(y_ref.dtype)


def _lru_scan(x_src, x_col0, data_block, p_lru, wa_p, wx_p, ba_p, bx_p, sp_p, conv_wb, h0, y_prev, *,
              nb, n_chunks, seg_len, first_valid, mark_pos0, row_block):
    t_rows = p_lru.shape[0]
    width = p_lru.shape[1] // 2
    n_pairs = width // PAIR_W
    nseg = ROWS // seg_len
    conv = conv_wb is not None
    hrows = 1 if seg_len == ROWS else nseg
    sw = LRU_PPS * PAIR_W
    x_blk0 = x_col0 // sw
    h_spec = pl.BlockSpec((None, hrows, sw), lambda b, p, c: (b, 0, p))
    vec_spec = pl.BlockSpec((LRU_PPS, 1, PAIR_W), lambda b, p, c: (p, 0, 0))
    in_specs = [
        pl.BlockSpec((ROWS, sw), lambda b, p, c: (data_block(b, c), x_blk0 + p)),
        pl.BlockSpec((ROWS, sw), lambda b, p, c: (row_block(b, c), p)),
        pl.BlockSpec((LRU_PPS, PAIR_W, PAIR_W), lambda b, p, c: (p, 0, 0)),
        pl.BlockSpec((LRU_PPS, PAIR_W, PAIR_W), lambda b, p, c: (p, 0, 0)),
        vec_spec, vec_spec, vec_spec, h_spec,
    ]
    args = [x_src, p_lru, wa_p, wx_p, ba_p, bx_p, sp_p, h0]
    scratch = [pltpu.VMEM((1, sw), F32)]
    if conv:
        cw, cb = conv_wb
        in_specs += [pl.BlockSpec((CONV_W, sw), lambda b, p, c: (0, p)),
                     pl.BlockSpec((1, sw), lambda b, p, c: (0, p))]
        args += [cw, cb]
        scratch.append(pltpu.VMEM((ROWS + V7X_SUBLANES, sw), F32))
    aliases = _alias_prev(in_specs, args, y_prev)
    return pl.pallas_call(
        functools.partial(_lru_body, seg_len=seg_len, n_chunks=n_chunks, first_valid=first_valid,
                          mark_pos0=mark_pos0, has_prev=y_prev is not None, conv=conv),
        grid=(nb, n_pairs // LRU_PPS, n_chunks),
        in_specs=in_specs,
        out_specs=[pl.BlockSpec((ROWS, sw), lambda b, p, c: (row_block(b, c), p)), h_spec],
        out_shape=[jax.ShapeDtypeStruct((t_rows, width), BF16),
                   jax.ShapeDtypeStruct((nb, hrows, width), F32)],
        scratch_shapes=scratch,
        input_output_aliases=aliases,
        compiler_params=_cparams(("arbitrary", "arbitrary", "arbitrary")),
        name="rglru",
    )(*args)


def _router_body(h_ref, g_ref, wr_ref, br_ref, hn_ref, route_ref, cnt_ref, carry, *, tm):
    i = pl.program_id(0)

    @pl.when(i == 0)
    def _():
        carry[...] = jnp.zeros_like(carry)

    x = h_ref[...]
    hn = x * lax.rsqrt(jnp.mean(x * x, axis=-1, keepdims=True) + EPS) * g_ref[...]
    hn_ref[...] = hn

    h_hi = hn.astype(BF16)
    h_lo = (hn - h_hi.astype(F32)).astype(BF16)
    w = wr_ref[...]
    w_hi = w.astype(BF16)
    w_lo = (w - w_hi.astype(F32)).astype(BF16)
    logits = (jnp.dot(h_hi, w_hi, preferred_element_type=F32)
              + jnp.dot(h_hi, w_lo, preferred_element_type=F32)
              + jnp.dot(h_lo, w_hi, preferred_element_type=F32)) + br_ref[...]

    lane_i = lax.broadcasted_iota(I32, (tm, V7X_LANES), 1)
    lane = lane_i.astype(F32)
    big = float(4 * V7X_LANES)
    neg = -jnp.inf
    is_g = jnp.logical_and(lane_i >= N_EXPERTS, lane_i < N_EXPERTS + MOE_GROUPS)
    glog = jnp.where(is_g, logits, neg)
    gmax = jnp.max(glog, axis=-1, keepdims=True)
    gsel = jnp.min(jnp.where(glog == gmax, lane, big), axis=-1, keepdims=True) - float(N_EXPERTS)
    gsum = jnp.sum(jnp.exp(glog - gmax), axis=-1, keepdims=True)
    pg = 1.0 / gsum
    lo = gsel * float(MOE_PER_GROUP)
    in_grp = jnp.logical_and(lane >= lo, lane < lo + MOE_PER_GROUP)
    elog = jnp.where(in_grp, logits, neg)
    emax = jnp.max(elog, axis=-1, keepdims=True)
    eexp = jnp.exp(elog - emax)
    ep = eexp / jnp.sum(eexp, axis=-1, keepdims=True)
    ep = jnp.where(in_grp, ep, -1.0)
    v1 = jnp.max(ep, axis=-1, keepdims=True)
    i1 = jnp.min(jnp.where(ep == v1, lane, big), axis=-1, keepdims=True)
    ep2 = jnp.where(lane == i1, -1.0, ep)
    v2 = jnp.max(ep2, axis=-1, keepdims=True)
    i2 = jnp.min(jnp.where(ep2 == v2, lane, big), axis=-1, keepdims=True)
    vs = v1 + v2
    w1 = v1 / vs * pg
    w2 = v2 / vs * pg

    oh1 = lane == i1
    oh2 = lane == i2
    oh = jnp.logical_or(oh1, oh2).astype(BF16)
    ti = lax.broadcasted_iota(I32, (tm, tm), 0)
    si = lax.broadcasted_iota(I32, (tm, tm), 1)
    before = (si < ti).astype(BF16)
    tot = jnp.dot(before, oh, preferred_element_type=F32) + carry[...]
    rank1 = jnp.sum(jnp.where(oh1, tot, 0.0), axis=-1, keepdims=True)
    rank2 = jnp.sum(jnp.where(oh2, tot, 0.0), axis=-1, keepdims=True)
    carry[...] = carry[...] + jnp.sum(oh.astype(F32), axis=0, keepdims=True)
    cnt_ref[...] = carry[...]

    out = jnp.where(lane_i == 0, i1, 0.0)
    out = jnp.where(lane_i == 1, i2, out)
    out = jnp.where(lane_i == 2, rank1, out)
    out = jnp.where(lane_i == 3, rank2, out)
    out = jnp.where(lane_i == 4, w1, out)
    out = jnp.where(lane_i == 5, w2, out)
    route_ref[...] = out


def _router(h, g, wr, br, tm):
    t_rows, d = h.shape
    return pl.pallas_call(
        functools.partial(_router_body, tm=tm),
        grid=(t_rows // tm,),
        in_specs=[pl.BlockSpec((tm, d), lambda i: (i, 0)),
                  pl.BlockSpec((1, d), lambda i: (0, 0)),
                  pl.BlockSpec((d, V7X_LANES), lambda i: (0, 0)),
                  pl.BlockSpec((1, V7X_LANES), lambda i: (0, 0))],
        out_specs=[pl.BlockSpec((tm, d), lambda i: (i, 0)),
                   pl.BlockSpec((tm, V7X_LANES), lambda i: (i, 0)),
                   pl.BlockSpec((1, V7X_LANES), lambda i: (0, 0))],
        out_shape=[jax.ShapeDtypeStruct((t_rows, d), F32),
                   jax.ShapeDtypeStruct((t_rows, V7X_LANES), F32),
                   jax.ShapeDtypeStruct((1, V7X_LANES), F32)],
        scratch_shapes=[pltpu.VMEM((1, V7X_LANES), F32)],
        compiler_params=_cparams(("arbitrary",)),
        name="moe_router",
    )(h, g.reshape(1, d), wr, br)


def _expert_body(te_ref, tfirst_ref, ngrp_ref, nt_ref, dst_cur, dst_nxt, hn_hbm, w1_hbm, w3_hbm, w2_hbm,
                 o_hbm, xbuf, ybuf, wb1, wb3, wb2, st13, st2, gsem, ssem, wsem13, wsem2,
                 *, tm, layer, nt_max, t_rows):
    i = pl.program_id(0)
    nt = nt_ref[0]
    slot = lax.rem(i, 2)

    def token_of(d):
        return jnp.where(d >= 2 * t_rows, 0, jnp.where(d >= t_rows, d - t_rows, d))

    def start_gather(dst_ref, s, ngrp):
        def body(r8, carry_):
            for u in range(ROW_UNROLL):
                r = r8 * ROW_UNROLL + u
                pltpu.make_async_copy(hn_hbm.at[pl.ds(token_of(dst_ref[0, 0, r]), 1)],
                                      xbuf.at[s, pl.ds(r, 1)], gsem.at[s]).start()
            return carry_
        lax.fori_loop(0, ngrp, body, 0)

    def wait_gather(s, ngrp):
        def body(r8, carry_):
            pltpu.make_async_copy(hn_hbm.at[pl.ds(0, ROW_UNROLL)], xbuf.at[s, pl.ds(0, ROW_UNROLL)],
                                  gsem.at[s]).wait()
            return carry_
        lax.fori_loop(0, ngrp, body, 0)

    def start_scatter(ngrp):
        def body(r8, carry_):
            for u in range(ROW_UNROLL):
                r = r8 * ROW_UNROLL + u
                pltpu.make_async_copy(ybuf.at[pl.ds(r, 1)], o_hbm.at[pl.ds(dst_cur[0, 0, r], 1)],
                                      ssem.at[0]).start(priority=1)
            return carry_
        lax.fori_loop(0, ngrp, body, 0)

    def wait_scatter(ngrp):
        def body(r8, carry_):
            pltpu.make_async_copy(ybuf.at[pl.ds(0, ROW_UNROLL)], o_hbm.at[pl.ds(0, ROW_UNROLL)],
                                  ssem.at[0]).wait()
            return carry_
        lax.fori_loop(0, ngrp, body, 0)

    @pl.when(i == 0)
    def _():
        xbuf[...] = jnp.zeros_like(xbuf)
        start_gather(dst_cur, 0, ngrp_ref[0])

    @pl.when(i + 1 < nt)
    def _():
        start_gather(dst_nxt, 1 - slot, ngrp_ref[jnp.minimum(i + 1, nt_max - 1)])

    e = te_ref[i]
    first = jnp.logical_and(i < nt, tfirst_ref[i] == 1)
    n13 = 2 * (wb1.shape[0] // W13_ROWS)
    n2 = wb2.shape[0] // W2_ROWS

    def c13(k):
        src = w1_hbm if k % 2 == 0 else w3_hbm
        r0 = (k // 2) * W13_ROWS
        return pltpu.make_async_copy(src.at[layer, e, pl.ds(r0, W13_ROWS)], st13.at[k % W_RING],
                                     wsem13.at[k % W_RING])

    def c2(k):
        return pltpu.make_async_copy(w2_hbm.at[layer, e, pl.ds(k * W2_ROWS, W2_ROWS)],
                                     st2.at[k % W_RING], wsem2.at[k % W_RING])

    @pl.when(first)
    def _():
        for k in range(W_RING):
            c13(k).start(priority=1)
        for k in range(W_RING):
            c2(k).start(priority=1)
        for k in range(n13):
            c13(k).wait()
            dstw = wb1 if k % 2 == 0 else wb3
            r0 = (k // 2) * W13_ROWS
            dstw[r0:r0 + W13_ROWS, :] = st13[k % W_RING].astype(BF16)
            if k + W_RING < n13:
                c13(k + W_RING).start(priority=1)

    @pl.when(i < nt)
    def _():
        wait_gather(slot, ngrp_ref[i])
        x = xbuf[slot].astype(BF16)
        a = jnp.dot(x, wb1[...], preferred_element_type=F32)
        b = jnp.dot(x, wb3[...], preferred_element_type=F32)
        hdn = (_silu(a) * b).astype(BF16)

        @pl.when(first)
        def _():
            for k in range(n2):
                c2(k).wait()
                wb2[k * W2_ROWS:(k + 1) * W2_ROWS, :] = st2[k % W_RING].astype(BF16)
                if k + W_RING < n2:
                    c2(k + W_RING).start(priority=1)

        @pl.when(i > 0)
        def _():
            wait_scatter(ngrp_ref[jnp.maximum(i - 1, 0)])

        ybuf[...] = jnp.dot(hdn, wb2[...], preferred_element_type=F32)
        start_scatter(ngrp_ref[i])

    @pl.when(i == nt)
    def _():
        wait_scatter(ngrp_ref[jnp.maximum(i - 1, 0)])

    @pl.when(jnp.logical_and(i == nt_max - 1, i < nt))
    def _():
        wait_scatter(ngrp_ref[i])


def _experts(hn, dst, tile_expert, tile_first, tile_ngrp, n_tiles, w1, w3, w2, *, layer, tm, nt_max):
    t_rows, d = hn.shape
    ff = w1.shape[3]
    smem_spec = lambda f: pl.BlockSpec((1, 1, tm), f, memory_space=pltpu.SMEM)
    grid_spec = pltpu.PrefetchScalarGridSpec(
        num_scalar_prefetch=4,
        grid=(nt_max,),
        in_specs=[
            smem_spec(lambda i, *_: (i, 0, 0)),
            smem_spec(lambda i, *_: (jnp.minimum(i + 1, nt_max - 1), 0, 0)),
            pl.BlockSpec(memory_space=pl.ANY),
            pl.BlockSpec(memory_space=pl.ANY),
            pl.BlockSpec(memory_space=pl.ANY),
            pl.BlockSpec(memory_space=pl.ANY),
        ],
        out_specs=pl.BlockSpec(memory_space=pl.ANY),
        scratch_shapes=[
            pltpu.VMEM((2, tm, d), F32), pltpu.VMEM((tm, d), F32),
            pltpu.VMEM((d, ff), BF16), pltpu.VMEM((d, ff), BF16), pltpu.VMEM((ff, d), BF16),
            pltpu.VMEM((W_RING, W13_ROWS, ff), F32), pltpu.VMEM((W_RING, W2_ROWS, d), F32),
            pltpu.SemaphoreType.DMA((2,)), pltpu.SemaphoreType.DMA((1,)),
            pltpu.SemaphoreType.DMA((W_RING,)), pltpu.SemaphoreType.DMA((W_RING,)),
        ],
    )
    return pl.pallas_call(
        functools.partial(_expert_body, tm=tm, layer=layer, nt_max=nt_max, t_rows=t_rows),
        grid_spec=grid_spec,
        out_shape=jax.ShapeDtypeStruct((2 * t_rows + tm, d), F32),
        compiler_params=_cparams(("arbitrary",)),
        name="moe_experts",
    )(tile_expert, tile_first, tile_ngrp, n_tiles, dst, dst, hn, w1, w3, w2)


def _combine_body(h_ref, y0_ref, y1_ref, route_ref, *rest, with_norm):
    route = route_ref[...]
    h_new = h_ref[...] + route[:, 4:5] * y0_ref[...] + route[:, 5:6] * y1_ref[...]
    if with_norm:
        g_ref, o_ref, n_ref = rest
        o_ref[...] = h_new
        y = h_new * lax.rsqrt(jnp.mean(h_new * h_new, axis=-1, keepdims=True) + EPS)
        n_ref[...] = (y * g_ref[...]).astype(n_ref.dtype)
    else:
        rest[0][...] = h_new


def _combine(h, y2, route, g_next, *, tm):
    t_rows, d = h.shape
    n = t_rows // tm
    row = pl.BlockSpec((tm, d), lambda i: (i, 0))
    in_specs = [row, row,
                pl.BlockSpec((tm, d), lambda i: (n + i, 0)),
                pl.BlockSpec((tm, V7X_LANES), lambda i: (i, 0))]
    args = [h, y2, y2, route]
    with_norm = g_next is not None
    if with_norm:
        in_specs.append(pl.BlockSpec((1, d), lambda i: (0, 0)))
        args.append(g_next.reshape(1, d))
        out_specs = [row, row]
        out_shape = [jax.ShapeDtypeStruct((t_rows, d), F32), jax.ShapeDtypeStruct((t_rows, d), BF16)]
    else:
        out_specs = row
        out_shape = jax.ShapeDtypeStruct((t_rows, d), F32)
    return pl.pallas_call(
        functools.partial(_combine_body, with_norm=with_norm),
        grid=(t_rows // tm,),
        in_specs=in_specs,
        out_specs=out_specs,
        out_shape=out_shape,
        compiler_params=_cparams(("arbitrary",)),
        name="moe_combine",
    )(*args)


def _dispatch_body(cnt_ref, rt_ref, dst_ref, te_ref, tfirst_ref, ngrp_ref, nt_ref, off_ref,
                   *, tm, nt_max, t_rows):
    tm_shift = int(math.log2(tm))
    grp_shift = int(math.log2(ROW_UNROLL))

    def per_expert(e, tile0):
        c = cnt_ref[e]
        n = lax.shift_right_logical(c + (tm - 1), tm_shift)
        off_ref[e] = tile0 * tm

        def per_tile(k, carry_):
            i = tile0 + k
            te_ref[i] = e
            tfirst_ref[i] = jnp.where(k == 0, 1, 0)
            nval = jnp.minimum(c - k * tm, tm)
            ngrp_ref[i] = lax.shift_right_logical(nval + (ROW_UNROLL - 1), grp_shift)
            return carry_
        lax.fori_loop(0, n, per_tile, 0)
        for u in range(ROW_UNROLL - 1):
            r = c + u
            dst_ref[tile0 * tm + r] = 2 * t_rows + jnp.bitwise_and(r, tm - 1)
        return tile0 + n
    nt = lax.fori_loop(0, N_EXPERTS, per_expert, 0)
    nt_ref[0] = nt

    def dead(i, carry_):
        te_ref[i] = 0
        tfirst_ref[i] = 0
        ngrp_ref[i] = 0
        return carry_
    lax.fori_loop(nt, nt_max, dead, 0)

    def per_token(t, carry_):
        dst_ref[off_ref[rt_ref[t]] + rt_ref[2 * t_rows + t]] = t
        dst_ref[off_ref[rt_ref[t_rows + t]] + rt_ref[3 * t_rows + t]] = t_rows + t
        return carry_
    lax.fori_loop(0, t_rows, per_token, 0, unroll=ROW_UNROLL)


def _dispatch(cnt, rt, *, tm, nt_max, t_rows):
    smem = pl.BlockSpec(memory_space=pltpu.SMEM)
    return pl.pallas_call(
        functools.partial(_dispatch_body, tm=tm, nt_max=nt_max, t_rows=t_rows),
        in_specs=[smem, smem],
        out_specs=[smem] * 5,
        out_shape=[jax.ShapeDtypeStruct((nt_max * tm,), I32), jax.ShapeDtypeStruct((nt_max,), I32),
                   jax.ShapeDtypeStruct((nt_max,), I32), jax.ShapeDtypeStruct((nt_max,), I32),
                   jax.ShapeDtypeStruct((1,), I32)],
        scratch_shapes=[pltpu.SMEM((N_EXPERTS,), I32)],
        name="moe_dispatch",
    )(cnt, rt)


def _moe(h, layer, g, w_rg, b_rg, w_re, b_re, w1, w3, w2, g_next, *, tm_tok, tm_e):
    t_rows, d = h.shape
    pad = V7X_LANES - N_EXPERTS - MOE_GROUPS
    wr = jnp.concatenate([w_re[layer], w_rg[layer], jnp.zeros((d, pad), F32)], axis=1)
    br = jnp.concatenate([b_re[layer], b_rg[layer], jnp.zeros((pad,), F32)]).reshape(1, V7X_LANES)
    hn, route, counts = _router(h, g, wr, br, tm_tok)

    nt_max = (2 * t_rows) // tm_e + N_EXPERTS
    cnt = counts[0, :N_EXPERTS].astype(I32)
    rt = route[:, :4].astype(I32).T.reshape(4 * t_rows)
    dst, te, tfirst, ngrp, n_tiles = _dispatch(cnt, rt, tm=tm_e, nt_max=nt_max, t_rows=t_rows)

    y2 = _experts(hn, dst.reshape(nt_max, 1, tm_e), te, tfirst, ngrp, n_tiles, w1, w3, w2,
                  layer=layer, tm=tm_e, nt_max=nt_max)
    return _combine(h, y2, route, g_next, tm=_pick(t_rows, (256, 128)))


def _conv_sample(u, buf, w, b):
    full = jnp.concatenate([buf, u], axis=1)
    l = u.shape[1]
    y = b + w[0] * full[:, 0:l]
    for t in range(1, CONV_W):
        y = y + w[t] * full[:, t:t + l]
    return y, full[:, -(CONV_W - 1):]


def _pair_blocks(w):
    nblk, bw, _ = w.shape
    w = w.reshape(nblk // 2, 2, bw, bw)
    z = jnp.zeros((nblk // 2, bw, bw), w.dtype)
    top = jnp.concatenate([w[:, 0], z], axis=2)
    bot = jnp.concatenate([z, w[:, 1]], axis=2)
    return jnp.concatenate([top, bot], axis=1).astype(BF16)


def kernel(x_prompt, x_sample, state_ssd_conv, state_ssd, state_ret, state_lru_conv, state_lru, meta_tokens, norm_mix, norm_ffn, norm_final, w_in0, ssd_conv_w, ssd_conv_b, ssd_dt_bias, ssd_a_log, ssd_d, ssd_norm_g, w_out0, w_in1, lru_conv_w, lru_conv_b, lru_wa, lru_ba, lru_wx, lru_bx, lru_lambda, w_out1, moe_w_rg, moe_b_rg, moe_w_re, moe_b_re, moe_w1, moe_w3, moe_w2):
    nb, seq, d = x_prompt.shape
    nbs, ls, _ = x_sample.shape
    assert seq % ROWS == 0 and ROWS % ls == 0 and (nbs * ls) % ROWS == 0 and ls >= CONV_W - 1
    n_xblk = seq // ROWS
    blk_p = n_xblk + 1
    lp_pad = blk_p * ROWS
    tp = nb * lp_pad
    ts = nbs * ls
    t_all = tp + ts
    nbs_blk = ts // ROWS

    def rb_prompt(b, c):
        return b * blk_p + lax.rem(c + n_xblk, blk_p)

    def rb_sample(b, c):
        return nb * blk_p + b

    def rb_own(b, c):
        return b

    def _last_prompt_rows(p, c0, c1):
        return jnp.stack([p[b * lp_pad + seq - (CONV_W - 1):b * lp_pad + seq, c0:c1] for b in range(nb)])

    prompt_kw = dict(nb=nb, n_chunks=blk_p, seg_len=ROWS, first_valid=N_META, row_block=rb_prompt)
    sample_kw = dict(nb=nbs_blk, n_chunks=1, seg_len=ls, first_valid=ROWS, row_block=rb_sample)

    h, hn = _embed_norm(x_prompt, x_sample, meta_tokens, norm_mix[0])

    tm_tok = _pick(t_all, (512, 256, 128))
    tm_mm = _pick(t_all, (1216, 608, 512, 256, 128))

    d_inner = SSD_GROUPS * GROUP_W
    conv_dim = d_inner + 2 * SSD_GROUPS * SSD_STATE
    n_heads = SSD_GROUPS * SSD_HPG
    c_zx = d_inner + conv_dim
    c_qkvg = w_in0.shape[1] - c_zx - n_heads
    w_in0_t = w_in0.T
    p1 = _matmul_t(hn, w_in0_t, 0, c_zx, tm_mm, 512)
    pdt = _matmul_t(hn, w_in0_t, c_zx, n_heads, tm_mm, n_heads)
    p3 = _matmul_t(hn, w_in0_t, c_zx + n_heads, c_qkvg, tm_mm, 512)

    dt = jax.nn.softplus(pdt + ssd_dt_bias)
    la = dt * (-jnp.exp(ssd_a_log))
    dt3 = dt.reshape(t_all, SSD_GROUPS, SSD_HPG)
    la3 = la.reshape(t_all, SSD_GROUPS, SSD_HPG)
    dtc = dt3.transpose(1, 0, 2)[:, None]
    lac = la3.transpose(1, 0, 2)[:, None]
    lar = la3.transpose(1, 2, 0)[:, None]
    dx = jnp.repeat(ssd_d, SSD_HEAD_DIM).reshape(SSD_GROUPS, 1, GROUP_W)
    ng = ssd_norm_g.reshape(SSD_GROUPS, 1, GROUP_W)

    xbc_s, conv_s = _conv_sample(p1[tp:, d_inner:].reshape(nbs, ls, conv_dim), state_ssd_conv,
                                 ssd_conv_w, ssd_conv_b)
    xbc_s = jax.nn.silu(xbc_s).reshape(ts, conv_dim)
    conv_p = _last_prompt_rows(p1, d_inner, c_zx)

    mix_w = d_inner + RET_HEADS * RET_DIM
    ssd_common = (p1, dtc, lac, lar, dx, ng)
    mix, ssd_p = _ssd_scan(p1, d_inner, rb_prompt, *ssd_common,
                           (ssd_conv_w, ssd_conv_b.reshape(1, conv_dim)), None, None,
                           gps=SSD_GPS_PROMPT, out_width=mix_w, **prompt_kw)
    mix, ssd_s = _ssd_scan(xbc_s, 0, rb_own, *ssd_common, None, jnp.swapaxes(state_ssd, 2, 3), mix,
                           gps=SSD_GPS_SAMPLE, out_width=mix_w, **sample_kw)

    half = RET_DIM // 2
    inv = 1.0 / (ROPE_BASE ** (jnp.arange(half, dtype=F32) / half))
    pos_p = jnp.concatenate([N_META + jnp.arange(seq, dtype=I32), jnp.arange(N_META, dtype=I32),
                             jnp.zeros((lp_pad - seq - N_META,), I32)])
    pos_s = PAST_LEN + (jnp.arange(ROWS, dtype=I32) % ls)
    ang = jnp.concatenate([pos_p, pos_s]).astype(F32)[:, None] * inv[None, :]
    cos, sin = jnp.cos(ang), jnp.sin(ang)
    log_gamma = jnp.log1p(-jnp.exp2(-5.0 - jnp.arange(RET_HEADS, dtype=F32)))
    lg = jnp.broadcast_to(log_gamma[:, None, None], (RET_HEADS, 1, V7X_LANES))
    ret_kw = dict(out_width=mix_w, out_col0=d_inner)
    mix, ret_p = _ret_scan(p3, cos, sin, lg, None, mix, hb=RET_HB_PROMPT,
                           cs_block=lambda b, c: lax.rem(c + n_xblk, blk_p), **ret_kw, **prompt_kw)
    mix, ret_s = _ret_scan(p3, cos, sin, lg, state_ret, mix, hb=RET_HB_SAMPLE, cs_block=lambda b, c: blk_p,
                           **ret_kw, **sample_kw)
    tm_o = _pick(t_all, (608, 512, 256, 128))
    h = _matmul(mix, w_out0.astype(BF16), d, tm_o, 512, res=h)
    moe_kw = dict(tm_tok=tm_tok, tm_e=min(MOE_TM, tm_tok))
    moe_w = (moe_w_rg, moe_b_rg, moe_w_re, moe_b_re, moe_w1, moe_w3, moe_w2)
    h, hn = _moe(h, 0, norm_ffn[0], *moe_w, norm_mix[1], **moe_kw)

    width = lru_lambda.shape[0]
    p_lru = _matmul(hn, w_in1, 2 * width, tm_mm, 512)
    xc_s, lconv_s = _conv_sample(p_lru[tp:, width:].reshape(nbs, ls, width), state_lru_conv,
                                 lru_conv_w, lru_conv_b)
    xc_s = xc_s.reshape(ts, width)
    lconv_p = _last_prompt_rows(p_lru, width, 2 * width)
    n_pairs = width // PAIR_W
    lru_common = (p_lru, _pair_blocks(lru_wa), _pair_blocks(lru_wx), lru_ba.reshape(n_pairs, 1, PAIR_W),
                  lru_bx.reshape(n_pairs, 1, PAIR_W),
                  jax.nn.softplus(-lru_lambda).reshape(n_pairs, 1, PAIR_W))
    y_lru, lru_p = _lru_scan(p_lru, width, rb_prompt, *lru_common,
                             (lru_conv_w, lru_conv_b.reshape(1, width)), jnp.zeros((nb, 1, width), F32),
                             None, mark_pos0=True, **prompt_kw)
    y_lru, lru_s = _lru_scan(xc_s, 0, rb_own, *lru_common, None,
                             state_lru.reshape(nbs_blk, ROWS // ls, width), y_lru, mark_pos0=False,
                             **sample_kw)
    h = _matmul(y_lru, w_out1, d, tm_o, 512, res=h)
    h = _moe(h, 1, norm_ffn[1], *moe_w, None, **moe_kw)

    y_prompt = _rmsnorm(h, norm_final, F32, ROWS, n_out_blocks=nb * n_xblk,
                        in_block=lambda i: (i // n_xblk) * blk_p + lax.rem(i, n_xblk))
    y_sample = _rmsnorm(h, norm_final, F32, ROWS, n_out_blocks=nbs_blk, in_block=lambda i: nb * blk_p + i)
    return (y_prompt.reshape(nb, seq, d), y_sample.reshape(nbs, ls, d),
            conv_p, jnp.swapaxes(ssd_p, 2, 3), ret_p, lconv_p, lru_p.reshape(nb, width),
            conv_s, jnp.swapaxes(ssd_s, 2, 3), ret_s, lconv_s, lru_s.reshape(nbs, width))
```

```python
import functools
import math

import jax
import jax.numpy as jnp
from jax import lax
from jax.experimental import pallas as pl
from jax.experimental.pallas import tpu as pltpu

F32, BF16, I32 = jnp.float32, jnp.bfloat16, jnp.int32

N_META = 16
CONV_W = 4
EPS = 1e-6
F32_TINY = 1.1754944e-38
PAST_LEN = 16384
SSD_HEAD_DIM = 64
SSD_GROUPS = 8
SSD_HPG = 8
SSD_STATE = 128
RET_HEADS = 16
RET_DIM = 256
RET_HB_PROMPT = 8
RET_HB_SAMPLE = 2
ROPE_BASE = 10000.0
LRU_C = 8.0
MOE_GROUPS = 4
MOE_PER_GROUP = 8
N_EXPERTS = MOE_GROUPS * MOE_PER_GROUP

V7X_LANES = 128
V7X_SUBLANES = 8
V7X_VMEM_LIMIT = 56 * 1024 * 1024
ROWS = 128
GROUP_W = SSD_HPG * SSD_HEAD_DIM
SSD_GPS_PROMPT = 8
SSD_GPS_SAMPLE = 2
XBC_W = GROUP_W + 2 * SSD_STATE
PAIR_W = 2 * 320
LRU_PPS = 8
MOE_TM = 512
W_RING = 4
W13_ROWS = 512
W2_ROWS = 64
ROW_UNROLL = 8


def _cparams(sem):
    return pltpu.CompilerParams(dimension_semantics=sem, vmem_limit_bytes=V7X_VMEM_LIMIT)


def _pick(n, cands):
    for c in cands:
        if n % c == 0:
            return c
    raise ValueError(f"no tile for {n} in {cands}")


def _sigmoid(x):
    return 0.5 * jnp.tanh(0.5 * x) + 0.5


def _silu(x):
    return x * _sigmoid(x)


def _rmsnorm_body(x_ref, g_ref, o_ref):
    x = x_ref[...]
    y = x * lax.rsqrt(jnp.mean(x * x, axis=-1, keepdims=True) + EPS)
    o_ref[...] = (y * g_ref[...]).astype(o_ref.dtype)


def _rmsnorm(x, g, out_dtype, tm, n_out_blocks=None, in_block=None):
    m, d = x.shape
    n_blocks = m // tm if n_out_blocks is None else n_out_blocks
    in_map = (lambda i: (i, 0)) if in_block is None else (lambda i: (in_block(i), 0))
    return pl.pallas_call(
        _rmsnorm_body,
        grid=(n_blocks,),
        in_specs=[pl.BlockSpec((tm, d), in_map), pl.BlockSpec((1, d), lambda i: (0, 0))],
        out_specs=pl.BlockSpec((tm, d), lambda i: (i, 0)),
        out_shape=jax.ShapeDtypeStruct((n_blocks * tm, d), out_dtype),
        compiler_params=_cparams(("arbitrary",)),
        name="rmsnorm",
    )(x, g.reshape(1, d))


def _embed_body(xp_ref, xs_ref, meta_ref, g_ref, h_ref, hn_ref, *, nb, blk_p, n_xblk):
    i = pl.program_id(0)
    k = lax.rem(i, blk_p)
    is_prompt = i < nb * blk_p

    def emit(x):
        h_ref[...] = x
        y = x * lax.rsqrt(jnp.mean(x * x, axis=-1, keepdims=True) + EPS)
        hn_ref[...] = (y * g_ref[...]).astype(hn_ref.dtype)

    @pl.when(jnp.logical_and(is_prompt, k < n_xblk))
    def _():
        emit(xp_ref[...])

    @pl.when(jnp.logical_and(is_prompt, k == n_xblk))
    def _():
        meta = meta_ref[...]
        emit(jnp.concatenate([meta, jnp.zeros((ROWS - meta.shape[0], meta.shape[1]), F32)], axis=0))

    @pl.when(jnp.logical_not(is_prompt))
    def _():
        emit(xs_ref[...])


def _embed_norm(x_prompt, x_sample, meta, g):
    nb, seq, d = x_prompt.shape
    ts = x_sample.shape[0] * x_sample.shape[1]
    n_xblk = seq // ROWS
    blk_p = n_xblk + 1
    n_blocks = nb * blk_p + ts // ROWS
    row = pl.BlockSpec((ROWS, d), lambda i: (i, 0))
    return pl.pallas_call(
        functools.partial(_embed_body, nb=nb, blk_p=blk_p, n_xblk=n_xblk),
        grid=(n_blocks,),
        in_specs=[
            pl.BlockSpec((ROWS, d), lambda i: (jnp.minimum(i // blk_p, nb - 1) * n_xblk
                                               + jnp.minimum(lax.rem(i, blk_p), n_xblk - 1), 0)),
            pl.BlockSpec((ROWS, d), lambda i: (jnp.clip(i - nb * blk_p, 0, ts // ROWS - 1), 0)),
            pl.BlockSpec(meta.shape, lambda i: (0, 0)),
            pl.BlockSpec((1, d), lambda i: (0, 0)),
        ],
        out_specs=[row, row],
        out_shape=[jax.ShapeDtypeStruct((n_blocks * ROWS, d), F32),
                   jax.ShapeDtypeStruct((n_blocks * ROWS, d), BF16)],
        compiler_params=_cparams(("arbitrary",)),
        name="embed_norm",
    )(x_prompt.reshape(nb * seq, d), x_sample.reshape(ts, d), meta, g.reshape(1, d))


def _mm_body(*refs, has_res, cast_b):
    a_ref, b_ref = refs[0], refs[1]
    r_ref = refs[2] if has_res else None
    o_ref = refs[3] if has_res else refs[2]
    if cast_b:
        bs_ref = refs[-1]

        @pl.when(pl.program_id(1) == 0)
        def _():
            bs_ref[...] = b_ref[...].astype(BF16)

        b = bs_ref[...]
    else:
        b = b_ref[...]
    acc = jnp.dot(a_ref[...], b, preferred_element_type=F32)
    if has_res:
        acc = acc + r_ref[...]
    o_ref[...] = acc


def _matmul(a, b, n_cols, tm, tn, res=None):
    m, k = a.shape
    cast_b = b.dtype != BF16
    in_specs = [pl.BlockSpec((tm, k), lambda j, i: (i, 0)),
                pl.BlockSpec((k, tn), lambda j, i: (0, j))]
    args = [a, b]
    if res is not None:
        in_specs.append(pl.BlockSpec((tm, tn), lambda j, i: (i, j)))
        args.append(res)
    return pl.pallas_call(
        functools.partial(_mm_body, has_res=res is not None, cast_b=cast_b),
        grid=(n_cols // tn, m // tm),
        in_specs=in_specs,
        out_specs=pl.BlockSpec((tm, tn), lambda j, i: (i, j)),
        out_shape=jax.ShapeDtypeStruct((m, n_cols), F32),
        scratch_shapes=[pltpu.VMEM((k, tn), BF16)] if cast_b else [],
        compiler_params=_cparams(("arbitrary", "arbitrary")),
        name="proj",
    )(*args)


def _mm_t_body(a_ref, bt_ref, o_ref, bs_ref):
    @pl.when(pl.program_id(1) == 0)
    def _():
        bs_ref[...] = bt_ref[...].astype(BF16)

    o_ref[...] = lax.dot_general(a_ref[...], bs_ref[...], (((1,), (1,)), ((), ())),
                                 preferred_element_type=F32)


def _matmul_t(a, bt, row0, n_cols, tm, tn):
    m, k = a.shape
    return pl.pallas_call(
        _mm_t_body,
        grid=(n_cols // tn, m // tm),
        in_specs=[pl.BlockSpec((tm, k), lambda j, i: (i, 0)),
                  pl.BlockSpec((pl.Element(tn), pl.Element(k)),
                               lambda j, i: (pl.multiple_of(row0 + j * tn, math.gcd(row0, tn)), 0))],
        out_specs=pl.BlockSpec((tm, tn), lambda j, i: (i, j)),
        out_shape=jax.ShapeDtypeStruct((m, n_cols), F32),
        scratch_shapes=[pltpu.VMEM((tn, k), BF16)],
        compiler_params=_cparams(("arbitrary", "arbitrary")),
        name="proj_t",
    )(a, bt)


def _split3(a):
    hi = a.astype(BF16)
    r1 = a - hi.astype(F32)
    mid = r1.astype(BF16)
    lo = (r1 - mid.astype(F32)).astype(BF16)
    return hi, mid, lo


def _dot_a01(a, m01, terms=3):
    hi, mid, lo = _split3(a)
    out = jnp.dot(hi, m01, preferred_element_type=F32) + jnp.dot(mid, m01, preferred_element_type=F32)
    if terms == 3:
        out = out + jnp.dot(lo, m01, preferred_element_type=F32)
    return out


def _dot_01a(m01, a):
    hi, mid, lo = _split3(a)
    return (jnp.dot(m01, hi, preferred_element_type=F32)
            + jnp.dot(m01, mid, preferred_element_type=F32)
            + jnp.dot(m01, lo, preferred_element_type=F32))


def _seg_masks(seg_len):
    ti = lax.broadcasted_iota(I32, (ROWS, ROWS), 0)
    si = lax.broadcasted_iota(I32, (ROWS, ROWS), 1)
    if seg_len == ROWS:
        causal = si <= ti
        causal_t = ti <= si
        sel = si == ROWS - 1
    else:
        shift = int(math.log2(seg_len))
        tseg = lax.shift_right_logical(ti, shift)
        sseg = lax.shift_right_logical(si, shift)
        same = tseg == sseg
        causal = jnp.logical_and(si <= ti, same)
        causal_t = jnp.logical_and(ti <= si, same)
        sel = si == lax.shift_left(tseg, shift) + (seg_len - 1)
    return causal, causal_t, sel


def _conv_block(u, ext_scr, w, b, c, first_valid):
    halo = V7X_SUBLANES

    @pl.when(c == 0)
    def _():
        ext_scr[0:halo, :] = jnp.zeros((halo, u.shape[1]), F32)

    ext_scr[halo:halo + ROWS, :] = u
    y = b + w[0:1] * ext_scr[pl.ds(halo - 3, ROWS), :]
    y = y + w[1:2] * ext_scr[pl.ds(halo - 2, ROWS), :]
    y = y + w[2:3] * ext_scr[pl.ds(halo - 1, ROWS), :]
    y = y + w[3:4] * u

    @pl.when(c == 0)
    def _():
        ext_scr[0:halo, :] = ext_scr[first_valid:first_valid + halo, :]

    @pl.when(c != 0)
    def _():
        ext_scr[0:halo, :] = ext_scr[ROWS:ROWS + halo, :]

    return y


def _alias_prev(in_specs, args, y_prev):
    if y_prev is None:
        return {}
    in_specs.append(pl.BlockSpec(memory_space=pl.ANY))
    args.append(y_prev)
    return {len(args) - 1: 0}


def _ssd_body(xs_ref, bm_ref, cm_ref, z_ref, dtc_ref, lac_ref, lar_ref, dx_ref, ng_ref, *rest,
              seg_len, n_chunks, first_valid, has_s0, has_prev, conv):
    rest = list(rest)
    cw = [rest.pop(0) for _ in range(6)] if conv else None
    s0_ref = rest.pop(0) if has_s0 else None
    if has_prev:
        rest.pop(0)
    y_ref, sf_ref, s_scr = rest[:3]
    nseg = ROWS // seg_len
    ngr = xs_ref.shape[1] // GROUP_W
    c = pl.program_id(2)

    @pl.when(c == 0)
    def _():
        for i in range(nseg):
            for gi in range(ngr):
                if has_s0:
                    s_scr[i, gi] = jnp.concatenate([s0_ref[i, gi * SSD_HPG + j] for j in range(SSD_HPG)],
                                                   axis=0)
                else:
                    s_scr[i, gi] = jnp.zeros((GROUP_W, SSD_STATE), F32)

    if conv:
        u = jnp.concatenate([xs_ref[...], bm_ref[...], cm_ref[...]], axis=1)
        w = jnp.concatenate([cw[0][...], cw[1][...], cw[2][...]], axis=1)
        b = jnp.concatenate([cw[3][...], cw[4][...], cw[5][...]], axis=1)
        act = _silu(_conv_block(u, rest[3], w, b, c, first_valid))
        xs_all = act[:, :ngr * GROUP_W]
        bm_all = act[:, ngr * GROUP_W:ngr * (GROUP_W + SSD_STATE)]
        cm_all = act[:, ngr * (GROUP_W + SSD_STATE):]
    else:
        xs_all, bm_all, cm_all = xs_ref[...], bm_ref[...], cm_ref[...]

    nvalid = jnp.where(c == 0, first_valid, ROWS) if first_valid != ROWS else ROWS
    row_c = lax.broadcasted_iota(I32, (ROWS, 1), 0)
    row_r = lax.broadcasted_iota(I32, (1, ROWS), 1)
    valid = row_c < nvalid
    valid_r = row_r < nvalid
    causal, causal_t, sel = _seg_masks(seg_len)
    tril = causal.astype(BF16)
    triu = causal_t.astype(BF16)
    e_i = lax.broadcasted_iota(I32, (SSD_HPG, GROUP_W), 0)
    e_l = lax.shift_right_logical(lax.broadcasted_iota(I32, (SSD_HPG, GROUP_W), 1), 6)
    expand = (e_i == e_l).astype(BF16)
    lane_head = lax.shift_right_logical(lax.broadcasted_iota(I32, (1, GROUP_W), 1), 6)

    for gi in range(ngr):
        gcols = slice(gi * GROUP_W, (gi + 1) * GROUP_W)
        ncols = slice(gi * SSD_STATE, (gi + 1) * SSD_STATE)
        xs = jnp.where(valid, xs_all[:, gcols], 0.0)
        bm = jnp.where(valid, bm_all[:, ncols], 0.0).astype(BF16)
        cm = jnp.where(valid, cm_all[:, ncols], 0.0).astype(BF16)
        z = jnp.where(valid, z_ref[:, gcols], 0.0)
        dt = jnp.where(valid, dtc_ref[gi, 0], 0.0)
        la = jnp.where(valid, lac_ref[gi, 0], 0.0)
        la_r = jnp.where(valid_r, lar_ref[gi, 0], 0.0)

        cum = _dot_01a(tril, la)
        cum_r = _dot_a01(la_r, triu)
        if seg_len == ROWS:
            cum_last = jnp.broadcast_to(cum[ROWS - 1:ROWS, :], (ROWS, SSD_HPG))
        else:
            cum_last = _dot_01a(sel.astype(BF16), cum)
        ecum_x = _dot_a01(jnp.exp(cum), expand)
        v = xs * _dot_a01(dt, expand, terms=2)
        kvw = (v * _dot_a01(jnp.exp(cum_last - cum), expand, terms=2)).astype(BF16)

        scores = lax.dot_general(cm, bm, (((1,), (1,)), ((), ())), preferred_element_type=F32)
        y = jnp.zeros((ROWS, GROUP_W), F32)
        for j in range(SSD_HPG):
            dj = jnp.exp(jnp.where(causal, cum[:, j:j + 1] - cum_r[j:j + 1, :], -jnp.inf))
            pj = (scores * dj).astype(BF16)
            vj = jnp.where(lane_head == j, v, 0.0).astype(BF16)
            y = y + jnp.dot(pj, vj, preferred_element_type=F32)

        y_inter = []
        for i in range(nseg):
            r0 = i * seg_len
            r_last = r0 + seg_len - 1
            s_old = s_scr[i, gi]
            y_inter.append(lax.dot_general(cm[r0:r0 + seg_len], s_old.astype(BF16),
                                           (((1,), (1,)), ((), ())), preferred_element_type=F32))
            cs = lax.dot_general(kvw[r0:r0 + seg_len], bm[r0:r0 + seg_len], (((0,), (0,)), ((), ())),
                                 preferred_element_type=F32)
            e_last = jnp.exp(cum_r[:, r_last:r_last + 1])
            dec = jnp.concatenate([jnp.broadcast_to(e_last[j:j + 1, :], (SSD_HEAD_DIM, SSD_STATE))
                                   for j in range(SSD_HPG)], axis=0)
            s_scr[i, gi] = dec * s_old + cs
        y_inter = y_inter[0] if nseg == 1 else jnp.concatenate(y_inter, axis=0)
        y = y + y_inter * ecum_x
        y = y + dx_ref[gi] * xs
        y = y * _silu(z)
        y = y * lax.rsqrt(jnp.mean(y * y, axis=-1, keepdims=True) + EPS) * ng_ref[gi]
        y_ref[:, gcols] = y.astype(y_ref.dtype)

    @pl.when(c == n_chunks - 1)
    def _():
        for i in range(nseg):
            for gi in range(ngr):
                s_fin = s_scr[i, gi]
                for j in range(SSD_HPG):
                    sf_ref[i, gi * SSD_HPG + j] = s_fin[j * SSD_HEAD_DIM:(j + 1) * SSD_HEAD_DIM, :]


def _ssd_scan(xbc_src, xbc_col0, data_block, p1, dtc, lac, lar, dx, ng, conv_wb, s0, y_prev, *, gps,
              out_width, nb, n_chunks, seg_len, first_valid, row_block):
    t_rows = p1.shape[0]
    nseg = ROWS // seg_len
    conv = conv_wb is not None
    d_inner = SSD_GROUPS * GROUP_W
    gw = gps * GROUP_W
    sw = gps * SSD_STATE
    x_blk0 = xbc_col0 // gw
    b_blk0 = (xbc_col0 + d_inner) // sw
    c_blk0 = b_blk0 + SSD_GROUPS // gps
    aux = lambda shape, f: pl.BlockSpec((gps,) + shape, f)
    in_specs = [
        pl.BlockSpec((ROWS, gw), lambda b, g, c: (data_block(b, c), x_blk0 + g)),
        pl.BlockSpec((ROWS, sw), lambda b, g, c: (data_block(b, c), b_blk0 + g)),
        pl.BlockSpec((ROWS, sw), lambda b, g, c: (data_block(b, c), c_blk0 + g)),
        pl.BlockSpec((ROWS, gw), lambda b, g, c: (row_block(b, c), g)),
        aux((1, ROWS, SSD_HPG), lambda b, g, c: (g, 0, row_block(b, c), 0)),
        aux((1, ROWS, SSD_HPG), lambda b, g, c: (g, 0, row_block(b, c), 0)),
        aux((1, SSD_HPG, ROWS), lambda b, g, c: (g, 0, 0, row_block(b, c))),
        aux((1, GROUP_W), lambda b, g, c: (g, 0, 0)),
        aux((1, GROUP_W), lambda b, g, c: (g, 0, 0)),
    ]
    args = [xbc_src, xbc_src, xbc_src, p1, dtc, lac, lar, dx, ng]
    scratch = [pltpu.VMEM((nseg, gps, GROUP_W, SSD_STATE), F32)]
    if conv:
        cw, cb = conv_wb
        wb_blk0 = d_inner // sw
        for arr, rows in ((cw, CONV_W), (cb, 1)):
            in_specs += [pl.BlockSpec((rows, gw), lambda b, g, c: (0, g)),
                         pl.BlockSpec((rows, sw), lambda b, g, c: (0, wb_blk0 + g)),
                         pl.BlockSpec((rows, sw), lambda b, g, c: (0, wb_blk0 + SSD_GROUPS // gps + g))]
            args += [arr, arr, arr]
        scratch.append(pltpu.VMEM((ROWS + V7X_SUBLANES, gps * XBC_W), F32))
    state_spec = pl.BlockSpec((nseg, gps * SSD_HPG, SSD_HEAD_DIM, SSD_STATE), lambda b, g, c: (b, g, 0, 0))
    if s0 is not None:
        in_specs.append(state_spec)
        args.append(s0)
    aliases = _alias_prev(in_specs, args, y_prev)
    n_heads = SSD_GROUPS * SSD_HPG
    return pl.pallas_call(
        functools.partial(_ssd_body, seg_len=seg_len, n_chunks=n_chunks, first_valid=first_valid,
                          has_s0=s0 is not None, has_prev=y_prev is not None, conv=conv),
        grid=(nb, SSD_GROUPS // gps, n_chunks),
        in_specs=in_specs,
        out_specs=[pl.BlockSpec((ROWS, gw), lambda b, g, c: (row_block(b, c), g)), state_spec],
        out_shape=[jax.ShapeDtypeStruct((t_rows, out_width), BF16),
                   jax.ShapeDtypeStruct((nb * nseg, n_heads, SSD_HEAD_DIM, SSD_STATE), F32)],
        scratch_shapes=scratch,
        input_output_aliases=aliases,
        compiler_params=_cparams(("arbitrary", "arbitrary", "arbitrary")),
        name="ssd_scan",
    )(*args)


def _ret_body(q_ref, k_ref, v_ref, g_ref, cos_ref, sin_ref, lg_ref, *rest,
              seg_len, n_chunks, first_valid, has_s0, has_prev):
    s0_ref = rest[0] if has_s0 else None
    y_ref, sf_ref, s_scr = rest[int(has_s0) + int(has_prev):]
    nseg = ROWS // seg_len
    half = RET_DIM // 2
    hb = q_ref.shape[1] // RET_DIM
    c = pl.program_id(2)

    @pl.when(c == 0)
    def _():
        for i in range(nseg):
            for hh in range(hb):
                if has_s0:
                    s_scr[i, hh] = s0_ref[i, hh]
                else:
                    s_scr[i, hh] = jnp.zeros((RET_DIM, RET_DIM), F32)

    nvalid = jnp.where(c == 0, first_valid, ROWS) if first_valid != ROWS else ROWS
    row_c = lax.broadcasted_iota(I32, (ROWS, 1), 0)
    row_r = lax.broadcasted_iota(I32, (1, ROWS), 1)
    valid = row_c < nvalid
    causal, _, _ = _seg_masks(seg_len)
    pos_c = jnp.minimum(jnp.bitwise_and(row_c, seg_len - 1) + 1, nvalid).astype(F32)
    pos_r = jnp.minimum(jnp.bitwise_and(row_r, seg_len - 1) + 1, nvalid).astype(F32)
    last = jnp.minimum(seg_len, nvalid).astype(F32) if first_valid != ROWS else float(seg_len)
    cos = cos_ref[...]
    sin = sin_ref[...]

    def rot(x):
        x1, x2 = x[:, :half], x[:, half:]
        return jnp.concatenate([x1 * cos - x2 * sin, x1 * sin + x2 * cos], axis=1)

    for hh in range(hb):
        cols = slice(hh * RET_DIM, (hh + 1) * RET_DIM)
        lg_c = lg_ref[hh][:, 0:1]
        cum = pos_c * lg_c
        cum_r = pos_r * lg_c
        cum_last = last * lg_c
        q = rot(jnp.where(valid, q_ref[:, cols], 0.0)).astype(BF16)
        k = (rot(jnp.where(valid, k_ref[:, cols], 0.0)) * (RET_DIM ** -0.5)).astype(BF16)
        v = jnp.where(valid, v_ref[:, cols], 0.0)
        gate = jnp.where(valid, g_ref[:, cols], 0.0)

        decay = jnp.exp(jnp.where(causal, cum - cum_r, -jnp.inf))
        scores = lax.dot_general(q, k, (((1,), (1,)), ((), ())), preferred_element_type=F32)
        y = jnp.dot((scores * decay).astype(BF16), v.astype(BF16), preferred_element_type=F32)
        kvw = (v * jnp.exp(cum_last - cum)).astype(BF16)
        ecum = jnp.exp(cum)
        e_last = jnp.exp(cum_last)

        y_inter = []
        for i in range(nseg):
            r0 = i * seg_len
            s_old = s_scr[i, hh]
            y_inter.append(jnp.dot(q[r0:r0 + seg_len], s_old.astype(BF16), preferred_element_type=F32))
            cs = lax.dot_general(k[r0:r0 + seg_len], kvw[r0:r0 + seg_len], (((0,), (0,)), ((), ())),
                                 preferred_element_type=F32)
            s_scr[i, hh] = e_last * s_old + cs
        y_inter = y_inter[0] if nseg == 1 else jnp.concatenate(y_inter, axis=0)
        y = y + y_inter * ecum
        y = y * lax.rsqrt(jnp.mean(y * y, axis=-1, keepdims=True) + EPS)
        y = y * _silu(gate)
        y_ref[:, cols] = y.astype(y_ref.dtype)

    @pl.when(c == n_chunks - 1)
    def _():
        for i in range(nseg):
            for hh in range(hb):
                sf_ref[i, hh] = s_scr[i, hh]


def _ret_scan(p3, cos, sin, lg, s0, y_prev, *, hb, out_width, out_col0, nb, n_chunks, seg_len, first_valid,
              row_block, cs_block):
    t_rows = p3.shape[0]
    nseg = ROWS // seg_len
    half = RET_DIM // 2
    wide = hb * RET_DIM
    hblk = RET_HEADS // hb
    out_blk0 = out_col0 // wide
    in_specs = [
        pl.BlockSpec((ROWS, wide), lambda b, h, c: (row_block(b, c), h)),
        pl.BlockSpec((ROWS, wide), lambda b, h, c: (row_block(b, c), hblk + h)),
        pl.BlockSpec((ROWS, wide), lambda b, h, c: (row_block(b, c), 2 * hblk + h)),
        pl.BlockSpec((ROWS, wide), lambda b, h, c: (row_block(b, c), 3 * hblk + h)),
        pl.BlockSpec((ROWS, half), lambda b, h, c: (cs_block(b, c), 0)),
        pl.BlockSpec((ROWS, half), lambda b, h, c: (cs_block(b, c), 0)),
        pl.BlockSpec((hb, 1, V7X_LANES), lambda b, h, c: (h, 0, 0)),
    ]
    args = [p3, p3, p3, p3, cos, sin, lg]
    state_spec = pl.BlockSpec((nseg, hb, RET_DIM, RET_DIM), lambda b, h, c: (b, h, 0, 0))
    if s0 is not None:
        in_specs.append(state_spec)
        args.append(s0)
    aliases = _alias_prev(in_specs, args, y_prev)
    return pl.pallas_call(
        functools.partial(_ret_body, seg_len=seg_len, n_chunks=n_chunks, first_valid=first_valid,
                          has_s0=s0 is not None, has_prev=y_prev is not None),
        grid=(nb, hblk, n_chunks),
        in_specs=in_specs,
        out_specs=[pl.BlockSpec((ROWS, wide), lambda b, h, c: (row_block(b, c), out_blk0 + h)),
                   state_spec],
        out_shape=[jax.ShapeDtypeStruct((t_rows, out_width), BF16),
                   jax.ShapeDtypeStruct((nb * nseg, RET_HEADS, RET_DIM, RET_DIM), F32)],
        scratch_shapes=[pltpu.VMEM((nseg, hb, RET_DIM, RET_DIM), F32)],
        input_output_aliases=aliases,
        compiler_params=_cparams(("arbitrary", "arbitrary", "arbitrary")),
        name="ret_scan",
    )(*args)


def _lru_body(x_ref, gate_ref, wa_ref, wx_ref, ba_ref, bx_ref, sp_ref, h0_ref, *rest,
              seg_len, n_chunks, first_valid, mark_pos0, has_prev, conv):
    rest = list(rest)
    cw = [rest.pop(0) for _ in range(2)] if conv else None
    if has_prev:
        rest.pop(0)
    y_ref, hl_ref, carry = rest[:3]
    c = pl.program_id(2)
    nvalid = jnp.where(c == 0, first_valid, ROWS) if first_valid != ROWS else ROWS
    row_c = lax.broadcasted_iota(I32, (ROWS, 1), 0)
    valid = row_c < nvalid

    if conv:
        x_in = _conv_block(x_ref[...], rest[3], cw[0][...], cw[1][...], c, first_valid)
    else:
        x_in = x_ref[...]
    x = jnp.where(valid, x_in, 0.0)
    xb = x.astype(BF16)
    npp = x.shape[1] // PAIR_W
    ra, ri = [], []
    for pp in range(npp):
        xbp = xb[:, pp * PAIR_W:(pp + 1) * PAIR_W]
        ra.append(jnp.dot(xbp, wa_ref[pp], preferred_element_type=F32) + ba_ref[pp])
        ri.append(jnp.dot(xbp, wx_ref[pp], preferred_element_type=F32) + bx_ref[pp])
    r = _sigmoid(jnp.concatenate(ra, axis=1))
    ig = _sigmoid(jnp.concatenate(ri, axis=1))
    log_a = -LRU_C * r * jnp.concatenate([sp_ref[pp] for pp in range(npp)], axis=1)
    a = jnp.exp(log_a)
    m2 = -jnp.tanh(log_a) * (a * a + 1.0)
    mult = m2 * lax.rsqrt(jnp.maximum(m2, F32_TINY))
    if mark_pos0:
        mult = jnp.where(jnp.logical_and(c == 0, row_c == 0), 1.0, mult)
    bterm = mult * ig * x
    a = jnp.where(valid, a, 1.0)
    bterm = jnp.where(valid, bterm, 0.0)

    sub = lax.broadcasted_iota(I32, (V7X_SUBLANES, x.shape[1]), 0)
    if seg_len == ROWS:
        @pl.when(c == 0)
        def _():
            carry[...] = h0_ref[...]
        h_prev = carry[...]
    tiles = []
    for t in range(ROWS // V7X_SUBLANES):
        at = a[8 * t:8 * t + 8]
        bt = bterm[8 * t:8 * t + 8]
        for d in (1, 2, 4):
            a_sh = jnp.where(sub >= d, pltpu.roll(at, d, 0), 1.0)
            b_sh = jnp.where(sub >= d, pltpu.roll(bt, d, 0), 0.0)
            bt = at * b_sh + bt
            at = at * a_sh
        if seg_len != ROWS:
            h_prev = h0_ref[t:t + 1, :]
        ht = bt + at * h_prev
        h_prev = ht[7:8, :]
        if seg_len != ROWS:
            hl_ref[t:t + 1, :] = h_prev
        tiles.append(ht)
    h = jnp.concatenate(tiles, axis=0)
    if seg_len == ROWS:
        carry[...] = h_prev

        @pl.when(c == n_chunks - 1)
        def _():
            hl_ref[...] = h_prev

    g = gate_ref[...]
    gelu = 0.5 * g * (1.0 + jnp.tanh(math.sqrt(2.0 / math.pi) * (g + 0.044715 * (g * g * g))))
    y_ref[...] = (h * gelu).astype(y_ref.dtype)


def _lru_scan(x_src, x_col0, data_block, p_lru, wa_p, wx_p, ba_p, bx_p, sp_p, conv_wb, h0, y_prev, *,
              nb, n_chunks, seg_len, first_valid, mark_pos0, row_block):
    t_rows = p_lru.shape[0]
    width = p_lru.shape[1] // 2
    n_pairs = width // PAIR_W
    nseg = ROWS // seg_len
    conv = conv_wb is not None
    hrows = 1 if seg_len == ROWS else nseg
    sw = LRU_PPS * PAIR_W
    x_blk0 = x_col0 // sw
    h_spec = pl.BlockSpec((None, hrows, sw), lambda b, p, c: (b, 0, p))
    vec_spec = pl.BlockSpec((LRU_PPS, 1, PAIR_W), lambda b, p, c: (p, 0, 0))
    in_specs = [
        pl.BlockSpec((ROWS, sw), lambda b, p, c: (data_block(b, c), x_blk0 + p)),
        pl.BlockSpec((ROWS, sw), lambda b, p, c: (row_block(b, c), p)),
        pl.BlockSpec((LRU_PPS, PAIR_W, PAIR_W), lambda b, p, c: (p, 0, 0)),
        pl.BlockSpec((LRU_PPS, PAIR_W, PAIR_W), lambda b, p, c: (p, 0, 0)),
        vec_spec, vec_spec, vec_spec, h_spec,
    ]
    args = [x_src, p_lru, wa_p, wx_p, ba_p, bx_p, sp_p, h0]
    scratch = [pltpu.VMEM((1, sw), F32)]
    if conv:
        cw, cb = conv_wb
        in_specs += [pl.BlockSpec((CONV_W, sw), lambda b, p, c: (0, p)),
                     pl.BlockSpec((1, sw), lambda b, p, c: (0, p))]
        args += [cw, cb]
        scratch.append(pltpu.VMEM((ROWS + V7X_SUBLANES, sw), F32))
    aliases = _alias_prev(in_specs, args, y_prev)
    return pl.pallas_call(
        functools.partial(_lru_body, seg_len=seg_len, n_chunks=n_chunks, first_valid=first_valid,
                          mark_pos0=mark_pos0, has_prev=y_prev is not None, conv=conv),
        grid=(nb, n_pairs // LRU_PPS, n_chunks),
        in_specs=in_specs,
        out_specs=[pl.BlockSpec((ROWS, sw), lambda b, p, c: (row_block(b, c), p)), h_spec],
        out_shape=[jax.ShapeDtypeStruct((t_rows, width), BF16),
                   jax.ShapeDtypeStruct((nb, hrows, width), F32)],
        scratch_shapes=scratch,
        input_output_aliases=aliases,
        compiler_params=_cparams(("arbitrary", "arbitrary", "arbitrary")),
        name="rglru",
    )(*args)


def _router_body(h_ref, g_ref, wr_ref, br_ref, hn_ref, route_ref, cnt_ref, carry, *, tm):
    i = pl.program_id(0)

    @pl.when(i == 0)
    def _():
        carry[...] = jnp.zeros_like(carry)

    x = h_ref[...]
    hn = x * lax.rsqrt(jnp.mean(x * x, axis=-1, keepdims=True) + EPS) * g_ref[...]
    hn_ref[...] = hn

    h_hi = hn.astype(BF16)
    h_lo = (hn - h_hi.astype(F32)).astype(BF16)
    w = wr_ref[...]
    w_hi = w.astype(BF16)
    w_lo = (w - w_hi.astype(F32)).astype(BF16)
    logits = (jnp.dot(h_hi, w_hi, preferred_element_type=F32)
              + jnp.dot(h_hi, w_lo, preferred_element_type=F32)
              + jnp.dot(h_lo, w_hi, preferred_element_type=F32)) + br_ref[...]

    lane_i = lax.broadcasted_iota(I32, (tm, V7X_LANES), 1)
    lane = lane_i.astype(F32)
    big = float(4 * V7X_LANES)
    neg = -jnp.inf
    is_g = jnp.logical_and(lane_i >= N_EXPERTS, lane_i < N_EXPERTS + MOE_GROUPS)
    glog = jnp.where(is_g, logits, neg)
    gmax = jnp.max(glog, axis=-1, keepdims=True)
    gsel = jnp.min(jnp.where(glog == gmax, lane, big), axis=-1, keepdims=True) - float(N_EXPERTS)
    gsum = jnp.sum(jnp.exp(glog - gmax), axis=-1, keepdims=True)
    pg = 1.0 / gsum
    lo = gsel * float(MOE_PER_GROUP)
    in_grp = jnp.logical_and(lane >= lo, lane < lo + MOE_PER_GROUP)
    elog = jnp.where(in_grp, logits, neg)
    emax = jnp.max(elog, axis=-1, keepdims=True)
    eexp = jnp.exp(elog - emax)
    ep = eexp / jnp.sum(eexp, axis=-1, keepdims=True)
    ep = jnp.where(in_grp, ep, -1.0)
    v1 = jnp.max(ep, axis=-1, keepdims=True)
    i1 = jnp.min(jnp.where(ep == v1, lane, big), axis=-1, keepdims=True)
    ep2 = jnp.where(lane == i1, -1.0, ep)
    v2 = jnp.max(ep2, axis=-1, keepdims=True)
    i2 = jnp.min(jnp.where(ep2 == v2, lane, big), axis=-1, keepdims=True)
    vs = v1 + v2
    w1 = v1 / vs * pg
    w2 = v2 / vs * pg

    oh1 = lane == i1
    oh2 = lane == i2
    oh = jnp.logical_or(oh1, oh2).astype(BF16)
    ti = lax.broadcasted_iota(I32, (tm, tm), 0)
    si = lax.broadcasted_iota(I32, (tm, tm), 1)
    before = (si < ti).astype(BF16)
    tot = jnp.dot(before, oh, preferred_element_type=F32) + carry[...]
    rank1 = jnp.sum(jnp.where(oh1, tot, 0.0), axis=-1, keepdims=True)
    rank2 = jnp.sum(jnp.where(oh2, tot, 0.0), axis=-1, keepdims=True)
    carry[...] = carry[...] + jnp.sum(oh.astype(F32), axis=0, keepdims=True)
    cnt_ref[...] = carry[...]

    out = jnp.where(lane_i == 0, i1, 0.0)
    out = jnp.where(lane_i == 1, i2, out)
    out = jnp.where(lane_i == 2, rank1, out)
    out = jnp.where(lane_i == 3, rank2, out)
    out = jnp.where(lane_i == 4, w1, out)
    out = jnp.where(lane_i == 5, w2, out)
    route_ref[...] = out


def _router(h, g, wr, br, tm):
    t_rows, d = h.shape
    return pl.pallas_call(
        functools.partial(_router_body, tm=tm),
        grid=(t_rows // tm,),
        in_specs=[pl.BlockSpec((tm, d), lambda i: (i, 0)),
                  pl.BlockSpec((1, d), lambda i: (0, 0)),
                  pl.BlockSpec((d, V7X_LANES), lambda i: (0, 0)),
                  pl.BlockSpec((1, V7X_LANES), lambda i: (0, 0))],
        out_specs=[pl.BlockSpec((tm, d), lambda i: (i, 0)),
                   pl.BlockSpec((tm, V7X_LANES), lambda i: (i, 0)),
                   pl.BlockSpec((1, V7X_LANES), lambda i: (0, 0))],
        out_shape=[jax.ShapeDtypeStruct((t_rows, d), F32),
                   jax.ShapeDtypeStruct((t_rows, V7X_LANES), F32),
                   jax.ShapeDtypeStruct((1, V7X_LANES), F32)],
        scratch_shapes=[pltpu.VMEM((1, V7X_LANES), F32)],
        compiler_params=_cparams(("arbitrary",)),
        name="moe_router",
    )(h, g.reshape(1, d), wr, br)


def _expert_body(te_ref, tfirst_ref, ngrp_ref, nt_ref, dst_cur, dst_nxt, hn_hbm, w1_hbm, w3_hbm, w2_hbm,
                 o_hbm, xbuf, ybuf, wb1, wb3, wb2, st13, st2, gsem, ssem, wsem13, wsem2,
                 *, tm, layer, nt_max, t_rows):
    i = pl.program_id(0)
    nt = nt_ref[0]
    slot = lax.rem(i, 2)

    def token_of(d):
        return jnp.where(d >= 2 * t_rows, 0, jnp.where(d >= t_rows, d - t_rows, d))

    def start_gather(dst_ref, s, ngrp):
        def body(r8, carry_):
            for u in range(ROW_UNROLL):
                r = r8 * ROW_UNROLL + u
                pltpu.make_async_copy(hn_hbm.at[pl.ds(token_of(dst_ref[0, 0, r]), 1)],
                                      xbuf.at[s, pl.ds(r, 1)], gsem.at[s]).start()
            return carry_
        lax.fori_loop(0, ngrp, body, 0)

    def wait_gather(s, ngrp):
        def body(r8, carry_):
            pltpu.make_async_copy(hn_hbm.at[pl.ds(0, ROW_UNROLL)], xbuf.at[s, pl.ds(0, ROW_UNROLL)],
                                  gsem.at[s]).wait()
            return carry_
        lax.fori_loop(0, ngrp, body, 0)

    def start_scatter(ngrp):
        def body(r8, carry_):
            for u in range(ROW_UNROLL):
                r = r8 * ROW_UNROLL + u
                pltpu.make_async_copy(ybuf.at[pl.ds(r, 1)], o_hbm.at[pl.ds(dst_cur[0, 0, r], 1)],
                                      ssem.at[0]).start()
            return carry_
        lax.fori_loop(0, ngrp, body, 0)

    def wait_scatter(ngrp):
        def body(r8, carry_):
            pltpu.make_async_copy(ybuf.at[pl.ds(0, ROW_UNROLL)], o_hbm.at[pl.ds(0, ROW_UNROLL)],
                                  ssem.at[0]).wait()
            return carry_
        lax.fori_loop(0, ngrp, body, 0)

    @pl.when(i == 0)
    def _():
        xbuf[...] = jnp.zeros_like(xbuf)
        start_gather(dst_cur, 0, ngrp_ref[0])

    @pl.when(i + 1 < nt)
    def _():
        start_gather(dst_nxt, 1 - slot, ngrp_ref[jnp.minimum(i + 1, nt_max - 1)])

    e = te_ref[i]
    first = jnp.logical_and(i < nt, tfirst_ref[i] == 1)
    n13 = 2 * (wb1.shape[0] // W13_ROWS)
    n2 = wb2.shape[0] // W2_ROWS

    def c13(k):
        src = w1_hbm if k % 2 == 0 else w3_hbm
        r0 = (k // 2) * W13_ROWS
        return pltpu.make_async_copy(src.at[layer, e, pl.ds(r0, W13_ROWS)], st13.at[k % W_RING],
                                     wsem13.at[k % W_RING])

    def c2(k):
        return pltpu.make_async_copy(w2_hbm.at[layer, e, pl.ds(k * W2_ROWS, W2_ROWS)],
                                     st2.at[k % W_RING], wsem2.at[k % W_RING])

    @pl.when(first)
    def _():
        for k in range(W_RING):
            c13(k).start(priority=1)
        for k in range(W_RING):
            c2(k).start(priority=1)
        for k in range(n13):
            c13(k).wait()
            dstw = wb1 if k % 2 == 0 else wb3
            r0 = (k // 2) * W13_ROWS
            dstw[r0:r0 + W13_ROWS, :] = st13[k % W_RING].astype(BF16)
            if k + W_RING < n13:
                c13(k + W_RING).start(priority=1)

    @pl.when(i < nt)
    def _():
        wait_gather(slot, ngrp_ref[i])
        x = xbuf[slot].astype(BF16)
        a = jnp.dot(x, wb1[...], preferred_element_type=F32)
        b = jnp.dot(x, wb3[...], preferred_element_type=F32)
        hdn = (_silu(a) * b).astype(BF16)

        @pl.when(first)
        def _():
            for k in range(n2):
                c2(k).wait()
                wb2[k * W2_ROWS:(k + 1) * W2_ROWS, :] = st2[k % W_RING].astype(BF16)
                if k + W_RING < n2:
                    c2(k + W_RING).start(priority=1)

        @pl.when(i > 0)
        def _():
            wait_scatter(ngrp_ref[jnp.maximum(i - 1, 0)])

        ybuf[...] = jnp.dot(hdn, wb2[...], preferred_element_type=F32)
        start_scatter(ngrp_ref[i])

    @pl.when(i == nt)
    def _():
        wait_scatter(ngrp_ref[jnp.maximum(i - 1, 0)])

    @pl.when(jnp.logical_and(i == nt_max - 1, i < nt))
    def _():
        wait_scatter(ngrp_ref[i])


def _experts(hn, dst, tile_expert, tile_first, tile_ngrp, n_tiles, w1, w3, w2, *, layer, tm, nt_max):
    t_rows, d = hn.shape
    ff = w1.shape[3]
    smem_spec = lambda f: pl.BlockSpec((1, 1, tm), f, memory_space=pltpu.SMEM)
    grid_spec = pltpu.PrefetchScalarGridSpec(
        num_scalar_prefetch=4,
        grid=(nt_max,),
        in_specs=[
            smem_spec(lambda i, *_: (i, 0, 0)),
            smem_spec(lambda i, *_: (jnp.minimum(i + 1, nt_max - 1), 0, 0)),
            pl.BlockSpec(memory_space=pl.ANY),
            pl.BlockSpec(memory_space=pl.ANY),
            pl.BlockSpec(memory_space=pl.ANY),
            pl.BlockSpec(memory_space=pl.ANY),
        ],
        out_specs=pl.BlockSpec(memory_space=pl.ANY),
        scratch_shapes=[
            pltpu.VMEM((2, tm, d), F32), pltpu.VMEM((tm, d), F32),
            pltpu.VMEM((d, ff), BF16), pltpu.VMEM((d, ff), BF16), pltpu.VMEM((ff, d), BF16),
            pltpu.VMEM((W_RING, W13_ROWS, ff), F32), pltpu.VMEM((W_RING, W2_ROWS, d), F32),
            pltpu.SemaphoreType.DMA((2,)), pltpu.SemaphoreType.DMA((1,)),
            pltpu.SemaphoreType.DMA((W_RING,)), pltpu.SemaphoreType.DMA((W_RING,)),
        ],
    )
    return pl.pallas_call(
        functools.partial(_expert_body, tm=tm, layer=layer, nt_max=nt_max, t_rows=t_rows),
        grid_spec=grid_spec,
        out_shape=jax.ShapeDtypeStruct((2 * t_rows + tm, d), F32),
        compiler_params=_cparams(("arbitrary",)),
        name="moe_experts",
    )(tile_expert, tile_first, tile_ngrp, n_tiles, dst, dst, hn, w1, w3, w2)


def _combine_body(h_ref, y0_ref, y1_ref, route_ref, *rest, with_norm):
    route = route_ref[...]
    h_new = h_ref[...] + route[:, 4:5] * y0_ref[...] + route[:, 5:6] * y1_ref[...]
    if with_norm:
        g_ref, o_ref, n_ref = rest
        o_ref[...] = h_new
        y = h_new * lax.rsqrt(jnp.mean(h_new * h_new, axis=-1, keepdims=True) + EPS)
        n_ref[...] = (y * g_ref[...]).astype(n_ref.dtype)
    else:
        rest[0][...] = h_new


def _combine(h, y2, route, g_next, *, tm):
    t_rows, d = h.shape
    n = t_rows // tm
    row = pl.BlockSpec((tm, d), lambda i: (i, 0))
    in_specs = [row, row,
                pl.BlockSpec((tm, d), lambda i: (n + i, 0)),
                pl.BlockSpec((tm, V7X_LANES), lambda i: (i, 0))]
    args = [h, y2, y2, route]
    with_norm = g_next is not None
    if with_norm:
        in_specs.append(pl.BlockSpec((1, d), lambda i: (0, 0)))
        args.append(g_next.reshape(1, d))
        out_specs = [row, row]
        out_shape = [jax.ShapeDtypeStruct((t_rows, d), F32), jax.ShapeDtypeStruct((t_rows, d), BF16)]
    else:
        out_specs = row
        out_shape = jax.ShapeDtypeStruct((t_rows, d), F32)
    return pl.pallas_call(
        functools.partial(_combine_body, with_norm=with_norm),
        grid=(t_rows // tm,),
        in_specs=in_specs,
        out_specs=out_specs,
        out_shape=out_shape,
        compiler_params=_cparams(("arbitrary",)),
        name="moe_combine",
    )(*args)


def _dispatch_body(cnt_ref, rt_ref, dst_ref, te_ref, tfirst_ref, ngrp_ref, nt_ref, off_ref,
                   *, tm, nt_max, t_rows):
    tm_shift = int(math.log2(tm))
    grp_shift = int(math.log2(ROW_UNROLL))

    def per_expert(e, tile0):
        c = cnt_ref[e]
        n = lax.shift_right_logical(c + (tm - 1), tm_shift)
        off_ref[e] = tile0 * tm

        def per_tile(k, carry_):
            i = tile0 + k
            te_ref[i] = e
            tfirst_ref[i] = jnp.where(k == 0, 1, 0)
            nval = jnp.minimum(c - k * tm, tm)
            ngrp_ref[i] = lax.shift_right_logical(nval + (ROW_UNROLL - 1), grp_shift)
            return carry_
        lax.fori_loop(0, n, per_tile, 0)
        for u in range(ROW_UNROLL - 1):
            r = c + u
            dst_ref[tile0 * tm + r] = 2 * t_rows + jnp.bitwise_and(r, tm - 1)
        return tile0 + n
    nt = lax.fori_loop(0, N_EXPERTS, per_expert, 0)
    nt_ref[0] = nt

    def dead(i, carry_):
        te_ref[i] = 0
        tfirst_ref[i] = 0
        ngrp_ref[i] = 0
        return carry_
    lax.fori_loop(nt, nt_max, dead, 0)

    def per_token(t, carry_):
        dst_ref[off_ref[rt_ref[t]] + rt_ref[2 * t_rows + t]] = t
        dst_ref[off_ref[rt_ref[t_rows + t]] + rt_ref[3 * t_rows + t]] = t_rows + t
        return carry_
    lax.fori_loop(0, t_rows, per_token, 0, unroll=ROW_UNROLL)


def _dispatch(cnt, rt, *, tm, nt_max, t_rows):
    smem = pl.BlockSpec(memory_space=pltpu.SMEM)
    return pl.pallas_call(
        functools.partial(_dispatch_body, tm=tm, nt_max=nt_max, t_rows=t_rows),
        in_specs=[smem, smem],
        out_specs=[smem] * 5,
        out_shape=[jax.ShapeDtypeStruct((nt_max * tm,), I32), jax.ShapeDtypeStruct((nt_max,), I32),
                   jax.ShapeDtypeStruct((nt_max,), I32), jax.ShapeDtypeStruct((nt_max,), I32),
                   jax.ShapeDtypeStruct((1,), I32)],
        scratch_shapes=[pltpu.SMEM((N_EXPERTS,), I32)],
        name="moe_dispatch",
    )(cnt, rt)


def _moe(h, layer, g, w_rg, b_rg, w_re, b_re, w1, w3, w2, g_next, *, tm_tok, tm_e):
    t_rows, d = h.shape
    pad = V7X_LANES - N_EXPERTS - MOE_GROUPS
    wr = jnp.concatenate([w_re[layer], w_rg[layer], jnp.zeros((d, pad), F32)], axis=1)
    br = jnp.concatenate([b_re[layer], b_rg[layer], jnp.zeros((pad,), F32)]).reshape(1, V7X_LANES)
    hn, route, counts = _router(h, g, wr, br, tm_tok)

    nt_max = (2 * t_rows) // tm_e + N_EXPERTS
    cnt = counts[0, :N_EXPERTS].astype(I32)
    rt = route[:, :4].astype(I32).T.reshape(4 * t_rows)
    dst, te, tfirst, ngrp, n_tiles = _dispatch(cnt, rt, tm=tm_e, nt_max=nt_max, t_rows=t_rows)

    y2 = _experts(hn, dst.reshape(nt_max, 1, tm_e), te, tfirst, ngrp, n_tiles, w1, w3, w2,
                  layer=layer, tm=tm_e, nt_max=nt_max)
    return _combine(h, y2, route, g_next, tm=_pick(t_rows, (256, 128)))


def _conv_sample(u, buf, w, b):
    full = jnp.concatenate([buf, u], axis=1)
    l = u.shape[1]
    y = b + w[0] * full[:, 0:l]
    for t in range(1, CONV_W):
        y = y + w[t] * full[:, t:t + l]
    return y, full[:, -(CONV_W - 1):]


def _pair_blocks(w):
    nblk, bw, _ = w.shape
    w = w.reshape(nblk // 2, 2, bw, bw)
    z = jnp.zeros((nblk // 2, bw, bw), w.dtype)
    top = jnp.concatenate([w[:, 0], z], axis=2)
    bot = jnp.concatenate([z, w[:, 1]], axis=2)
    return jnp.concatenate([top, bot], axis=1).astype(BF16)


def kernel(x_prompt, x_sample, state_ssd_conv, state_ssd, state_ret, state_lru_conv, state_lru, meta_tokens, norm_mix, norm_ffn, norm_final, w_in0, ssd_conv_w, ssd_conv_b, ssd_dt_bias, ssd_a_log, ssd_d, ssd_norm_g, w_out0, w_in1, lru_conv_w, lru_conv_b, lru_wa, lru_ba, lru_wx, lru_bx, lru_lambda, w_out1, moe_w_rg, moe_b_rg, moe_w_re, moe_b_re, moe_w1, moe_w3, moe_w2):
    nb, seq, d = x_prompt.shape
    nbs, ls, _ = x_sample.shape
    assert seq % ROWS == 0 and ROWS % ls == 0 and (nbs * ls) % ROWS == 0 and ls >= CONV_W - 1
    n_xblk = seq // ROWS
    blk_p = n_xblk + 1
    lp_pad = blk_p * ROWS
    tp = nb * lp_pad
    ts = nbs * ls
    t_all = tp + ts
    nbs_blk = ts // ROWS

    def rb_prompt(b, c):
        return b * blk_p + lax.rem(c + n_xblk, blk_p)

    def rb_sample(b, c):
        return nb * blk_p + b

    def rb_own(b, c):
        return b

    def _last_prompt_rows(p, c0, c1):
        return jnp.stack([p[b * lp_pad + seq - (CONV_W - 1):b * lp_pad + seq, c0:c1] for b in range(nb)])

    prompt_kw = dict(nb=nb, n_chunks=blk_p, seg_len=ROWS, first_valid=N_META, row_block=rb_prompt)
    sample_kw = dict(nb=nbs_blk, n_chunks=1, seg_len=ls, first_valid=ROWS, row_block=rb_sample)

    h, hn = _embed_norm(x_prompt, x_sample, meta_tokens, norm_mix[0])

    tm_tok = _pick(t_all, (512, 256, 128))
    tm_mm = _pick(t_all, (1216, 608, 512, 256, 128))

    d_inner = SSD_GROUPS * GROUP_W
    conv_dim = d_inner + 2 * SSD_GROUPS * SSD_STATE
    n_heads = SSD_GROUPS * SSD_HPG
    c_zx = d_inner + conv_dim
    c_qkvg = w_in0.shape[1] - c_zx - n_heads
    w_in0_t = w_in0.T
    p1 = _matmul_t(hn, w_in0_t, 0, c_zx, tm_mm, 512)
    pdt = _matmul_t(hn, w_in0_t, c_zx, n_heads, tm_mm, n_heads)
    p3 = _matmul_t(hn, w_in0_t, c_zx + n_heads, c_qkvg, tm_mm, 512)

    dt = jax.nn.softplus(pdt + ssd_dt_bias)
    la = dt * (-jnp.exp(ssd_a_log))
    dt3 = dt.reshape(t_all, SSD_GROUPS, SSD_HPG)
    la3 = la.reshape(t_all, SSD_GROUPS, SSD_HPG)
    dtc = dt3.transpose(1, 0, 2)[:, None]
    lac = la3.transpose(1, 0, 2)[:, None]
    lar = la3.transpose(1, 2, 0)[:, None]
    dx = jnp.repeat(ssd_d, SSD_HEAD_DIM).reshape(SSD_GROUPS, 1, GROUP_W)
    ng = ssd_norm_g.reshape(SSD_GROUPS, 1, GROUP_W)

    xbc_s, conv_s = _conv_sample(p1[tp:, d_inner:].reshape(nbs, ls, conv_dim), state_ssd_conv,
                                 ssd_conv_w, ssd_conv_b)
    xbc_s = jax.nn.silu(xbc_s).reshape(ts, conv_dim)
    conv_p = _last_prompt_rows(p1, d_inner, c_zx)

    mix_w = d_inner + RET_HEADS * RET_DIM
    ssd_common = (p1, dtc, lac, lar, dx, ng)
    mix, ssd_p = _ssd_scan(p1, d_inner, rb_prompt, *ssd_common,
                           (ssd_conv_w, ssd_conv_b.reshape(1, conv_dim)), None, None,
                           gps=SSD_GPS_PROMPT, out_width=mix_w, **prompt_kw)
    mix, ssd_s = _ssd_scan(xbc_s, 0, rb_own, *ssd_common, None, jnp.swapaxes(state_ssd, 2, 3), mix,
                           gps=SSD_GPS_SAMPLE, out_width=mix_w, **sample_kw)

    half = RET_DIM // 2
    inv = 1.0 / (ROPE_BASE ** (jnp.arange(half, dtype=F32) / half))
    pos_p = jnp.concatenate([N_META + jnp.arange(seq, dtype=I32), jnp.arange(N_META, dtype=I32),
                             jnp.zeros((lp_pad - seq - N_META,), I32)])
    pos_s = PAST_LEN + (jnp.arange(ROWS, dtype=I32) % ls)
    ang = jnp.concatenate([pos_p, pos_s]).astype(F32)[:, None] * inv[None, :]
    cos, sin = jnp.cos(ang), jnp.sin(ang)
    log_gamma = jnp.log1p(-jnp.exp2(-5.0 - jnp.arange(RET_HEADS, dtype=F32)))
    lg = jnp.broadcast_to(log_gamma[:, None, None], (RET_HEADS, 1, V7X_LANES))
    ret_kw = dict(out_width=mix_w, out_col0=d_inner)
    mix, ret_p = _ret_scan(p3, cos, sin, lg, None, mix, hb=RET_HB_PROMPT,
                           cs_block=lambda b, c: lax.rem(c + n_xblk, blk_p), **ret_kw, **prompt_kw)
    mix, ret_s = _ret_scan(p3, cos, sin, lg, state_ret, mix, hb=RET_HB_SAMPLE, cs_block=lambda b, c: blk_p,
                           **ret_kw, **sample_kw)
    tm_o = _pick(t_all, (608, 512, 256, 128))
    h = _matmul(mix, w_out0.astype(BF16), d, tm_o, 512, res=h)
    moe_kw = dict(tm_tok=tm_tok, tm_e=min(MOE_TM, tm_tok))
    moe_w = (moe_w_rg, moe_b_rg, moe_w_re, moe_b_re, moe_w1, moe_w3, moe_w2)
    h, hn = _moe(h, 0, norm_ffn[0], *moe_w, norm_mix[1], **moe_kw)

    width = lru_lambda.shape[0]
    p_lru = _matmul(hn, w_in1, 2 * width, tm_mm, 512)
    xc_s, lconv_s = _conv_sample(p_lru[tp:, width:].reshape(nbs, ls, width), state_lru_conv,
                                 lru_conv_w, lru_conv_b)
    xc_s = xc_s.reshape(ts, width)
    lconv_p = _last_prompt_rows(p_lru, width, 2 * width)
    n_pairs = width // PAIR_W
    lru_common = (p_lru, _pair_blocks(lru_wa), _pair_blocks(lru_wx), lru_ba.reshape(n_pairs, 1, PAIR_W),
                  lru_bx.reshape(n_pairs, 1, PAIR_W),
                  jax.nn.softplus(-lru_lambda).reshape(n_pairs, 1, PAIR_W))
    y_lru, lru_p = _lru_scan(p_lru, width, rb_prompt, *lru_common,
                             (lru_conv_w, lru_conv_b.reshape(1, width)), jnp.zeros((nb, 1, width), F32),
                             None, mark_pos0=True, **prompt_kw)
    y_lru, lru_s = _lru_scan(xc_s, 0, rb_own, *lru_common, None,
                             state_lru.reshape(nbs_blk, ROWS // ls, width), y_lru, mark_pos0=False,
                             **sample_kw)
    h = _matmul(y_lru, w_out1, d, tm_o, 512, res=h)
    h = _moe(h, 1, norm_ffn[1], *moe_w, None, **moe_kw)

    y_prompt = _rmsnorm(h, norm_final, F32, ROWS, n_out_blocks=nb * n_xblk,
                        in_block=lambda i: (i // n_xblk) * blk_p + lax.rem(i, n_xblk))
    y_sample = _rmsnorm(h, norm_final, F32, ROWS, n_out_blocks=nbs_blk, in_block=lambda i: nb * blk_p + i)
    return (y_prompt.reshape(nb, seq, d), y_sample.reshape(nbs, ls, d),
            conv_p, jnp.swapaxes(ssd_p, 2, 3), ret_p, lconv_p, lru_p.reshape(nb, width),
            conv_s, jnp.swapaxes(ssd_s, 2, 3), ret_s, lconv_s, lru_s.reshape(nbs, width))
```

```python
import functools
import math

import jax
import jax.numpy as jnp
from jax import lax
from jax.experimental import pallas as pl
from jax.experimental.pallas import tpu as pltpu

F32, BF16, I32 = jnp.float32, jnp.bfloat16, jnp.int32

N_META = 16
CONV_W = 4
EPS = 1e-6
F32_TINY = 1.1754944e-38
PAST_LEN = 16384
SSD_HEAD_DIM = 64
SSD_GROUPS = 8
SSD_HPG = 8
SSD_STATE = 128
RET_HEADS = 16
RET_DIM = 256
RET_HB_PROMPT = 8
RET_HB_SAMPLE = 2
ROPE_BASE = 10000.0
LRU_C = 8.0
MOE_GROUPS = 4
MOE_PER_GROUP = 8
N_EXPERTS = MOE_GROUPS * MOE_PER_GROUP

V7X_LANES = 128
V7X_SUBLANES = 8
V7X_VMEM_LIMIT = 56 * 1024 * 1024
ROWS = 128
GROUP_W = SSD_HPG * SSD_HEAD_DIM
SSD_GPS_PROMPT = 8
SSD_GPS_SAMPLE = 2
XBC_W = GROUP_W + 2 * SSD_STATE
PAIR_W = 2 * 320
LRU_PPS = 8
MOE_TM = 512
W_RING = 4
W13_ROWS = 512
W2_ROWS = 64
ROW_UNROLL = 8


def _cparams(sem):
    return pltpu.CompilerParams(dimension_semantics=sem, vmem_limit_bytes=V7X_VMEM_LIMIT)


def _pick(n, cands):
    for c in cands:
        if n % c == 0:
            return c
    raise ValueError(f"no tile for {n} in {cands}")


def _sigmoid(x):
    return 0.5 * jnp.tanh(0.5 * x) + 0.5


def _silu(x):
    return x * _sigmoid(x)


def _rmsnorm_body(x_ref, g_ref, o_ref):
    x = x_ref[...]
    y = x * lax.rsqrt(jnp.mean(x * x, axis=-1, keepdims=True) + EPS)
    o_ref[...] = (y * g_ref[...]).astype(o_ref.dtype)


def _rmsnorm(x, g, out_dtype, tm, n_out_blocks=None, in_block=None):
    m, d = x.shape
    n_blocks = m // tm if n_out_blocks is None else n_out_blocks
    in_map = (lambda i: (i, 0)) if in_block is None else (lambda i: (in_block(i), 0))
    return pl.pallas_call(
        _rmsnorm_body,
        grid=(n_blocks,),
        in_specs=[pl.BlockSpec((tm, d), in_map), pl.BlockSpec((1, d), lambda i: (0, 0))],
        out_specs=pl.BlockSpec((tm, d), lambda i: (i, 0)),
        out_shape=jax.ShapeDtypeStruct((n_blocks * tm, d), out_dtype),
        compiler_params=_cparams(("arbitrary",)),
        name="rmsnorm",
    )(x, g.reshape(1, d))


def _embed_body(xp_ref, xs_ref, meta_ref, g_ref, h_ref, hn_ref, *, nb, blk_p, n_xblk):
    i = pl.program_id(0)
    k = lax.rem(i, blk_p)
    is_prompt = i < nb * blk_p

    def emit(x):
        h_ref[...] = x
        y = x * lax.rsqrt(jnp.mean(x * x, axis=-1, keepdims=True) + EPS)
        hn_ref[...] = (y * g_ref[...]).astype(hn_ref.dtype)

    @pl.when(jnp.logical_and(is_prompt, k < n_xblk))
    def _():
        emit(xp_ref[...])

    @pl.when(jnp.logical_and(is_prompt, k == n_xblk))
    def _():
        meta = meta_ref[...]
        emit(jnp.concatenate([meta, jnp.zeros((ROWS - meta.shape[0], meta.shape[1]), F32)], axis=0))

    @pl.when(jnp.logical_not(is_prompt))
    def _():
        emit(xs_ref[...])


def _embed_norm(x_prompt, x_sample, meta, g):
    nb, seq, d = x_prompt.shape
    ts = x_sample.shape[0] * x_sample.shape[1]
    n_xblk = seq // ROWS
    blk_p = n_xblk + 1
    n_blocks = nb * blk_p + ts // ROWS
    row = pl.BlockSpec((ROWS, d), lambda i: (i, 0))
    return pl.pallas_call(
        functools.partial(_embed_body, nb=nb, blk_p=blk_p, n_xblk=n_xblk),
        grid=(n_blocks,),
        in_specs=[
            pl.BlockSpec((ROWS, d), lambda i: (jnp.minimum(i // blk_p, nb - 1) * n_xblk
                                               + jnp.minimum(lax.rem(i, blk_p), n_xblk - 1), 0)),
            pl.BlockSpec((ROWS, d), lambda i: (jnp.clip(i - nb * blk_p, 0, ts // ROWS - 1), 0)),
            pl.BlockSpec(meta.shape, lambda i: (0, 0)),
            pl.BlockSpec((1, d), lambda i: (0, 0)),
        ],
        out_specs=[row, row],
        out_shape=[jax.ShapeDtypeStruct((n_blocks * ROWS, d), F32),
                   jax.ShapeDtypeStruct((n_blocks * ROWS, d), BF16)],
        compiler_params=_cparams(("arbitrary",)),
        name="embed_norm",
    )(x_prompt.reshape(nb * seq, d), x_sample.reshape(ts, d), meta, g.reshape(1, d))


def _mm_body(*refs, has_res, cast_b):
    a_ref, b_ref = refs[0], refs[1]
    r_ref = refs[2] if has_res else None
    o_ref = refs[3] if has_res else refs[2]
    if cast_b:
        bs_ref = refs[-1]

        @pl.when(pl.program_id(1) == 0)
        def _():
            bs_ref[...] = b_ref[...].astype(BF16)

        b = bs_ref[...]
    else:
        b = b_ref[...]
    acc = jnp.dot(a_ref[...], b, preferred_element_type=F32)
    if has_res:
        acc = acc + r_ref[...]
    o_ref[...] = acc


def _matmul(a, b, n_cols, tm, tn, res=None):
    m, k = a.shape
    cast_b = b.dtype != BF16
    in_specs = [pl.BlockSpec((tm, k), lambda j, i: (i, 0)),
                pl.BlockSpec((k, tn), lambda j, i: (0, j))]
    args = [a, b]
    if res is not None:
        in_specs.append(pl.BlockSpec((tm, tn), lambda j, i: (i, j)))
        args.append(res)
    return pl.pallas_call(
        functools.partial(_mm_body, has_res=res is not None, cast_b=cast_b),
        grid=(n_cols // tn, m // tm),
        in_specs=in_specs,
        out_specs=pl.BlockSpec((tm, tn), lambda j, i: (i, j)),
        out_shape=jax.ShapeDtypeStruct((m, n_cols), F32),
        scratch_shapes=[pltpu.VMEM((k, tn), BF16)] if cast_b else [],
        compiler_params=_cparams(("arbitrary", "arbitrary")),
        name="proj",
    )(*args)


def _mm_t_body(a_ref, bt_ref, o_ref, bs_ref):
    @pl.when(pl.program_id(1) == 0)
    def _():
        bs_ref[...] = bt_ref[...].astype(BF16)

    o_ref[...] = lax.dot_general(a_ref[...], bs_ref[...], (((1,), (1,)), ((), ())),
                                 preferred_element_type=F32)


def _matmul_t(a, bt, row0, n_cols, tm, tn):
    m, k = a.shape
    return pl.pallas_call(
        _mm_t_body,
        grid=(n_cols // tn, m // tm),
        in_specs=[pl.BlockSpec((tm, k), lambda j, i: (i, 0)),
                  pl.BlockSpec((pl.Element(tn), pl.Element(k)),
                               lambda j, i: (pl.multiple_of(row0 + j * tn, math.gcd(row0, tn)), 0))],
        out_specs=pl.BlockSpec((tm, tn), lambda j, i: (i, j)),
        out_shape=jax.ShapeDtypeStruct((m, n_cols), F32),
        scratch_shapes=[pltpu.VMEM((tn, k), BF16)],
        compiler_params=_cparams(("arbitrary", "arbitrary")),
        name="proj_t",
    )(a, bt)


def _split3(a):
    hi = a.astype(BF16)
    r1 = a - hi.astype(F32)
    mid = r1.astype(BF16)
    lo = (r1 - mid.astype(F32)).astype(BF16)
    return hi, mid, lo


def _dot_a01(a, m01, terms=3):
    hi, mid, lo = _split3(a)
    out = jnp.dot(hi, m01, preferred_element_type=F32) + jnp.dot(mid, m01, preferred_element_type=F32)
    if terms == 3:
        out = out + jnp.dot(lo, m01, preferred_element_type=F32)
    return out


def _dot_01a(m01, a):
    hi, mid, lo = _split3(a)
    return (jnp.dot(m01, hi, preferred_element_type=F32)
            + jnp.dot(m01, mid, preferred_element_type=F32)
            + jnp.dot(m01, lo, preferred_element_type=F32))


def _seg_masks(seg_len):
    ti = lax.broadcasted_iota(I32, (ROWS, ROWS), 0)
    si = lax.broadcasted_iota(I32, (ROWS, ROWS), 1)
    if seg_len == ROWS:
        causal = si <= ti
        causal_t = ti <= si
        sel = si == ROWS - 1
    else:
        shift = int(math.log2(seg_len))
        tseg = lax.shift_right_logical(ti, shift)
        sseg = lax.shift_right_logical(si, shift)
        same = tseg == sseg
        causal = jnp.logical_and(si <= ti, same)
        causal_t = jnp.logical_and(ti <= si, same)
        sel = si == lax.shift_left(tseg, shift) + (seg_len - 1)
    return causal, causal_t, sel


def _conv_block(u, ext_scr, w, b, c, first_valid):
    halo = V7X_SUBLANES

    @pl.when(c == 0)
    def _():
        ext_scr[0:halo, :] = jnp.zeros((halo, u.shape[1]), F32)

    ext_scr[halo:halo + ROWS, :] = u
    y = b + w[0:1] * ext_scr[pl.ds(halo - 3, ROWS), :]
    y = y + w[1:2] * ext_scr[pl.ds(halo - 2, ROWS), :]
    y = y + w[2:3] * ext_scr[pl.ds(halo - 1, ROWS), :]
    y = y + w[3:4] * u

    @pl.when(c == 0)
    def _():
        ext_scr[0:halo, :] = ext_scr[first_valid:first_valid + halo, :]

    @pl.when(c != 0)
    def _():
        ext_scr[0:halo, :] = ext_scr[ROWS:ROWS + halo, :]

    return y


def _alias_prev(in_specs, args, y_prev):
    if y_prev is None:
        return {}
    in_specs.append(pl.BlockSpec(memory_space=pl.ANY))
    args.append(y_prev)
    return {len(args) - 1: 0}


def _ssd_body(xs_ref, bm_ref, cm_ref, z_ref, dtc_ref, lac_ref, lar_ref, dx_ref, ng_ref, *rest,
              seg_len, n_chunks, first_valid, has_s0, has_prev, conv):
    rest = list(rest)
    cw = [rest.pop(0) for _ in range(6)] if conv else None
    s0_ref = rest.pop(0) if has_s0 else None
    if has_prev:
        rest.pop(0)
    y_ref, sf_ref, s_scr = rest[:3]
    nseg = ROWS // seg_len
    ngr = xs_ref.shape[1] // GROUP_W
    c = pl.program_id(2)

    @pl.when(c == 0)
    def _():
        for i in range(nseg):
            for gi in range(ngr):
                if has_s0:
                    s_scr[i, gi] = jnp.concatenate([s0_ref[i, gi * SSD_HPG + j] for j in range(SSD_HPG)],
                                                   axis=0)
                else:
                    s_scr[i, gi] = jnp.zeros((GROUP_W, SSD_STATE), F32)

    if conv:
        u = jnp.concatenate([xs_ref[...], bm_ref[...], cm_ref[...]], axis=1)
        w = jnp.concatenate([cw[0][...], cw[1][...], cw[2][...]], axis=1)
        b = jnp.concatenate([cw[3][...], cw[4][...], cw[5][...]], axis=1)
        act = _silu(_conv_block(u, rest[3], w, b, c, first_valid))
        xs_all = act[:, :ngr * GROUP_W]
        bm_all = act[:, ngr * GROUP_W:ngr * (GROUP_W + SSD_STATE)]
        cm_all = act[:, ngr * (GROUP_W + SSD_STATE):]
    else:
        xs_all, bm_all, cm_all = xs_ref[...], bm_ref[...], cm_ref[...]

    nvalid = jnp.where(c == 0, first_valid, ROWS) if first_valid != ROWS else ROWS
    row_c = lax.broadcasted_iota(I32, (ROWS, 1), 0)
    row_r = lax.broadcasted_iota(I32, (1, ROWS), 1)
    valid = row_c < nvalid
    valid_r = row_r < nvalid
    causal, causal_t, sel = _seg_masks(seg_len)
    tril = causal.astype(BF16)
    triu = causal_t.astype(BF16)
    e_i = lax.broadcasted_iota(I32, (SSD_HPG, GROUP_W), 0)
    e_l = lax.shift_right_logical(lax.broadcasted_iota(I32, (SSD_HPG, GROUP_W), 1), 6)
    expand = (e_i == e_l).astype(BF16)
    lane_head = lax.shift_right_logical(lax.broadcasted_iota(I32, (1, GROUP_W), 1), 6)

    for gi in range(ngr):
        gcols = slice(gi * GROUP_W, (gi + 1) * GROUP_W)
        ncols = slice(gi * SSD_STATE, (gi + 1) * SSD_STATE)
        xs = jnp.where(valid, xs_all[:, gcols], 0.0)
        bm = jnp.where(valid, bm_all[:, ncols], 0.0).astype(BF16)
        cm = jnp.where(valid, cm_all[:, ncols], 0.0).astype(BF16)
        z = jnp.where(valid, z_ref[:, gcols], 0.0)
        dt = jnp.where(valid, dtc_ref[gi, 0], 0.0)
        la = jnp.where(valid, lac_ref[gi, 0], 0.0)
        la_r = jnp.where(valid_r, lar_ref[gi, 0], 0.0)

        cum = _dot_01a(tril, la)
        cum_r = _dot_a01(la_r, triu)
        if seg_len == ROWS:
            cum_last = jnp.broadcast_to(cum[ROWS - 1:ROWS, :], (ROWS, SSD_HPG))
        else:
            cum_last = _dot_01a(sel.astype(BF16), cum)
        ecum_x = _dot_a01(jnp.exp(cum), expand)
        v = xs * _dot_a01(dt, expand, terms=2)
        kvw = (v * _dot_a01(jnp.exp(cum_last - cum), expand, terms=2)).astype(BF16)

        scores = lax.dot_general(cm, bm, (((1,), (1,)), ((), ())), preferred_element_type=F32)
        y = jnp.zeros((ROWS, GROUP_W), F32)
        for j in range(SSD_HPG):
            dj = jnp.exp(jnp.where(causal, cum[:, j:j + 1] - cum_r[j:j + 1, :], -jnp.inf))
            pj = (scores * dj).astype(BF16)
            vj = jnp.where(lane_head == j, v, 0.0).astype(BF16)
            y = y + jnp.dot(pj, vj, preferred_element_type=F32)

        y_inter = []
        for i in range(nseg):
            r0 = i * seg_len
            r_last = r0 + seg_len - 1
            s_old = s_scr[i, gi]
            y_inter.append(lax.dot_general(cm[r0:r0 + seg_len], s_old.astype(BF16),
                                           (((1,), (1,)), ((), ())), preferred_element_type=F32))
            cs = lax.dot_general(kvw[r0:r0 + seg_len], bm[r0:r0 + seg_len], (((0,), (0,)), ((), ())),
                                 preferred_element_type=F32)
            e_last = jnp.exp(cum_r[:, r_last:r_last + 1])
            dec = jnp.concatenate([jnp.broadcast_to(e_last[j:j + 1, :], (SSD_HEAD_DIM, SSD_STATE))
                                   for j in range(SSD_HPG)], axis=0)
            s_scr[i, gi] = dec * s_old + cs
        y_inter = y_inter[0] if nseg == 1 else jnp.concatenate(y_inter, axis=0)
        y = y + y_inter * ecum_x
        y = y + dx_ref[gi] * xs
        y = y * _silu(z)
        y = y * lax.rsqrt(jnp.mean(y * y, axis=-1, keepdims=True) + EPS) * ng_ref[gi]
        y_ref[:, gcols] = y.astype(y_ref.dtype)

    @pl.when(c == n_chunks - 1)
    def _():
        for i in range(nseg):
            for gi in range(ngr):
                s_fin = s_scr[i, gi]
                for j in range(SSD_HPG):
                    sf_ref[i, gi * SSD_HPG + j] = s_fin[j * SSD_HEAD_DIM:(j + 1) * SSD_HEAD_DIM, :]


def _ssd_scan(xbc_src, xbc_col0, data_block, p1, dtc, lac, lar, dx, ng, conv_wb, s0, y_prev, *, gps,
              out_width, nb, n_chunks, seg_len, first_valid, row_block):
    t_rows = p1.shape[0]
    nseg = ROWS // seg_len
    conv = conv_wb is not None
    d_inner = SSD_GROUPS * GROUP_W
    gw = gps * GROUP_W
    sw = gps * SSD_STATE
    x_blk0 = xbc_col0 // gw
    b_blk0 = (xbc_col0 + d_inner) // sw
    c_blk0 = b_blk0 + SSD_GROUPS // gps
    aux = lambda shape, f: pl.BlockSpec((gps,) + shape, f)
    in_specs = [
        pl.BlockSpec((ROWS, gw), lambda b, g, c: (data_block(b, c), x_blk0 + g)),
        pl.BlockSpec((ROWS, sw), lambda b, g, c: (data_block(b, c), b_blk0 + g)),
        pl.BlockSpec((ROWS, sw), lambda b, g, c: (data_block(b, c), c_blk0 + g)),
        pl.BlockSpec((ROWS, gw), lambda b, g, c: (row_block(b, c), g)),
        aux((1, ROWS, SSD_HPG), lambda b, g, c: (g, 0, row_block(b, c), 0)),
        aux((1, ROWS, SSD_HPG), lambda b, g, c: (g, 0, row_block(b, c), 0)),
        aux((1, SSD_HPG, ROWS), lambda b, g, c: (g, 0, 0, row_block(b, c))),
        aux((1, GROUP_W), lambda b, g, c: (g, 0, 0)),
        aux((1, GROUP_W), lambda b, g, c: (g, 0, 0)),
    ]
    args = [xbc_src, xbc_src, xbc_src, p1, dtc, lac, lar, dx, ng]
    scratch = [pltpu.VMEM((nseg, gps, GROUP_W, SSD_STATE), F32)]
    if conv:
        cw, cb = conv_wb
        wb_blk0 = d_inner // sw
        for arr, rows in ((cw, CONV_W), (cb, 1)):
            in_specs += [pl.BlockSpec((rows, gw), lambda b, g, c: (0, g)),
                         pl.BlockSpec((rows, sw), lambda b, g, c: (0, wb_blk0 + g)),
                         pl.BlockSpec((rows, sw), lambda b, g, c: (0, wb_blk0 + SSD_GROUPS // gps + g))]
            args += [arr, arr, arr]
        scratch.append(pltpu.VMEM((ROWS + V7X_SUBLANES, gps * XBC_W), F32))
    state_spec = pl.BlockSpec((nseg, gps * SSD_HPG, SSD_HEAD_DIM, SSD_STATE), lambda b, g, c: (b, g, 0, 0))
    if s0 is not None:
        in_specs.append(state_spec)
        args.append(s0)
    aliases = _alias_prev(in_specs, args, y_prev)
    n_heads = SSD_GROUPS * SSD_HPG
    return pl.pallas_call(
        functools.partial(_ssd_body, seg_len=seg_len, n_chunks=n_chunks, first_valid=first_valid,
                          has_s0=s0 is not None, has_prev=y_prev is not None, conv=conv),
        grid=(nb, SSD_GROUPS // gps, n_chunks),
        in_specs=in_specs,
        out_specs=[pl.BlockSpec((ROWS, gw), lambda b, g, c: (row_block(b, c), g)), state_spec],
        out_shape=[jax.ShapeDtypeStruct((t_rows, out_width), BF16),
                   jax.ShapeDtypeStruct((nb * nseg, n_heads, SSD_HEAD_DIM, SSD_STATE), F32)],
        scratch_shapes=scratch,
        input_output_aliases=aliases,
        compiler_params=_cparams(("arbitrary", "arbitrary", "arbitrary")),
        name="ssd_scan",
    )(*args)


def _ret_body(q_ref, k_ref, v_ref, g_ref, cos_ref, sin_ref, lg_ref, *rest,
              seg_len, n_chunks, first_valid, has_s0, has_prev):
    s0_ref = rest[0] if has_s0 else None
    y_ref, sf_ref, s_scr = rest[int(has_s0) + int(has_prev):]
    nseg = ROWS // seg_len
    half = RET_DIM // 2
    hb = q_ref.shape[1] // RET_DIM
    c = pl.program_id(2)

    @pl.when(c == 0)
    def _():
        for i in range(nseg):
            for hh in range(hb):
                if has_s0:
                    s_scr[i, hh] = s0_ref[i, hh]
                else:
                    s_scr[i, hh] = jnp.zeros((RET_DIM, RET_DIM), F32)

    nvalid = jnp.where(c == 0, first_valid, ROWS) if first_valid != ROWS else ROWS
    row_c = lax.broadcasted_iota(I32, (ROWS, 1), 0)
    row_r = lax.broadcasted_iota(I32, (1, ROWS), 1)
    valid = row_c < nvalid
    causal, _, _ = _seg_masks(seg_len)
    pos_c = jnp.minimum(jnp.bitwise_and(row_c, seg_len - 1) + 1, nvalid).astype(F32)
    pos_r = jnp.minimum(jnp.bitwise_and(row_r, seg_len - 1) + 1, nvalid).astype(F32)
    last = jnp.minimum(seg_len, nvalid).astype(F32) if first_valid != ROWS else float(seg_len)
    cos = cos_ref[...]
    sin = sin_ref[...]

    def rot(x):
        x1, x2 = x[:, :half], x[:, half:]
        return jnp.concatenate([x1 * cos - x2 * sin, x1 * sin + x2 * cos], axis=1)

    for hh in range(hb):
        cols = slice(hh * RET_DIM, (hh + 1) * RET_DIM)
        lg_c = lg_ref[hh][:, 0:1]
        cum = pos_c * lg_c
        cum_r = pos_r * lg_c
        cum_last = last * lg_c
        q = rot(jnp.where(valid, q_ref[:, cols], 0.0)).astype(BF16)
        k = (rot(jnp.where(valid, k_ref[:, cols], 0.0)) * (RET_DIM ** -0.5)).astype(BF16)
        v = jnp.where(valid, v_ref[:, cols], 0.0)
        gate = jnp.where(valid, g_ref[:, cols], 0.0)

        decay = jnp.exp(jnp.where(causal, cum - cum_r, -jnp.inf))
        scores = lax.dot_general(q, k, (((1,), (1,)), ((), ())), preferred_element_type=F32)
        y = jnp.dot((scores * decay).astype(BF16), v.astype(BF16), preferred_element_type=F32)
        kvw = (v * jnp.exp(cum_last - cum)).astype(BF16)
        ecum = jnp.exp(cum)
        e_last = jnp.exp(cum_last)

        y_inter = []
        for i in range(nseg):
            r0 = i * seg_len
            s_old = s_scr[i, hh]
            y_inter.append(jnp.dot(q[r0:r0 + seg_len], s_old.astype(BF16), preferred_element_type=F32))
            cs = lax.dot_general(k[r0:r0 + seg_len], kvw[r0:r0 + seg_len], (((0,), (0,)), ((), ())),
                                 preferred_element_type=F32)
            s_scr[i, hh] = e_last * s_old + cs
        y_inter = y_inter[0] if nseg == 1 else jnp.concatenate(y_inter, axis=0)
        y = y + y_inter * ecum
        y = y * lax.rsqrt(jnp.mean(y * y, axis=-1, keepdims=True) + EPS)
        y = y * _silu(gate)
        y_ref[:, cols] = y.astype(y_ref.dtype)

    @pl.when(c == n_chunks - 1)
    def _():
        for i in range(nseg):
            for hh in range(hb):
                sf_ref[i, hh] = s_scr[i, hh]


def _ret_scan(p3, cos, sin, lg, s0, y_prev, *, hb, out_width, out_col0, nb, n_chunks, seg_len, first_valid,
              row_block, cs_block):
    t_rows = p3.shape[0]
    nseg = ROWS // seg_len
    half = RET_DIM // 2
    wide = hb * RET_DIM
    hblk = RET_HEADS // hb
    out_blk0 = out_col0 // wide
    in_specs = [
        pl.BlockSpec((ROWS, wide), lambda b, h, c: (row_block(b, c), h)),
        pl.BlockSpec((ROWS, wide), lambda b, h, c: (row_block(b, c), hblk + h)),
        pl.BlockSpec((ROWS, wide), lambda b, h, c: (row_block(b, c), 2 * hblk + h)),
        pl.BlockSpec((ROWS, wide), lambda b, h, c: (row_block(b, c), 3 * hblk + h)),
        pl.BlockSpec((ROWS, half), lambda b, h, c: (cs_block(b, c), 0)),
        pl.BlockSpec((ROWS, half), lambda b, h, c: (cs_block(b, c), 0)),
        pl.BlockSpec((hb, 1, V7X_LANES), lambda b, h, c: (h, 0, 0)),
    ]
    args = [p3, p3, p3, p3, cos, sin, lg]
    state_spec = pl.BlockSpec((nseg, hb, RET_DIM, RET_DIM), lambda b, h, c: (b, h, 0, 0))
    if s0 is not None:
        in_specs.append(state_spec)
        args.append(s0)
    aliases = _alias_prev(in_specs, args, y_prev)
    return pl.pallas_call(
        functools.partial(_ret_body, seg_len=seg_len, n_chunks=n_chunks, first_valid=first_valid,
                          has_s0=s0 is not None, has_prev=y_prev is not None),
        grid=(nb, hblk, n_chunks),
        in_specs=in_specs,
        out_specs=[pl.BlockSpec((ROWS, wide), lambda b, h, c: (row_block(b, c), out_blk0 + h)),
                   state_spec],
        out_shape=[jax.ShapeDtypeStruct((t_rows, out_width), BF16),
                   jax.ShapeDtypeStruct((nb * nseg, RET_HEADS, RET_DIM, RET_DIM), F32)],
        scratch_shapes=[pltpu.VMEM((nseg, hb, RET_DIM, RET_DIM), F32)],
        input_output_aliases=aliases,
        compiler_params=_cparams(("arbitrary", "arbitrary", "arbitrary")),
        name="ret_scan",
    )(*args)


def _lru_body(x_ref, gate_ref, wa_ref, wx_ref, ba_ref, bx_ref, sp_ref, h0_ref, *rest,
              seg_len, n_chunks, first_valid, mark_pos0, has_prev, conv):
    rest = list(rest)
    cw = [rest.pop(0) for _ in range(2)] if conv else None
    if has_prev:
        rest.pop(0)
    y_ref, hl_ref, carry = rest[:3]
    c = pl.program_id(2)
    nvalid = jnp.where(c == 0, first_valid, ROWS) if first_valid != ROWS else ROWS
    row_c = lax.broadcasted_iota(I32, (ROWS, 1), 0)
    valid = row_c < nvalid

    if conv:
        x_in = _conv_block(x_ref[...], rest[3], cw[0][...], cw[1][...], c, first_valid)
    else:
        x_in = x_ref[...]
    x = jnp.where(valid, x_in, 0.0)
    xb = x.astype(BF16)
    npp = x.shape[1] // PAIR_W
    ra, ri = [], []
    for pp in range(npp):
        xbp = xb[:, pp * PAIR_W:(pp + 1) * PAIR_W]
        ra.append(jnp.dot(xbp, wa_ref[pp], preferred_element_type=F32) + ba_ref[pp])
        ri.append(jnp.dot(xbp, wx_ref[pp], preferred_element_type=F32) + bx_ref[pp])
    r = _sigmoid(jnp.concatenate(ra, axis=1))
    ig = _sigmoid(jnp.concatenate(ri, axis=1))
    log_a = -LRU_C * r * jnp.concatenate([sp_ref[pp] for pp in range(npp)], axis=1)
    a = jnp.exp(log_a)
    m2 = -jnp.tanh(log_a) * (a * a + 1.0)
    mult = m2 * lax.rsqrt(jnp.maximum(m2, F32_TINY))
    if mark_pos0:
        mult = jnp.where(jnp.logical_and(c == 0, row_c == 0), 1.0, mult)
    bterm = mult * ig * x
    a = jnp.where(valid, a, 1.0)
    bterm = jnp.where(valid, bterm, 0.0)

    sub = lax.broadcasted_iota(I32, (V7X_SUBLANES, x.shape[1]), 0)
    if seg_len == ROWS:
        @pl.when(c == 0)
        def _():
            carry[...] = h0_ref[...]
        h_prev = carry[...]
    tiles = []
    for t in range(ROWS // V7X_SUBLANES):
        at = a[8 * t:8 * t + 8]
        bt = bterm[8 * t:8 * t + 8]
        for d in (1, 2, 4):
            a_sh = jnp.where(sub >= d, pltpu.roll(at, d, 0), 1.0)
            b_sh = jnp.where(sub >= d, pltpu.roll(bt, d, 0), 0.0)
            bt = at * b_sh + bt
            at = at * a_sh
        if seg_len != ROWS:
            h_prev = h0_ref[t:t + 1, :]
        ht = bt + at * h_prev
        h_prev = ht[7:8, :]
        if seg_len != ROWS:
            hl_ref[t:t + 1, :] = h_prev
        tiles.append(ht)
    h = jnp.concatenate(tiles, axis=0)
    if seg_len == ROWS:
        carry[...] = h_prev

        @pl.when(c == n_chunks - 1)
        def _():
            hl_ref[...] = h_prev

    g = gate_ref[...]
    gelu = 0.5 * g * (1.0 + jnp.tanh(math.sqrt(2.0 / math.pi) * (g + 0.044715 * (g * g * g))))
    y_ref[...] = (h * gelu).astype(y_ref.dtype)


def _lru_scan(x_src, x_col0, data_block, p_lru, wa_p, wx_p, ba_p, bx_p, sp_p, conv_wb, h0, y_prev, *,
              nb, n_chunks, seg_len, first_valid, mark_pos0, row_block):
    t_rows = p_lru.shape[0]
    width = p_lru.shape[1] // 2
    n_pairs = width // PAIR_W
    nseg = ROWS // seg_len
    conv = conv_wb is not None
    hrows = 1 if seg_len == ROWS else nseg
    sw = LRU_PPS * PAIR_W
    x_blk0 = x_col0 // sw
    h_spec = pl.BlockSpec((None, hrows, sw), lambda b, p, c: (b, 0, p))
    vec_spec = pl.BlockSpec((LRU_PPS, 1, PAIR_W), lambda b, p, c: (p, 0, 0))
    in_specs = [
        pl.BlockSpec((ROWS, sw), lambda b, p, c: (data_block(b, c), x_blk0 + p)),
        pl.BlockSpec((ROWS, sw), lambda b, p, c: (row_block(b, c), p)),
        pl.BlockSpec((LRU_PPS, PAIR_W, PAIR_W), lambda b, p, c: (p, 0, 0)),
        pl.BlockSpec((LRU_PPS, PAIR_W, PAIR_W), lambda b, p, c: (p, 0, 0)),
        vec_spec, vec_spec, vec_spec, h_spec,
    ]
    args = [x_src, p_lru, wa_p, wx_p, ba_p, bx_p, sp_p, h0]
    scratch = [pltpu.VMEM((1, sw), F32)]
    if conv:
        cw, cb = conv_wb
        in_specs += [pl.BlockSpec((CONV_W, sw), lambda b, p, c: (0, p)),
                     pl.BlockSpec((1, sw), lambda b, p, c: (0, p))]
        args += [cw, cb]
        scratch.append(pltpu.VMEM((ROWS + V7X_SUBLANES, sw), F32))
    aliases = _alias_prev(in_specs, args, y_prev)
    return pl.pallas_call(
        functools.partial(_lru_body, seg_len=seg_len, n_chunks=n_chunks, first_valid=first_valid,
                          mark_pos0=mark_pos0, has_prev=y_prev is not None, conv=conv),
        grid=(nb, n_pairs // LRU_PPS, n_chunks),
        in_specs=in_specs,
        out_specs=[pl.BlockSpec((ROWS, sw), lambda b, p, c: (row_block(b, c), p)), h_spec],
        out_shape=[jax.ShapeDtypeStruct((t_rows, width), BF16),
                   jax.ShapeDtypeStruct((nb, hrows, width), F32)],
        scratch_shapes=scratch,
        input_output_aliases=aliases,
        compiler_params=_cparams(("arbitrary", "arbitrary", "arbitrary")),
        name="rglru",
    )(*args)


def _router_body(h_ref, g_ref, wr_ref, br_ref, hn_ref, route_ref, cnt_ref, carry, *, tm):
    i = pl.program_id(0)

    @pl.when(i == 0)
    def _():
        carry[...] = jnp.zeros_like(carry)

    x = h_ref[...]
    hn = x * lax.rsqrt(jnp.mean(x * x, axis=-1, keepdims=True) + EPS) * g_ref[...]
    hn_ref[...] = hn

    h_hi = hn.astype(BF16)
    h_lo = (hn - h_hi.astype(F32)).astype(BF16)
    w = wr_ref[...]
    w_hi = w.astype(BF16)
    w_lo = (w - w_hi.astype(F32)).astype(BF16)
    logits = (jnp.dot(h_hi, w_hi, preferred_element_type=F32)
              + jnp.dot(h_hi, w_lo, preferred_element_type=F32)
              + jnp.dot(h_lo, w_hi, preferred_element_type=F32)) + br_ref[...]

    lane_i = lax.broadcasted_iota(I32, (tm, V7X_LANES), 1)
    lane = lane_i.astype(F32)
    big = float(4 * V7X_LANES)
    neg = -jnp.inf
    is_g = jnp.logical_and(lane_i >= N_EXPERTS, lane_i < N_EXPERTS + MOE_GROUPS)
    glog = jnp.where(is_g, logits, neg)
    gmax = jnp.max(glog, axis=-1, keepdims=True)
    gsel = jnp.min(jnp.where(glog == gmax, lane, big), axis=-1, keepdims=True) - float(N_EXPERTS)
    gsum = jnp.sum(jnp.exp(glog - gmax), axis=-1, keepdims=True)
    pg = 1.0 / gsum
    lo = gsel * float(MOE_PER_GROUP)
    in_grp = jnp.logical_and(lane >= lo, lane < lo + MOE_PER_GROUP)
    elog = jnp.where(in_grp, logits, neg)
    emax = jnp.max(elog, axis=-1, keepdims=True)
    eexp = jnp.exp(elog - emax)
    ep = eexp / jnp.sum(eexp, axis=-1, keepdims=True)
    ep = jnp.where(in_grp, ep, -1.0)
    v1 = jnp.max(ep, axis=-1, keepdims=True)
    i1 = jnp.min(jnp.where(ep == v1, lane, big), axis=-1, keepdims=True)
    ep2 = jnp.where(lane == i1, -1.0, ep)
    v2 = jnp.max(ep2, axis=-1, keepdims=True)
    i2 = jnp.min(jnp.where(ep2 == v2, lane, big), axis=-1, keepdims=True)
    vs = v1 + v2
    w1 = v1 / vs * pg
    w2 = v2 / vs * pg

    oh1 = lane == i1
    oh2 = lane == i2
    oh = jnp.logical_or(oh1, oh2).astype(BF16)
    ti = lax.broadcasted_iota(I32, (tm, tm), 0)
    si = lax.broadcasted_iota(I32, (tm, tm), 1)
    before = (si < ti).astype(BF16)
    tot = jnp.dot(before, oh, preferred_element_type=F32) + carry[...]
    rank1 = jnp.sum(jnp.where(oh1, tot, 0.0), axis=-1, keepdims=True)
    rank2 = jnp.sum(jnp.where(oh2, tot, 0.0), axis=-1, keepdims=True)
    carry[...] = carry[...] + jnp.sum(oh.astype(F32), axis=0, keepdims=True)
    cnt_ref[...] = carry[...]

    out = jnp.where(lane_i == 0, i1, 0.0)
    out = jnp.where(lane_i == 1, i2, out)
    out = jnp.where(lane_i == 2, rank1, out)
    out = jnp.where(lane_i == 3, rank2, out)
    out = jnp.where(lane_i == 4, w1, out)
    out = jnp.where(lane_i == 5, w2, out)
    route_ref[...] = out


def _router(h, g, wr, br, tm):
    t_rows, d = h.shape
    return pl.pallas_call(
        functools.partial(_router_body, tm=tm),
        grid=(t_rows // tm,),
        in_specs=[pl.BlockSpec((tm, d), lambda i: (i, 0)),
                  pl.BlockSpec((1, d), lambda i: (0, 0)),
                  pl.BlockSpec((d, V7X_LANES), lambda i: (0, 0)),
                  pl.BlockSpec((1, V7X_LANES), lambda i: (0, 0))],
        out_specs=[pl.BlockSpec((tm, d), lambda i: (i, 0)),
                   pl.BlockSpec((tm, V7X_LANES), lambda i: (i, 0)),
                   pl.BlockSpec((1, V7X_LANES), lambda i: (0, 0))],
        out_shape=[jax.ShapeDtypeStruct((t_rows, d), F32),
                   jax.ShapeDtypeStruct((t_rows, V7X_LANES), F32),
                   jax.ShapeDtypeStruct((1, V7X_LANES), F32)],
        scratch_shapes=[pltpu.VMEM((1, V7X_LANES), F32)],
        compiler_params=_cparams(("arbitrary",)),
        name="moe_router",
    )(h, g.reshape(1, d), wr, br)


def _expert_body(te_ref, tfirst_ref, ngrp_ref, nt_ref, dst_cur, dst_nxt, hn_hbm, w1_hbm, w3_hbm, w2_hbm,
                 o_hbm, xbuf, ybuf, wb13, wb2, st13, st2, gsem, ssem, wsem13, wsem2,
                 *, tm, layer, nt_max, t_rows):
    i = pl.program_id(0)
    nt = nt_ref[0]
    slot = lax.rem(i, 2)

    def token_of(d):
        return jnp.where(d >= 2 * t_rows, 0, jnp.where(d >= t_rows, d - t_rows, d))

    def start_gather(dst_ref, s, ngrp):
        def body(r8, carry_):
            for u in range(ROW_UNROLL):
                r = r8 * ROW_UNROLL + u
                pltpu.make_async_copy(hn_hbm.at[pl.ds(token_of(dst_ref[0, 0, r]), 1)],
                                      xbuf.at[s, pl.ds(r, 1)], gsem.at[s]).start()
            return carry_
        lax.fori_loop(0, ngrp, body, 0)

    def wait_gather(s, ngrp):
        def body(r8, carry_):
            pltpu.make_async_copy(hn_hbm.at[pl.ds(0, ROW_UNROLL)], xbuf.at[s, pl.ds(0, ROW_UNROLL)],
                                  gsem.at[s]).wait()
            return carry_
        lax.fori_loop(0, ngrp, body, 0)

    def start_scatter(ngrp):
        def body(r8, carry_):
            for u in range(ROW_UNROLL):
                r = r8 * ROW_UNROLL + u
                pltpu.make_async_copy(ybuf.at[pl.ds(r, 1)], o_hbm.at[pl.ds(dst_cur[0, 0, r], 1)],
                                      ssem.at[0]).start(priority=1)
            return carry_
        lax.fori_loop(0, ngrp, body, 0)

    def wait_scatter(ngrp):
        def body(r8, carry_):
            pltpu.make_async_copy(ybuf.at[pl.ds(0, ROW_UNROLL)], o_hbm.at[pl.ds(0, ROW_UNROLL)],
                                  ssem.at[0]).wait()
            return carry_
        lax.fori_loop(0, ngrp, body, 0)

    @pl.when(i == 0)
    def _():
        xbuf[...] = jnp.zeros_like(xbuf)
        start_gather(dst_cur, 0, ngrp_ref[0])

    @pl.when(i + 1 < nt)
    def _():
        start_gather(dst_nxt, 1 - slot, ngrp_ref[jnp.minimum(i + 1, nt_max - 1)])

    e = te_ref[i]
    first = jnp.logical_and(i < nt, tfirst_ref[i] == 1)
    ff = wb2.shape[0]
    n13 = 2 * (wb13.shape[0] // W13_ROWS)
    n2 = wb2.shape[0] // W2_ROWS

    def c13(k):
        src = w1_hbm if k % 2 == 0 else w3_hbm
        r0 = (k // 2) * W13_ROWS
        return pltpu.make_async_copy(src.at[layer, e, pl.ds(r0, W13_ROWS)], st13.at[k % W_RING],
                                     wsem13.at[k % W_RING])

    def c2(k):
        return pltpu.make_async_copy(w2_hbm.at[layer, e, pl.ds(k * W2_ROWS, W2_ROWS)],
                                     st2.at[k % W_RING], wsem2.at[k % W_RING])

    @pl.when(first)
    def _():
        for k in range(W_RING):
            c13(k).start(priority=1)
        for k in range(W_RING):
            c2(k).start(priority=1)
        for k in range(n13):
            c13(k).wait()
            r0 = (k // 2) * W13_ROWS
            c0 = (k % 2) * ff
            wb13[r0:r0 + W13_ROWS, c0:c0 + ff] = st13[k % W_RING].astype(BF16)
            if k + W_RING < n13:
                c13(k + W_RING).start(priority=1)

    @pl.when(i < nt)
    def _():
        wait_gather(slot, ngrp_ref[i])
        x = xbuf[slot].astype(BF16)
        ab = jnp.dot(x, wb13[...], preferred_element_type=F32)
        hdn = (_silu(ab[:, :ff]) * ab[:, ff:]).astype(BF16)

        @pl.when(first)
        def _():
            for k in range(n2):
                c2(k).wait()
                wb2[k * W2_ROWS:(k + 1) * W2_ROWS, :] = st2[k % W_RING].astype(BF16)
                if k + W_RING < n2:
                    c2(k + W_RING).start(priority=1)

        @pl.when(i > 0)
        def _():
            wait_scatter(ngrp_ref[jnp.maximum(i - 1, 0)])

        ybuf[...] = jnp.dot(hdn, wb2[...], preferred_element_type=F32)
        start_scatter(ngrp_ref[i])

    @pl.when(i == nt)
    def _():
        wait_scatter(ngrp_ref[jnp.maximum(i - 1, 0)])

    @pl.when(jnp.logical_and(i == nt_max - 1, i < nt))
    def _():
        wait_scatter(ngrp_ref[i])


def _experts(hn, dst, tile_expert, tile_first, tile_ngrp, n_tiles, w1, w3, w2, *, layer, tm, nt_max):
    t_rows, d = hn.shape
    ff = w1.shape[3]
    smem_spec = lambda f: pl.BlockSpec((1, 1, tm), f, memory_space=pltpu.SMEM)
    grid_spec = pltpu.PrefetchScalarGridSpec(
        num_scalar_prefetch=4,
        grid=(nt_max,),
        in_specs=[
            smem_spec(lambda i, *_: (i, 0, 0)),
            smem_spec(lambda i, *_: (jnp.minimum(i + 1, nt_max - 1), 0, 0)),
            pl.BlockSpec(memory_space=pl.ANY),
            pl.BlockSpec(memory_space=pl.ANY),
            pl.BlockSpec(memory_space=pl.ANY),
            pl.BlockSpec(memory_space=pl.ANY),
        ],
        out_specs=pl.BlockSpec(memory_space=pl.ANY),
        scratch_shapes=[
            pltpu.VMEM((2, tm, d), F32), pltpu.VMEM((tm, d), F32),
            pltpu.VMEM((d, 2 * ff), BF16), pltpu.VMEM((ff, d), BF16),
            pltpu.VMEM((W_RING, W13_ROWS, ff), F32), pltpu.VMEM((W_RING, W2_ROWS, d), F32),
            pltpu.SemaphoreType.DMA((2,)), pltpu.SemaphoreType.DMA((1,)),
            pltpu.SemaphoreType.DMA((W_RING,)), pltpu.SemaphoreType.DMA((W_RING,)),
        ],
    )
    return pl.pallas_call(
        functools.partial(_expert_body, tm=tm, layer=layer, nt_max=nt_max, t_rows=t_rows),
        grid_spec=grid_spec,
        out_shape=jax.ShapeDtypeStruct((2 * t_rows + tm, d), F32),
        compiler_params=_cparams(("arbitrary",)),
        name="moe_experts",
    )(tile_expert, tile_first, tile_ngrp, n_tiles, dst, dst, hn, w1, w3, w2)


def _combine_body(h_ref, y0_ref, y1_ref, route_ref, *rest, with_norm):
    route = route_ref[...]
    h_new = h_ref[...] + route[:, 4:5] * y0_ref[...] + route[:, 5:6] * y1_ref[...]
    if with_norm:
        g_ref, o_ref, n_ref = rest
        o_ref[...] = h_new
        y = h_new * lax.rsqrt(jnp.mean(h_new * h_new, axis=-1, keepdims=True) + EPS)
        n_ref[...] = (y * g_ref[...]).astype(n_ref.dtype)
    else:
        rest[0][...] = h_new


def _combine(h, y2, route, g_next, *, tm):
    t_rows, d = h.shape
    n = t_rows // tm
    row = pl.BlockSpec((tm, d), lambda i: (i, 0))
    in_specs = [row, row,
                pl.BlockSpec((tm, d), lambda i: (n + i, 0)),
                pl.BlockSpec((tm, V7X_LANES), lambda i: (i, 0))]
    args = [h, y2, y2, route]
    with_norm = g_next is not None
    if with_norm:
        in_specs.append(pl.BlockSpec((1, d), lambda i: (0, 0)))
        args.append(g_next.reshape(1, d))
        out_specs = [row, row]
        out_shape = [jax.ShapeDtypeStruct((t_rows, d), F32), jax.ShapeDtypeStruct((t_rows, d), BF16)]
    else:
        out_specs = row
        out_shape = jax.ShapeDtypeStruct((t_rows, d), F32)
    return pl.pallas_call(
        functools.partial(_combine_body, with_norm=with_norm),
        grid=(t_rows // tm,),
        in_specs=in_specs,
        out_specs=out_specs,
        out_shape=out_shape,
        compiler_params=_cparams(("arbitrary",)),
        name="moe_combine",
    )(*args)


def _dispatch_body(cnt_ref, rt_ref, dst_ref, te_ref, tfirst_ref, ngrp_ref, nt_ref, off_ref,
                   *, tm, nt_max, t_rows):
    tm_shift = int(math.log2(tm))
    grp_shift = int(math.log2(ROW_UNROLL))

    def per_expert(e, tile0):
        c = cnt_ref[e]
        n = lax.shift_right_logical(c + (tm - 1), tm_shift)
        off_ref[e] = tile0 * tm

        def per_tile(k, carry_):
            i = tile0 + k
            te_ref[i] = e
            tfirst_ref[i] = jnp.where(k == 0, 1, 0)
            nval = jnp.minimum(c - k * tm, tm)
            ngrp_ref[i] = lax.shift_right_logical(nval + (ROW_UNROLL - 1), grp_shift)
            return carry_
        lax.fori_loop(0, n, per_tile, 0)
        for u in range(ROW_UNROLL - 1):
            r = c + u
            dst_ref[tile0 * tm + r] = 2 * t_rows + jnp.bitwise_and(r, tm - 1)
        return tile0 + n
    nt = lax.fori_loop(0, N_EXPERTS, per_expert, 0)
    nt_ref[0] = nt

    def dead(i, carry_):
        te_ref[i] = 0
        tfirst_ref[i] = 0
        ngrp_ref[i] = 0
        return carry_
    lax.fori_loop(nt, nt_max, dead, 0)

    def per_token(t, carry_):
        dst_ref[off_ref[rt_ref[t]] + rt_ref[2 * t_rows + t]] = t
        dst_ref[off_ref[rt_ref[t_rows + t]] + rt_ref[3 * t_rows + t]] = t_rows + t
        return carry_
    lax.fori_loop(0, t_rows, per_token, 0, unroll=ROW_UNROLL)


def _dispatch(cnt, rt, *, tm, nt_max, t_rows):
    smem = pl.BlockSpec(memory_space=pltpu.SMEM)
    return pl.pallas_call(
        functools.partial(_dispatch_body, tm=tm, nt_max=nt_max, t_rows=t_rows),
        in_specs=[smem, smem],
        out_specs=[smem] * 5,
        out_shape=[jax.ShapeDtypeStruct((nt_max * tm,), I32), jax.ShapeDtypeStruct((nt_max,), I32),
                   jax.ShapeDtypeStruct((nt_max,), I32), jax.ShapeDtypeStruct((nt_max,), I32),
                   jax.ShapeDtypeStruct((1,), I32)],
        scratch_shapes=[pltpu.SMEM((N_EXPERTS,), I32)],
        name="moe_dispatch",
    )(cnt, rt)


def _moe(h, layer, g, w_rg, b_rg, w_re, b_re, w1, w3, w2, g_next, *, tm_tok, tm_e):
    t_rows, d = h.shape
    pad = V7X_LANES - N_EXPERTS - MOE_GROUPS
    wr = jnp.concatenate([w_re[layer], w_rg[layer], jnp.zeros((d, pad), F32)], axis=1)
    br = jnp.concatenate([b_re[layer], b_rg[layer], jnp.zeros((pad,), F32)]).reshape(1, V7X_LANES)
    hn, route, counts = _router(h, g, wr, br, tm_tok)

    nt_max = (2 * t_rows) // tm_e + N_EXPERTS
    cnt = counts[0, :N_EXPERTS].astype(I32)
    rt = route[:, :4].astype(I32).T.reshape(4 * t_rows)
    dst, te, tfirst, ngrp, n_tiles = _dispatch(cnt, rt, tm=tm_e, nt_max=nt_max, t_rows=t_rows)

    y2 = _experts(hn, dst.reshape(nt_max, 1, tm_e), te, tfirst, ngrp, n_tiles, w1, w3, w2,
                  layer=layer, tm=tm_e, nt_max=nt_max)
    return _combine(h, y2, route, g_next, tm=_pick(t_rows, (256, 128)))


def _conv_sample(u, buf, w, b):
    full = jnp.concatenate([buf, u], axis=1)
    l = u.shape[1]
    y = b + w[0] * full[:, 0:l]
    for t in range(1, CONV_W):
        y = y + w[t] * full[:, t:t + l]
    return y, full[:, -(CONV_W - 1):]


def _pair_blocks(w):
    nblk, bw, _ = w.shape
    w = w.reshape(nblk // 2, 2, bw, bw)
    z = jnp.zeros((nblk // 2, bw, bw), w.dtype)
    top = jnp.concatenate([w[:, 0], z], axis=2)
    bot = jnp.concatenate([z, w[:, 1]], axis=2)
    return jnp.concatenate([top, bot], axis=1).astype(BF16)


def kernel(x_prompt, x_sample, state_ssd_conv, state_ssd, state_ret, state_lru_conv, state_lru, meta_tokens, norm_mix, norm_ffn, norm_final, w_in0, ssd_conv_w, ssd_conv_b, ssd_dt_bias, ssd_a_log, ssd_d, ssd_norm_g, w_out0, w_in1, lru_conv_w, lru_conv_b, lru_wa, lru_ba, lru_wx, lru_bx, lru_lambda, w_out1, moe_w_rg, moe_b_rg, moe_w_re, moe_b_re, moe_w1, moe_w3, moe_w2):
    nb, seq, d = x_prompt.shape
    nbs, ls, _ = x_sample.shape
    assert seq % ROWS == 0 and ROWS % ls == 0 and (nbs * ls) % ROWS == 0 and ls >= CONV_W - 1
    n_xblk = seq // ROWS
    blk_p = n_xblk + 1
    lp_pad = blk_p * ROWS
    tp = nb * lp_pad
    ts = nbs * ls
    t_all = tp + ts
    nbs_blk = ts // ROWS

    def rb_prompt(b, c):
        return b * blk_p + lax.rem(c + n_xblk, blk_p)

    def rb_sample(b, c):
        return nb * blk_p + b

    def rb_own(b, c):
        return b

    def _last_prompt_rows(p, c0, c1):
        return jnp.stack([p[b * lp_pad + seq - (CONV_W - 1):b * lp_pad + seq, c0:c1] for b in range(nb)])

    prompt_kw = dict(nb=nb, n_chunks=blk_p, seg_len=ROWS, first_valid=N_META, row_block=rb_prompt)
    sample_kw = dict(nb=nbs_blk, n_chunks=1, seg_len=ls, first_valid=ROWS, row_block=rb_sample)

    h, hn = _embed_norm(x_prompt, x_sample, meta_tokens, norm_mix[0])

    tm_tok = _pick(t_all, (512, 256, 128))
    tm_mm = _pick(t_all, (1216, 608, 512, 256, 128))

    d_inner = SSD_GROUPS * GROUP_W
    conv_dim = d_inner + 2 * SSD_GROUPS * SSD_STATE
    n_heads = SSD_GROUPS * SSD_HPG
    c_zx = d_inner + conv_dim
    c_qkvg = w_in0.shape[1] - c_zx - n_heads
    w_in0_t = w_in0.T
    p1 = _matmul_t(hn, w_in0_t, 0, c_zx, tm_mm, 512)
    pdt = _matmul_t(hn, w_in0_t, c_zx, n_heads, tm_mm, n_heads)
    p3 = _matmul_t(hn, w_in0_t, c_zx + n_heads, c_qkvg, tm_mm, 512)

    dt = jax.nn.softplus(pdt + ssd_dt_bias)
    la = dt * (-jnp.exp(ssd_a_log))
    dt3 = dt.reshape(t_all, SSD_GROUPS, SSD_HPG)
    la3 = la.reshape(t_all, SSD_GROUPS, SSD_HPG)
    dtc = dt3.transpose(1, 0, 2)[:, None]
    lac = la3.transpose(1, 0, 2)[:, None]
    lar = la3.transpose(1, 2, 0)[:, None]
    dx = jnp.repeat(ssd_d, SSD_HEAD_DIM).reshape(SSD_GROUPS, 1, GROUP_W)
    ng = ssd_norm_g.reshape(SSD_GROUPS, 1, GROUP_W)

    xbc_s, conv_s = _conv_sample(p1[tp:, d_inner:].reshape(nbs, ls, conv_dim), state_ssd_conv,
                                 ssd_conv_w, ssd_conv_b)
    xbc_s = jax.nn.silu(xbc_s).reshape(ts, conv_dim)
    conv_p = _last_prompt_rows(p1, d_inner, c_zx)

    mix_w = d_inner + RET_HEADS * RET_DIM
    ssd_common = (p1, dtc, lac, lar, dx, ng)
    mix, ssd_p = _ssd_scan(p1, d_inner, rb_prompt, *ssd_common,
                           (ssd_conv_w, ssd_conv_b.reshape(1, conv_dim)), None, None,
                           gps=SSD_GPS_PROMPT, out_width=mix_w, **prompt_kw)
    mix, ssd_s = _ssd_scan(xbc_s, 0, rb_own, *ssd_common, None, jnp.swapaxes(state_ssd, 2, 3), mix,
                           gps=SSD_GPS_SAMPLE, out_width=mix_w, **sample_kw)

    half = RET_DIM // 2
    inv = 1.0 / (ROPE_BASE ** (jnp.arange(half, dtype=F32) / half))
    pos_p = jnp.concatenate([N_META + jnp.arange(seq, dtype=I32), jnp.arange(N_META, dtype=I32),
                             jnp.zeros((lp_pad - seq - N_META,), I32)])
    pos_s = PAST_LEN + (jnp.arange(ROWS, dtype=I32) % ls)
    ang = jnp.concatenate([pos_p, pos_s]).astype(F32)[:, None] * inv[None, :]
    cos, sin = jnp.cos(ang), jnp.sin(ang)
    log_gamma = jnp.log1p(-jnp.exp2(-5.0 - jnp.arange(RET_HEADS, dtype=F32)))
    lg = jnp.broadcast_to(log_gamma[:, None, None], (RET_HEADS, 1, V7X_LANES))
    ret_kw = dict(out_width=mix_w, out_col0=d_inner)
    mix, ret_p = _ret_scan(p3, cos, sin, lg, None, mix, hb=RET_HB_PROMPT,
                           cs_block=lambda b, c: lax.rem(c + n_xblk, blk_p), **ret_kw, **prompt_kw)
    mix, ret_s = _ret_scan(p3, cos, sin, lg, state_ret, mix, hb=RET_HB_SAMPLE, cs_block=lambda b, c: blk_p,
                           **ret_kw, **sample_kw)
    tm_o = _pick(t_all, (608, 512, 256, 128))
    h = _matmul(mix, w_out0.astype(BF16), d, tm_o, 512, res=h)
    moe_kw = dict(tm_tok=tm_tok, tm_e=min(MOE_TM, tm_tok))
    moe_w = (moe_w_rg, moe_b_rg, moe_w_re, moe_b_re, moe_w1, moe_w3, moe_w2)
    h, hn = _moe(h, 0, norm_ffn[0], *moe_w, norm_mix[1], **moe_kw)

    width = lru_lambda.shape[0]
    p_lru = _matmul(hn, w_in1, 2 * width, tm_mm, 512)
    xc_s, lconv_s = _conv_sample(p_lru[tp:, width:].reshape(nbs, ls, width), state_lru_conv,
                                 lru_conv_w, lru_conv_b)
    xc_s = xc_s.reshape(ts, width)
    lconv_p = _last_prompt_rows(p_lru, width, 2 * width)
    n_pairs = width // PAIR_W
    lru_common = (p_lru, _pair_blocks(lru_wa), _pair_blocks(lru_wx), lru_ba.reshape(n_pairs, 1, PAIR_W),
                  lru_bx.reshape(n_pairs, 1, PAIR_W),
                  jax.nn.softplus(-lru_lambda).reshape(n_pairs, 1, PAIR_W))
    y_lru, lru_p = _lru_scan(p_lru, width, rb_prompt, *lru_common,
                             (lru_conv_w, lru_conv_b.reshape(1, width)), jnp.zeros((nb, 1, width), F32),
                             None, mark_pos0=True, **prompt_kw)
    y_lru, lru_s = _lru_scan(xc_s, 0, rb_own, *lru_common, None,
                             state_lru.reshape(nbs_blk, ROWS // ls, width), y_lru, mark_pos0=False,
                             **sample_kw)
    h = _matmul(y_lru, w_out1, d, tm_o, 512, res=h)
    h = _moe(h, 1, norm_ffn[1], *moe_w, None, **moe_kw)

    y_prompt = _rmsnorm(h, norm_final, F32, ROWS, n_out_blocks=nb * n_xblk,
                        in_block=lambda i: (i // n_xblk) * blk_p + lax.rem(i, n_xblk))
    y_sample = _rmsnorm(h, norm_final, F32, ROWS, n_out_blocks=nbs_blk, in_block=lambda i: nb * blk_p + i)
    return (y_prompt.reshape(nb, seq, d), y_sample.reshape(nbs, ls, d),
            conv_p, jnp.swapaxes(ssd_p, 2, 3), ret_p, lconv_p, lru_p.reshape(nb, width),
            conv_s, jnp.swapaxes(ssd_s, 2, 3), ret_s, lconv_s, lru_s.reshape(nbs, width))
```
